```python
import jax, jax.numpy as jnp
from jax import lax
import numpy as np

D_MODEL = 1024
BATCH = 32
SEQ = 2048
DEPTH = 1

PLE_DIM = 256
NSA_HEADS = 8
NSA_KV_GROUPS = 2
NSA_HPG = NSA_HEADS // NSA_KV_GROUPS
NSA_HEAD_DIM = 64
Q_DIM = NSA_HEADS * NSA_HEAD_DIM
KV_DIM = NSA_KV_GROUPS * NSA_HEAD_DIM
NSA_BRANCHES = 3
NSA_GATE_DIM = NSA_HEADS * NSA_BRANCHES
CMP_BLOCK = 32
CMP_STRIDE = 16
CMP_HIDDEN = 128
SEL_BLOCK = 64
SEL_TOPK = 16
SEL_Q_CHUNK = 32
WINDOW = 512
WIN_Q_BLOCK = 128
ATTN_SCALE = NSA_HEAD_DIM ** -0.5
FORCE_BONUS = 1e4
NEG_INF = -1e30
CONV_DIM = 512
CONV_WIDTH = 3
IN_PROJ_DIM = Q_DIM + 6 * KV_DIM + NSA_GATE_DIM + 3 * CONV_DIM + 2 * D_MODEL
N_GROUPS = 4
EXPERTS_PER_GROUP = 8
N_EXPERTS = N_GROUPS * EXPERTS_PER_GROUP
EXPERT_HIDDEN = 256
TOPK_EXPERTS = 2
LN_EPS = 1e-5
DEEPNORM_ALPHA = (2.0 * DEPTH) ** 0.25
DEEPNORM_BETA = (8.0 * DEPTH) ** -0.25

kernel_name = "hybrid_nsa_shortconv_hmoe_deepnorm"


def _layer_norm(x, g, b):
    xf = x.astype(jnp.float32)
    mu = jnp.mean(xf, axis=-1, keepdims=True)
    var = jnp.mean(jnp.square(xf - mu), axis=-1, keepdims=True)
    y = (xf - mu) * lax.rsqrt(var + LN_EPS) * g.astype(jnp.float32) + b.astype(jnp.float32)
    return y.astype(x.dtype)


def _compress(kv, pe, w1, b1, w2, b2):
    B, S, G, dk = kv.shape
    n_cmp = (S - CMP_BLOCK) // CMP_STRIDE + 1
    idx = np.arange(n_cmp)[:, None] * CMP_STRIDE + np.arange(CMP_BLOCK)[None, :]
    blk = kv[:, idx] + pe[:, None, :]
    blk = jnp.transpose(blk, (0, 1, 3, 2, 4)).reshape(B, n_cmp, G, CMP_BLOCK * dk)
    return jax.nn.gelu(blk @ w1 + b1) @ w2 + b2


def _compressed_attention(q, kc, vc):
    S = q.shape[1]
    n_cmp = kc.shape[1]
    t = jnp.arange(S)
    blk_end = jnp.arange(n_cmp) * CMP_STRIDE + CMP_BLOCK - 1
    mask = blk_end[None, :] <= t[:, None]
    s = jnp.einsum('bsghd,bngd->bghsn', q, kc).astype(jnp.float32) * ATTN_SCALE
    pr = jax.nn.softmax(jnp.where(mask, s, NEG_INF), axis=-1)
    pr = pr * jnp.any(mask, axis=-1)[:, None].astype(jnp.float32)
    o = jnp.einsum('bghsn,bngd->bsghd', pr.astype(vc.dtype), vc)
    return o, pr


def _select_blocks(p_cmp, S):
    n_cmp = p_cmp.shape[-1]
    n_sel = S // SEL_BLOCK
    c_start = np.arange(n_cmp) * CMP_STRIDE
    c_end = c_start + CMP_BLOCK - 1
    s_start = np.arange(n_sel) * SEL_BLOCK
    s_end = s_start + SEL_BLOCK - 1
    overlap = ((c_start[:, None] <= s_end[None, :]) & (c_end[:, None] >= s_start[None, :])).astype(np.float32)
    imp = jnp.einsum('bghsn,nm->bgsm', p_cmp, jnp.asarray(overlap))
    t = jnp.arange(S)
    j = jnp.arange(n_sel)
    cur = t // SEL_BLOCK
    valid = s_start[None, :] <= t[:, None]
    forced = (j[None, :] == 0) | (j[None, :] == cur[:, None]) | (j[None, :] == cur[:, None] - 1)
    score = jnp.where(valid, imp + FORCE_BONUS * forced.astype(jnp.float32), -FORCE_BONUS)
    _, sel_idx = lax.top_k(score, min(SEL_TOPK, n_sel))
    return sel_idx


def _selected_attention(q, k, v, sel_idx):
    B, S, G, hpg, dk = q.shape
    n_sel = S // SEL_BLOCK
    nk = sel_idx.shape[-1]
    kb = k.reshape(B, n_sel, SEL_BLOCK, G, dk).transpose(0, 3, 1, 2, 4)
    vb = v.reshape(B, n_sel, SEL_BLOCK, G, dk).transpose(0, 3, 1, 2, 4)
    nc = S // SEL_Q_CHUNK
    qc = q.reshape(B, nc, SEL_Q_CHUNK, G, hpg, dk).swapaxes(0, 1)
    ic = sel_idx.reshape(B, G, nc, SEL_Q_CHUNK, nk).transpose(2, 0, 1, 3, 4)
    tc = jnp.arange(S).reshape(nc, SEL_Q_CHUNK)
    bi = jnp.arange(B)[:, None, None, None]
    gi = jnp.arange(G)[None, :, None, None]

    def chunk(args):
        q_c, i_c, t_c = args
        kg = kb[bi, gi, i_c]
        vg = vb[bi, gi, i_c].reshape(B, G, SEL_Q_CHUNK, nk * SEL_BLOCK, dk)
        kpos = i_c[..., None] * SEL_BLOCK + jnp.arange(SEL_BLOCK)
        mask = (kpos <= t_c[None, None, :, None, None]).reshape(B, G, 1, SEL_Q_CHUNK, nk * SEL_BLOCK)
        s = jnp.einsum('bqghd,bgqnld->bghqnl', q_c, kg).astype(jnp.float32) * ATTN_SCALE
        s = s.reshape(B, G, hpg, SEL_Q_CHUNK, nk * SEL_BLOCK)
        pr = jax.nn.softmax(jnp.where(mask, s, NEG_INF), axis=-1)
        return jnp.einsum('bghqm,bgqmd->bqghd', pr.astype(vg.dtype), vg)

    out = lax.map(chunk, (qc, ic, tc))
    return out.swapaxes(0, 1).reshape(B, S, G, hpg, dk)


def _window_attention(q, k, v):
    B, S, G, hpg, dk = q.shape
    nb = S // WIN_Q_BLOCK
    band = WINDOW + WIN_Q_BLOCK
    kp = jnp.pad(k, ((0, 0), (WINDOW, 0), (0, 0), (0, 0)))
    vp = jnp.pad(v, ((0, 0), (WINDOW, 0), (0, 0), (0, 0)))
    qb = q.reshape(B, nb, WIN_Q_BLOCK, G, hpg, dk).swapaxes(0, 1)

    def block(args):
        q_b, b = args
        start = b * WIN_Q_BLOCK
        kk = lax.dynamic_slice_in_dim(kp, start, band, axis=1)
        vv = lax.dynamic_slice_in_dim(vp, start, band, axis=1)
        t = start + jnp.arange(WIN_Q_BLOCK)
        kpos = start - WINDOW + jnp.arange(band)
        d = t[:, None] - kpos[None, :]
        mask = (d >= 0) & (d < WINDOW) & (kpos[None, :] >= 0)
        s = jnp.einsum('bqghd,bkgd->bghqk', q_b, kk).astype(jnp.float32) * ATTN_SCALE
        pr = jax.nn.softmax(jnp.where(mask, s, NEG_INF), axis=-1)
        return jnp.einsum('bghqk,bkgd->bqghd', pr.astype(vv.dtype), vv)

    out = lax.map(block, (qb, jnp.arange(nb)))
    return out.swapaxes(0, 1).reshape(B, S, G, hpg, dk)


def _token_mixer(x, w_in, cmp_pe, cmp_w1, cmp_b1, cmp_w2, cmp_b2, conv_w, w_nsa_out, w_conv_out, w_o):
    B, S, _ = x.shape
    G, hpg, dk = NSA_KV_GROUPS, NSA_HPG, NSA_HEAD_DIM
    sizes = [Q_DIM] + [KV_DIM] * 6 + [NSA_GATE_DIM] + [CONV_DIM] * 3 + [D_MODEL] * 2
    z = x @ w_in
    (q, k_cmp, v_cmp, k_slc, v_slc, k_win, v_win, g_nsa,
     conv_b, conv_c, conv_h, g_m_nsa, g_m_conv) = jnp.split(z, np.cumsum(sizes)[:-1].tolist(), axis=-1)
    q = q.reshape(B, S, G, hpg, dk)
    kc = _compress(k_cmp.reshape(B, S, G, dk), cmp_pe[0], cmp_w1[0], cmp_b1[0], cmp_w2[0], cmp_b2[0])
    vc = _compress(v_cmp.reshape(B, S, G, dk), cmp_pe[1], cmp_w1[1], cmp_b1[1], cmp_w2[1], cmp_b2[1])
    o_cmp, p_cmp = _compressed_attention(q, kc, vc)
    sel_idx = _select_blocks(p_cmp, S)
    o_slc = _selected_attention(q, k_slc.reshape(B, S, G, dk), v_slc.reshape(B, S, G, dk), sel_idx)
    o_win = _window_attention(q, k_win.reshape(B, S, G, dk), v_win.reshape(B, S, G, dk))
    g = jax.nn.sigmoid(g_nsa).reshape(B, S, G, hpg, NSA_BRANCHES, 1)
    o = g[..., 0, :] * o_cmp + g[..., 1, :] * o_slc + g[..., 2, :] * o_win
    y_nsa = o.reshape(B, S, Q_DIM) @ w_nsa_out
    u = conv_c * conv_h
    uc = lax.conv_general_dilated(u, conv_w[:, None, :], window_strides=(1,),
                                  padding=[(CONV_WIDTH - 1, 0)],
                                  dimension_numbers=('NWC', 'WIO', 'NWC'),
                                  feature_group_count=CONV_DIM)
    y_conv = (conv_b * uc) @ w_conv_out
    merged = jax.nn.sigmoid(g_m_nsa) * y_nsa + jax.nn.sigmoid(g_m_conv) * y_conv
    return merged @ w_o


def _hier_moe(x, rg_w, rg_b, re_w, re_b, w_gate, w_up, w_down):
    B, S, D = x.shape
    xt = x.reshape(-1, D)
    T = xt.shape[0]
    grp_p = jax.nn.softmax((xt @ rg_w + rg_b).astype(jnp.float32), axis=-1)
    grp_top, grp_idx = lax.top_k(grp_p, 1)
    exp_logits = (xt @ re_w + re_b).astype(jnp.float32).reshape(T, N_GROUPS, EXPERTS_PER_GROUP)
    chosen = jnp.take_along_axis(exp_logits, grp_idx[:, :, None], axis=1)[:, 0]
    top_v, top_i = lax.top_k(jax.nn.softmax(chosen, axis=-1), TOPK_EXPERTS)
    top_v = top_v / jnp.sum(top_v, axis=-1, keepdims=True)
    w_grp = jnp.einsum('tk,tke->te', top_v, jax.nn.one_hot(top_i, EXPERTS_PER_GROUP, dtype=jnp.float32))
    comb = (jax.nn.one_hot(grp_idx[:, 0], N_GROUPS, dtype=jnp.float32)[:, :, None]
            * w_grp[:, None, :] * grp_top[:, :, None]).reshape(T, N_EXPERTS).astype(x.dtype)
    y = jnp.zeros_like(xt)
    for e in range(N_EXPERTS):
        h = jax.nn.silu(xt @ w_gate[e]) * (xt @ w_up[e])
        y = y + comb[:, e:e + 1] * (h @ w_down[e])
    return y.reshape(B, S, D)


def setup_inputs(seed: int = 0) -> dict:
    key = jax.random.key(seed)
    ks = jax.random.split(key, 28)
    f = jnp.float32
    L = DEPTH

    def nrm(k, shape, scale):
        return jax.random.normal(k, shape, f) * scale

    return {
        "x": nrm(ks[0], (BATCH, SEQ, D_MODEL), 1.0),
        "p": nrm(ks[1], (DEPTH, BATCH, SEQ, PLE_DIM), 1.0),
        "w_in": nrm(ks[2], (L, D_MODEL, IN_PROJ_DIM), D_MODEL ** -0.5),
        "cmp_pe": nrm(ks[3], (L, 2, CMP_BLOCK, NSA_HEAD_DIM), 0.1),
        "cmp_w1": nrm(ks[4], (L, 2, CMP_BLOCK * NSA_HEAD_DIM, CMP_HIDDEN), (CMP_BLOCK * NSA_HEAD_DIM) ** -0.5),
        "cmp_b1": nrm(ks[5], (L, 2, CMP_HIDDEN), 0.01),
        "cmp_w2": nrm(ks[6], (L, 2, CMP_HIDDEN, NSA_HEAD_DIM), CMP_HIDDEN ** -0.5),
        "cmp_b2": nrm(ks[7], (L, 2, NSA_HEAD_DIM), 0.01),
        "conv_w": nrm(ks[8], (L, CONV_WIDTH, CONV_DIM), CONV_WIDTH ** -0.5),
        "w_nsa_out": nrm(ks[9], (L, Q_DIM, D_MODEL), Q_DIM ** -0.5),
        "w_conv_out": nrm(ks[10], (L, CONV_DIM, D_MODEL), CONV_DIM ** -0.5),
        "w_o": nrm(ks[11], (L, D_MODEL, D_MODEL), D_MODEL ** -0.5 * DEEPNORM_BETA),
        "ln1_g": 1.0 + nrm(ks[12], (L, D_MODEL), 0.02),
        "ln1_b": nrm(ks[13], (L, D_MODEL), 0.02),
        "router_group_w": nrm(ks[14], (L, D_MODEL, N_GROUPS), D_MODEL ** -0.5),
        "router_group_b": nrm(ks[15], (L, N_GROUPS), 0.01),
        "router_expert_w": nrm(ks[16], (L, D_MODEL, N_EXPERTS), D_MODEL ** -0.5),
        "router_expert_b": nrm(ks[17], (L, N_EXPERTS), 0.01),
        "expert_w_gate": nrm(ks[18], (L, N_EXPERTS, D_MODEL, EXPERT_HIDDEN), D_MODEL ** -0.5),
        "expert_w_up": nrm(ks[19], (L, N_EXPERTS, D_MODEL, EXPERT_HIDDEN), D_MODEL ** -0.5),
        "expert_w_down": nrm(ks[20], (L, N_EXPERTS, EXPERT_HIDDEN, D_MODEL), EXPERT_HIDDEN ** -0.5 * DEEPNORM_BETA),
        "ple_proj": nrm(ks[21], (L, PLE_DIM, D_MODEL), PLE_DIM ** -0.5),
        "ple_gate_w": nrm(ks[22], (L, D_MODEL, D_MODEL), D_MODEL ** -0.5),
        "ple_gate_b": nrm(ks[23], (L, D_MODEL), 0.01),
        "ln2_g": 1.0 + nrm(ks[24], (L, D_MODEL), 0.02),
        "ln2_b": nrm(ks[25], (L, D_MODEL), 0.02),
    }


def reference(x, p, w_in, cmp_pe, cmp_w1, cmp_b1, cmp_w2, cmp_b2, conv_w, w_nsa_out, w_conv_out, w_o,
              ln1_g, ln1_b, router_group_w, router_group_b, router_expert_w, router_expert_b,
              expert_w_gate, expert_w_up, expert_w_down, ple_proj, ple_gate_w, ple_gate_b, ln2_g, ln2_b):
    for i in range(DEPTH):
        mix = _token_mixer(x, w_in[i], cmp_pe[i], cmp_w1[i], cmp_b1[i], cmp_w2[i], cmp_b2[i],
                           conv_w[i], w_nsa_out[i], w_conv_out[i], w_o[i])
        x = _layer_norm(DEEPNORM_ALPHA * x + mix, ln1_g[i], ln1_b[i])
        ffn = _hier_moe(x, router_group_w[i], router_group_b[i], router_expert_w[i], router_expert_b[i],
                        expert_w_gate[i], expert_w_up[i], expert_w_down[i])
        ple = jax.nn.sigmoid(x @ ple_gate_w[i] + ple_gate_b[i]) * (p[i] @ ple_proj[i])
        x = _layer_norm(DEEPNORM_ALPHA * x + ffn + ple, ln2_g[i], ln2_b[i])
    return x
```

```python
import functools

import jax
import jax.numpy as jnp
import numpy as np
from jax import lax
from jax.experimental import pallas as pl
from jax.experimental.pallas import tpu as pltpu

F32 = jnp.float32
BF16 = jnp.bfloat16

N_HEADS = 8
N_GROUPS_KV = 2
HEADS_PER_GROUP = N_HEADS // N_GROUPS_KV
HEAD_DIM = 64
Q_DIM = N_HEADS * HEAD_DIM
KV_DIM = N_GROUPS_KV * HEAD_DIM
N_BRANCH = 3
CMP_BLOCK = 32
CMP_STRIDE = 16
SEL_BLOCK = 64
SEL_TOPK = 16
WINDOW = 512
CONV_DIM = 512
N_EXPERT_GROUPS = 4
EXPERTS_PER_GROUP = 8
N_EXPERTS = N_EXPERT_GROUPS * EXPERTS_PER_GROUP
ATTN_SCALE = HEAD_DIM ** -0.5
FORCE_BONUS = 1e4
NEG_INF = -1e30
LN_EPS = 1e-5

LANES = 128
ATT_TILE = 256
GATE_ROWS = 16
VMEM_LIMIT = 48 * 1024 * 1024


def _cparams(n_axes):
    return pltpu.CompilerParams(dimension_semantics=("arbitrary",) * n_axes,
                                vmem_limit_bytes=VMEM_LIMIT)


def _dot(a, b):
    return jnp.dot(a, b, preferred_element_type=F32)


def _dot_nt(a, b):
    return lax.dot_general(a, b, (((1,), (1,)), ((), ())), preferred_element_type=F32)


def _dot_tn(a, b):
    return lax.dot_general(a, b, (((0,), (0,)), ((), ())), preferred_element_type=F32)


def _sigmoid(v):
    return 1.0 / (1.0 + jnp.exp(-v))


def _layer_norm(v, g, b):
    mu = jnp.mean(v, axis=-1, keepdims=True)
    d = v - mu
    var = jnp.mean(d * d, axis=-1, keepdims=True)
    return d * lax.rsqrt(var + LN_EPS) * g + b


def _inproj_kernel(tiles_per_seq, x_ref, xprev_ref, wtok_ref, wt_ref, convw_ref,
                   kcmp_ref, vcmp_ref, kslc_ref, kwin_ref, cu_ref, qt_ref, gt_ref, vslct_ref, vwint_ref):
    i = pl.program_id(0)
    tm = x_ref.shape[0]
    xb = x_ref[...].astype(BF16)

    def tok(c0, n):
        return _dot(xb, wtok_ref[:, c0:c0 + n])

    kcmp_ref[...] = tok(0, KV_DIM).astype(BF16)
    vcmp_ref[...] = tok(KV_DIM, KV_DIM).astype(BF16)
    kslc_ref[...] = tok(2 * KV_DIM, KV_DIM).astype(BF16)
    kwin_ref[...] = tok(3 * KV_DIM, KV_DIM).astype(BF16)
    c0 = 4 * KV_DIM
    cb = tok(c0, CONV_DIM)
    u = tok(c0 + CONV_DIM, CONV_DIM) * tok(c0 + 2 * CONV_DIM, CONV_DIM)
    xpb = xprev_ref[...].astype(BF16)
    up = (_dot(xpb, wtok_ref[:, c0 + CONV_DIM:c0 + 2 * CONV_DIM])
          * _dot(xpb, wtok_ref[:, c0 + 2 * CONV_DIM:c0 + 3 * CONV_DIM]))
    up = jnp.where(i % tiles_per_seq == 0, 0.0, up)
    row = lax.broadcasted_iota(jnp.int32, (tm, CONV_DIM), 0)
    u1 = jnp.where(row == 0, up[7:8, :], pltpu.roll(u, 1, 0))
    u2 = jnp.where(row == 0, up[6:7, :], jnp.where(row == 1, up[7:8, :], pltpu.roll(u, 2, 0)))
    w = convw_ref[...]
    uc = w[0:1, :] * u2 + w[1:2, :] * u1 + w[2:3, :] * u
    cu_ref[...] = (cb * uc).astype(BF16)

    qt_ref[...] = (_dot_nt(wt_ref[0:Q_DIM, :], xb) * ATTN_SCALE).astype(BF16)
    r0 = Q_DIM
    gt_ref[...] = _sigmoid(_dot_nt(wt_ref[r0:r0 + 2 * GATE_ROWS, :], xb))
    r0 += 2 * GATE_ROWS
    vs = _dot_nt(wt_ref[r0:r0 + KV_DIM, :], xb).astype(BF16)
    vw = _dot_nt(wt_ref[r0 + KV_DIM:r0 + 2 * KV_DIM, :], xb).astype(BF16)
    for c in range(tm // ATT_TILE):
        vslct_ref[c] = vs[:, c * ATT_TILE:(c + 1) * ATT_TILE]
        vwint_ref[c] = vw[:, c * ATT_TILE:(c + 1) * ATT_TILE]


def _in_proj(x2, wtok, wt, conv_w, seq):
    T, D = x2.shape
    tm = 1024
    nt = T // tm
    n_tok = wtok.shape[1]
    n_t = wt.shape[0]
    row_blk = lambda n: pl.BlockSpec((tm, n), lambda i: (i, 0))
    out_shape = (
        jax.ShapeDtypeStruct((T, KV_DIM), BF16),
        jax.ShapeDtypeStruct((T, KV_DIM), BF16),
        jax.ShapeDtypeStruct((T, KV_DIM), BF16),
        jax.ShapeDtypeStruct((T, KV_DIM), BF16),
        jax.ShapeDtypeStruct((T, CONV_DIM), BF16),
        jax.ShapeDtypeStruct((Q_DIM, T), BF16),
        jax.ShapeDtypeStruct((2 * GATE_ROWS, T), F32),
        jax.ShapeDtypeStruct((T // ATT_TILE, KV_DIM, ATT_TILE), BF16),
        jax.ShapeDtypeStruct((T // ATT_TILE, KV_DIM, ATT_TILE), BF16),
    )
    vt_blk = pl.BlockSpec((tm // ATT_TILE, KV_DIM, ATT_TILE), lambda i: (i, 0, 0))
    return pl.pallas_call(
        functools.partial(_inproj_kernel, seq // tm),
        grid=(nt,),
        in_specs=[
            row_blk(D),
            pl.BlockSpec((8, D), lambda i: (jnp.maximum(i * (tm // 8) - 1, 0), 0)),
            pl.BlockSpec((D, n_tok), lambda i: (0, 0)),
            pl.BlockSpec((n_t, D), lambda i: (0, 0)),
            pl.BlockSpec((3, CONV_DIM), lambda i: (0, 0)),
        ],
        out_specs=(
            row_blk(KV_DIM), row_blk(KV_DIM), row_blk(KV_DIM), row_blk(KV_DIM), row_blk(CONV_DIM),
            pl.BlockSpec((Q_DIM, tm), lambda i: (0, i)),
            pl.BlockSpec((2 * GATE_ROWS, tm), lambda i: (0, i)),
            vt_blk, vt_blk,
        ),
        out_shape=out_shape,
        compiler_params=_cparams(1),
        name="in_proj",
    )(x2, x2, wtok, wt, conv_w)


def _compress_kernel(kin_ref, vin_ref, wa_ref, wb_ref, pe_ref, w1_ref, b1_ref, w2_ref, b2_ref,
                     kc_ref, vct_ref):
    def one(idx, in_ref):
        c = in_ref[0]
        a = _dot(c, wa_ref[idx])
        b = _dot(c, wb_ref[idx])
        n = b.shape[0]
        peb = _dot(pe_ref[idx], w1_ref[idx])[0:1, :]
        bias = jnp.concatenate([peb, peb], axis=1) + b1_ref[idx]
        h = a + pltpu.roll(b, n - 1, 0) + bias
        return _dot(jax.nn.gelu(h).astype(BF16), w2_ref[idx]) + b2_ref[idx]

    kc_ref[0] = one(0, kin_ref).astype(BF16)
    vct_ref[0] = one(1, vin_ref).T.astype(BF16)


def _compress(kcmp3, vcmp3, wa, wb, pe, w1, b1, w2, b2):
    B, nch, width = kcmp3.shape
    hid2 = wa.shape[2]
    full = lambda a: pl.BlockSpec(a.shape, lambda b: (0,) * a.ndim)
    in_blk = pl.BlockSpec((1, nch, width), lambda b: (b, 0, 0))
    out_blk = pl.BlockSpec((1, nch, KV_DIM), lambda b: (b, 0, 0))
    return pl.pallas_call(
        _compress_kernel,
        grid=(B,),
        in_specs=[in_blk, in_blk, full(wa), full(wb), full(pe), full(w1), full(b1), full(w2), full(b2)],
        out_specs=(out_blk, pl.BlockSpec((1, KV_DIM, nch), lambda b: (b, 0, 0))),
        out_shape=(jax.ShapeDtypeStruct((B, nch, KV_DIM), BF16),
                   jax.ShapeDtypeStruct((B, KV_DIM, nch), BF16)),
        compiler_params=_cparams(1),
        name="compress",
    )(kcmp3, vcmp3, wa, wb, pe, w1, b1, w2, b2)


def _group_q(qt_ref, hh, g_is0):
    qh = qt_ref[hh * HEAD_DIM:(hh + 1) * HEAD_DIM, :]
    z = jnp.zeros_like(qh)
    return jnp.concatenate([qh, z], axis=0) if g_is0 else jnp.concatenate([z, qh], axis=0)


def _cmpattn_kernel(n_cmp, qt_ref, kc_ref, vct_ref, gt_ref, ovt_ref, ocmpt_ref, selb_ref):
    i = pl.program_id(1)
    tq = qt_ref.shape[1]
    nc = kc_ref.shape[1]
    n_sel = ovt_ref.shape[0]
    kc = kc_ref[0]
    t_n = i * tq + lax.broadcasted_iota(jnp.int32, (nc, tq), 1)
    n_io = lax.broadcasted_iota(jnp.int32, (nc, tq), 0)
    blk_end = jnp.where(n_io < n_cmp, n_io * CMP_STRIDE + CMP_BLOCK - 1, jnp.int32(2 ** 30))
    vis = blk_end <= t_n
    j_io = lax.broadcasted_iota(jnp.int32, (n_sel, tq), 0)
    t_j = i * tq + lax.broadcasted_iota(jnp.int32, (n_sel, tq), 1)
    cur = t_j // SEL_BLOCK
    bonus = jnp.where(j_io == 0, FORCE_BONUS,
                      jnp.where(j_io == cur, FORCE_BONUS, jnp.where(j_io == cur - 1, FORCE_BONUS, 0.0)))
    valid = j_io * SEL_BLOCK <= t_j
    ktiles = selb_ref.shape[0] // N_GROUPS_KV
    blocks_per_tile = n_sel // ktiles
    for g in range(N_GROUPS_KV):
        imp = jnp.zeros((n_sel, tq), F32)
        for hh in range(HEADS_PER_GROUP):
            h = g * HEADS_PER_GROUP + hh
            s = _dot(kc, _group_q(qt_ref, h, g == 0))
            s = jnp.where(vis, s, NEG_INF)
            m = jnp.max(s, axis=0, keepdims=True)
            p = jnp.where(vis, jnp.exp(s - m), 0.0)
            l = jnp.sum(p, axis=0, keepdims=True)
            pr = (p * jnp.where(l > 0.0, 1.0 / l, 0.0)).astype(BF16)
            o = _dot(vct_ref[0, g * HEAD_DIM:(g + 1) * HEAD_DIM, :], pr)
            gate = gt_ref[g * GATE_ROWS + hh * N_BRANCH:g * GATE_ROWS + hh * N_BRANCH + 1, :]
            ocmpt_ref[h * HEAD_DIM:(h + 1) * HEAD_DIM, :] = (o * gate).astype(BF16)
            imp = imp + _dot(ovt_ref[...], pr)
        score = jnp.where(valid, imp + bonus, -FORCE_BONUS)
        cnt = jnp.zeros((n_sel, tq), F32)
        for jp in range(n_sel):
            r = score[jp:jp + 1, :]
            cnt = cnt + jnp.where(r > score, 1.0, jnp.where(r == score, jnp.where(j_io > jp, 1.0, 0.0), 0.0))
        bias = jnp.where(cnt < float(min(SEL_TOPK, n_sel)), 0.0, NEG_INF)
        for jt in range(ktiles):
            selb_ref[g * ktiles + jt, 0:blocks_per_tile, :] = bias[jt * blocks_per_tile:(jt + 1) * blocks_per_tile, :]
            selb_ref[g * ktiles + jt, blocks_per_tile:8, :] = jnp.zeros((8 - blocks_per_tile, tq), F32)


def _cmp_attn(qt, kc, vct, gt, ovt, B, seq, n_cmp):
    T = qt.shape[1]
    tq = ATT_TILE
    nq = seq // tq
    ktiles = seq // ATT_TILE
    nc = kc.shape[1]
    return pl.pallas_call(
        functools.partial(_cmpattn_kernel, n_cmp),
        grid=(B, nq),
        in_specs=[
            pl.BlockSpec((Q_DIM, tq), lambda b, i: (0, b * nq + i)),
            pl.BlockSpec((1, nc, KV_DIM), lambda b, i: (b, 0, 0)),
            pl.BlockSpec((1, KV_DIM, nc), lambda b, i: (b, 0, 0)),
            pl.BlockSpec((2 * GATE_ROWS, tq), lambda b, i: (0, b * nq + i)),
            pl.BlockSpec(ovt.shape, lambda b, i: (0, 0)),
        ],
        out_specs=(
            pl.BlockSpec((Q_DIM, tq), lambda b, i: (0, b * nq + i)),
            pl.BlockSpec((N_GROUPS_KV * ktiles, 8, tq), lambda b, i: (b * nq + i, 0, 0)),
        ),
        out_shape=(jax.ShapeDtypeStruct((Q_DIM, T), BF16),
                   jax.ShapeDtypeStruct((B * nq * N_GROUPS_KV * ktiles, 8, tq), F32)),
        compiler_params=_cparams(2),
        name="cmp_attn",
    )(qt, kc, vct, gt, ovt)


def _nsa_kernel(qt_ref, kslc_ref, kwin_ref, vslct_ref, vwint_ref, selb_ref, gt_ref, ocmpt_ref,
                ot_ref, qpad_s, m_s, l_s, acc_s):
    i = pl.program_id(1)
    g = pl.program_id(2)
    tq = qt_ref.shape[1]
    tk = ATT_TILE
    blocks_per_tile = tk // SEL_BLOCK

    row_group = lax.broadcasted_iota(jnp.int32, (KV_DIM, tq), 0) // HEAD_DIM
    for hh in range(HEADS_PER_GROUP):
        qh = qt_ref[hh * HEAD_DIM:(hh + 1) * HEAD_DIM, :]
        q2 = jnp.concatenate([qh, qh], axis=0)
        qpad_s[hh] = jnp.where(row_group == g, q2, jnp.zeros_like(q2))
    m_s[...] = jnp.full(m_s.shape, NEG_INF, F32)
    l_s[...] = jnp.zeros(l_s.shape, F32)
    acc_s[...] = jnp.zeros(acc_s.shape, F32)

    r_io = lax.broadcasted_iota(jnp.int32, (tk, tq), 0)
    c_io = lax.broadcasted_iota(jnp.int32, (tk, tq), 1)

    def attend(branch, k_tile, vt_tile, bias, mask):
        for hh in range(HEADS_PER_GROUP):
            r = branch * HEADS_PER_GROUP + hh
            s = _dot(k_tile, qpad_s[hh])
            if bias is not None:
                s = s + bias
            if mask is not None:
                s = jnp.where(mask, s, NEG_INF)
            m_old = m_s[r:r + 1, :]
            m_new = jnp.maximum(m_old, jnp.max(s, axis=0, keepdims=True))
            alpha = jnp.exp(m_old - m_new)
            p = jnp.exp(s - m_new)
            l_s[r:r + 1, :] = alpha * l_s[r:r + 1, :] + jnp.sum(p, axis=0, keepdims=True)
            acc_s[r] = alpha * acc_s[r] + _dot(vt_tile, p.astype(BF16))
            m_s[r:r + 1, :] = m_new

    def sel_bias(j):
        sb = selb_ref[j]
        return jnp.concatenate(
            [jnp.broadcast_to(sb[q:q + 1, :], (SEL_BLOCK, tq)) for q in range(blocks_per_tile)], axis=0)

    def k_rows(ref, j):
        return ref[pl.ds(pl.multiple_of(j * tk, tk), tk), :]

    attend(0, k_rows(kslc_ref, i), vslct_ref[i], sel_bias(i), r_io <= c_io)

    def sel_body(j, carry):
        attend(0, k_rows(kslc_ref, j), vslct_ref[j], sel_bias(j), None)
        return carry

    lax.fori_loop(0, i, sel_body, 0)

    attend(1, k_rows(kwin_ref, i), vwint_ref[i], None, r_io <= c_io)

    @pl.when(i >= 1)
    def _():
        attend(1, k_rows(kwin_ref, i - 1), vwint_ref[i - 1], None, None)

    @pl.when(i >= WINDOW // tk)
    def _():
        j = i - WINDOW // tk
        attend(1, k_rows(kwin_ref, j), vwint_ref[j], None, r_io > c_io)

    for hh in range(HEADS_PER_GROUP):
        o_slc = acc_s[hh] * (1.0 / l_s[hh:hh + 1, :])
        o_win = acc_s[HEADS_PER_GROUP + hh] * (1.0 / l_s[HEADS_PER_GROUP + hh:HEADS_PER_GROUP + hh + 1, :])
        g_slc = gt_ref[hh * N_BRANCH + 1:hh * N_BRANCH + 2, :]
        g_win = gt_ref[hh * N_BRANCH + 2:hh * N_BRANCH + 3, :]
        o = ocmpt_ref[hh * HEAD_DIM:(hh + 1) * HEAD_DIM, :].astype(F32) + g_slc * o_slc + g_win * o_win
        ot_ref[hh * HEAD_DIM:(hh + 1) * HEAD_DIM, :] = o.astype(BF16)


def _nsa_attn(qt, kslc, kwin, vslct, vwint, selb, gt, ocmpt, B, seq):
    T = qt.shape[1]
    tq = ATT_TILE
    nq = seq // tq
    ktiles = seq // ATT_TILE
    gq = HEADS_PER_GROUP * HEAD_DIM
    qblk = pl.BlockSpec((gq, tq), lambda b, i, g: (g, b * nq + i))
    kblk = pl.BlockSpec((seq, KV_DIM), lambda b, i, g: (b, 0))
    vblk = pl.BlockSpec((ktiles, HEAD_DIM, ATT_TILE), lambda b, i, g: (b, g, 0))
    return pl.pallas_call(
        _nsa_kernel,
        grid=(B, nq, N_GROUPS_KV),
        in_specs=[
            qblk, kblk, kblk, vblk, vblk,
            pl.BlockSpec((ktiles, 8, tq), lambda b, i, g: ((b * nq + i) * N_GROUPS_KV + g, 0, 0)),
            pl.BlockSpec((GATE_ROWS, tq), lambda b, i, g: (g, b * nq + i)),
            qblk,
        ],
        out_specs=qblk,
        out_shape=jax.ShapeDtypeStruct((Q_DIM, T), BF16),
        scratch_shapes=[
            pltpu.VMEM((HEADS_PER_GROUP, KV_DIM, tq), BF16),
            pltpu.VMEM((2 * HEADS_PER_GROUP, tq), F32),
            pltpu.VMEM((2 * HEADS_PER_GROUP, tq), F32),
            pltpu.VMEM((2 * HEADS_PER_GROUP, HEAD_DIM, tq), F32),
        ],
        compiler_params=_cparams(3),
        name="nsa_attn",
    )(qt, kslc, kwin, vslct, vwint, selb, gt, ocmpt)


def _route(logits):
    lane = lax.broadcasted_iota(jnp.int32, logits.shape, 1)
    is_grp = (lane >= N_EXPERTS) & (lane < N_EXPERTS + N_EXPERT_GROUPS)
    gmax = jnp.max(jnp.where(is_grp, logits, NEG_INF), axis=1, keepdims=True)
    ge = jnp.where(is_grp, jnp.exp(logits - gmax), 0.0)
    gp = ge / jnp.sum(ge, axis=1, keepdims=True)
    g_top = jnp.max(gp, axis=1, keepdims=True)
    g_idx = jnp.min(jnp.where(is_grp, jnp.where(gp == g_top, lane, 2 ** 20), 2 ** 20),
                    axis=1, keepdims=True) - N_EXPERTS
    chosen = (lane < N_EXPERTS) & (lane // EXPERTS_PER_GROUP == g_idx)
    cmax = jnp.max(jnp.where(chosen, logits, NEG_INF), axis=1, keepdims=True)
    ce = jnp.where(chosen, jnp.exp(logits - cmax), 0.0)
    cp = ce / jnp.sum(ce, axis=1, keepdims=True)
    v1 = jnp.max(jnp.where(chosen, cp, -1.0), axis=1, keepdims=True)
    i1 = jnp.min(jnp.where(chosen, jnp.where(cp == v1, lane, 2 ** 20), 2 ** 20), axis=1, keepdims=True)
    rest = chosen & (lane != i1)
    v2 = jnp.max(jnp.where(rest, cp, -1.0), axis=1, keepdims=True)
    i2 = jnp.min(jnp.where(rest, jnp.where(cp == v2, lane, 2 ** 20), 2 ** 20), axis=1, keepdims=True)
    tot = v1 + v2
    return jnp.where(lane == i1, v1 / tot, jnp.where(lane == i2, v2 / tot, 0.0)) * g_top


def _merge_kernel(alpha, ot_ref, cu_ref, x_ref, wn_ref, wc_ref, wgm_ref, wo_ref, g1_ref, b1_ref,
                  wr_ref, br_ref, h1_ref, comb_ref):
    d = x_ref.shape[1]
    x = x_ref[...]
    xb = x.astype(BF16)
    y_nsa = _dot_tn(ot_ref[...], wn_ref[...])
    y_conv = _dot(cu_ref[...], wc_ref[...])
    merged = (_sigmoid(_dot(xb, wgm_ref[:, 0:d])) * y_nsa
              + _sigmoid(_dot(xb, wgm_ref[:, d:2 * d])) * y_conv)
    mix = _dot(merged.astype(BF16), wo_ref[...])
    h1 = _layer_norm(alpha * x + mix, g1_ref[...], b1_ref[...])
    h1_ref[...] = h1
    comb_ref[...] = _route(_dot(h1.astype(BF16), wr_ref[...]) + br_ref[...])


def _merge(ot, cu, x2, wn, wc, wgm, wo, g1, b1, wr, br, alpha):
    T, D = x2.shape
    tm = 512
    full = lambda a: pl.BlockSpec(a.shape, lambda i: (0,) * a.ndim)
    return pl.pallas_call(
        functools.partial(_merge_kernel, alpha),
        grid=(T // tm,),
        in_specs=[
            pl.BlockSpec((Q_DIM, tm), lambda i: (0, i)),
            pl.BlockSpec((tm, CONV_DIM), lambda i: (i, 0)),
            pl.BlockSpec((tm, D), lambda i: (i, 0)),
            full(wn), full(wc), full(wgm), full(wo), full(g1), full(b1), full(wr), full(br),
        ],
        out_specs=(pl.BlockSpec((tm, D), lambda i: (i, 0)), pl.BlockSpec((tm, LANES), lambda i: (i, 0))),
        out_shape=(jax.ShapeDtypeStruct((T, D), F32), jax.ShapeDtypeStruct((T, LANES), F32)),
        compiler_params=_cparams(1),
        name="merge",
    )(ot, cu, x2, wn, wc, wgm, wo, g1, b1, wr, br)


def _moe_kernel(alpha, h1_ref, comb_ref, wg_ref, wu_ref, wd_ref, p_ref, pproj_ref, pgw_ref, pgb_ref,
                g2_ref, b2_ref, out_ref, xb_s, acc_s):
    e = pl.program_id(1)

    @pl.when(e == 0)
    def _():
        xb_s[...] = h1_ref[...].astype(BF16)
        acc_s[...] = jnp.zeros(acc_s.shape, F32)

    xb = xb_s[...]
    hg = _dot(xb, wg_ref[0])
    hu = _dot(xb, wu_ref[0])
    comb = comb_ref[...]
    lane = lax.broadcasted_iota(jnp.int32, comb.shape, 1)
    c = jnp.sum(jnp.where(lane == e, comb, 0.0), axis=1, keepdims=True)
    h = hg * _sigmoid(hg) * hu * c
    acc_s[...] += _dot(h.astype(BF16), wd_ref[0])

    @pl.when(e == pl.num_programs(1) - 1)
    def _():
        ple = (_sigmoid(_dot(xb, pgw_ref[...]) + pgb_ref[...])
               * _dot(p_ref[...].astype(BF16), pproj_ref[...]))
        out_ref[...] = _layer_norm(alpha * h1_ref[...] + acc_s[...] + ple, g2_ref[...], b2_ref[...])


def _moe(h1, comb, wg, wu, wd, p2, pproj, pgw, pgb, g2, b2, alpha):
    T, D = h1.shape
    tm = 512
    n_e, _, hid = wg.shape
    full = lambda a: pl.BlockSpec(a.shape, lambda i, e: (0,) * a.ndim)
    return pl.pallas_call(
        functools.partial(_moe_kernel, alpha),
        grid=(T // tm, n_e),
        in_specs=[
            pl.BlockSpec((tm, D), lambda i, e: (i, 0)),
            pl.BlockSpec((tm, LANES), lambda i, e: (i, 0)),
            pl.BlockSpec((1, D, hid), lambda i, e: (e, 0, 0)),
            pl.BlockSpec((1, D, hid), lambda i, e: (e, 0, 0)),
            pl.BlockSpec((1, hid, D), lambda i, e: (e, 0, 0)),
            pl.BlockSpec((tm, p2.shape[1]), lambda i, e: (i, 0)),
            full(pproj), full(pgw), full(pgb), full(g2), full(b2),
        ],
        out_specs=pl.BlockSpec((tm, D), lambda i, e: (i, 0)),
        out_shape=jax.ShapeDtypeStruct((T, D), F32),
        scratch_shapes=[pltpu.VMEM((tm, D), BF16), pltpu.VMEM((tm, D), F32)],
        compiler_params=_cparams(2),
        name="moe",
    )(h1, comb, wg, wu, wd, p2, pproj, pgw, pgb, g2, b2)


def _split_w_in(w_in, D):
    sizes = [Q_DIM] + [KV_DIM] * 6 + [N_HEADS * N_BRANCH] + [CONV_DIM] * 3 + [D] * 2
    offs = np.concatenate([[0], np.cumsum(sizes)])
    names = ["q", "k_cmp", "v_cmp", "k_slc", "v_slc", "k_win", "v_win", "g_nsa",
             "conv_b", "conv_c", "conv_h", "g_m_nsa", "g_m_conv"]
    return {n: w_in[:, int(offs[k]):int(offs[k + 1])] for k, n in enumerate(names)}


def _layer(x2, p2, B, seq, depth, w_in, cmp_pe, cmp_w1, cmp_b1, cmp_w2, cmp_b2, conv_w, w_nsa_out,
           w_conv_out, w_o, ln1_g, ln1_b, rg_w, rg_b, re_w, re_b, e_wg, e_wu, e_wd, ple_proj,
           ple_gate_w, ple_gate_b, ln2_g, ln2_b):
    T, D = x2.shape
    alpha = (2.0 * depth) ** 0.25
    w = _split_w_in(w_in, D)
    wtok = jnp.concatenate([w["k_cmp"], w["v_cmp"], w["k_slc"], w["k_win"],
                            w["conv_b"], w["conv_c"], w["conv_h"]], axis=1).astype(BF16)
    gcols = w["g_nsa"].reshape(D, N_GROUPS_KV, HEADS_PER_GROUP * N_BRANCH)
    gcols = jnp.pad(gcols, ((0, 0), (0, 0), (0, GATE_ROWS - HEADS_PER_GROUP * N_BRANCH)))
    wt = jnp.concatenate([w["q"], gcols.reshape(D, N_GROUPS_KV * GATE_ROWS), w["v_slc"], w["v_win"]],
                         axis=1).T.astype(BF16)
    kcmp, vcmp, kslc, kwin, cu, qt, gt, vslct, vwint = _in_proj(x2, wtok, wt, conv_w, seq)

    half = CMP_BLOCK // 2
    n_chunks = seq // CMP_STRIDE
    n_cmp = (seq - CMP_BLOCK) // CMP_STRIDE + 1
    hidden = cmp_w1.shape[-1]
    eye = jnp.eye(N_GROUPS_KV, dtype=F32)
    w1r = cmp_w1.reshape(2, CMP_BLOCK, HEAD_DIM, hidden)
    expand = lambda m: jnp.einsum("ildh,gk->ilgdkh", m, eye).reshape(
        2, half * KV_DIM, N_GROUPS_KV * hidden).astype(BF16)
    wa, wb = expand(w1r[:, :half]), expand(w1r[:, half:])
    pe = jnp.broadcast_to(cmp_pe.reshape(2, 1, CMP_BLOCK * HEAD_DIM), (2, 8, CMP_BLOCK * HEAD_DIM)).astype(BF16)
    b1t = jnp.tile(cmp_b1.reshape(2, 1, hidden), (1, 1, N_GROUPS_KV))
    w2b = jnp.einsum("ihd,gk->ighkd", cmp_w2, eye).reshape(2, N_GROUPS_KV * hidden, KV_DIM).astype(BF16)
    b2t = jnp.tile(cmp_b2.reshape(2, 1, HEAD_DIM), (1, 1, N_GROUPS_KV))
    kc, vct = _compress(kcmp.reshape(B, n_chunks, CMP_STRIDE * KV_DIM),
                        vcmp.reshape(B, n_chunks, CMP_STRIDE * KV_DIM),
                        wa, wb, pe, cmp_w1.astype(BF16), b1t, w2b, b2t)

    n_sel = seq // SEL_BLOCK
    c_start = np.arange(n_chunks) * CMP_STRIDE
    s_start = np.arange(n_sel) * SEL_BLOCK
    overlap = ((c_start[None, :] <= s_start[:, None] + SEL_BLOCK - 1)
               & (c_start[None, :] + CMP_BLOCK - 1 >= s_start[:, None])).astype(np.float32)
    ocmpt, selb = _cmp_attn(qt, kc, vct, gt, jnp.asarray(overlap, BF16), B, seq, n_cmp)
    ot = _nsa_attn(qt, kslc, kwin, vslct, vwint, selb, gt, ocmpt, B, seq)

    wgm = jnp.concatenate([w["g_m_nsa"], w["g_m_conv"]], axis=1).astype(BF16)
    wr = jnp.pad(jnp.concatenate([re_w, rg_w], axis=1), ((0, 0), (0, LANES - N_EXPERTS - N_EXPERT_GROUPS)))
    br = jnp.pad(jnp.concatenate([re_b, rg_b]), (0, LANES - N_EXPERTS - N_EXPERT_GROUPS)).reshape(1, LANES)
    h1, comb = _merge(ot, cu, x2, w_nsa_out.astype(BF16), w_conv_out.astype(BF16), wgm, w_o.astype(BF16),
                      ln1_g.reshape(1, D), ln1_b.reshape(1, D), wr.astype(BF16), br, alpha)
    return _moe(h1, comb, e_wg.astype(BF16), e_wu.astype(BF16), e_wd.astype(BF16), p2,
                ple_proj.astype(BF16), ple_gate_w.astype(BF16), ple_gate_b.reshape(1, D),
                ln2_g.reshape(1, D), ln2_b.reshape(1, D), alpha)


def kernel(x, p, w_in, cmp_pe, cmp_w1, cmp_b1, cmp_w2, cmp_b2, conv_w, w_nsa_out, w_conv_out, w_o, ln1_g, ln1_b, router_group_w, router_group_b, router_expert_w, router_expert_b, expert_w_gate, expert_w_up, expert_w_down, ple_proj, ple_gate_w, ple_gate_b, ln2_g, ln2_b):
    B, seq, D = x.shape
    depth = w_in.shape[0]
    x2 = x.reshape(B * seq, D)
    for i in range(depth):
        x2 = _layer(x2, p[i].reshape(B * seq, -1), B, seq, depth, w_in[i], cmp_pe[i], cmp_w1[i], cmp_b1[i],
                    cmp_w2[i], cmp_b2[i], conv_w[i], w_nsa_out[i], w_conv_out[i], w_o[i], ln1_g[i], ln1_b[i],
                    router_group_w[i], router_group_b[i], router_expert_w[i], router_expert_b[i],
                    expert_w_gate[i], expert_w_up[i], expert_w_down[i], ple_proj[i], ple_gate_w[i],
                    ple_gate_b[i], ln2_g[i], ln2_b[i])
    return x2.reshape(B, seq, D)
```

```python
import functools

import jax
import jax.numpy as jnp
import numpy as np
from jax import lax
from jax.experimental import pallas as pl
from jax.experimental.pallas import tpu as pltpu

F32 = jnp.float32
BF16 = jnp.bfloat16

N_HEADS = 8
N_GROUPS_KV = 2
HEADS_PER_GROUP = N_HEADS // N_GROUPS_KV
HEAD_DIM = 64
Q_DIM = N_HEADS * HEAD_DIM
KV_DIM = N_GROUPS_KV * HEAD_DIM
N_BRANCH = 3
CMP_BLOCK = 32
CMP_STRIDE = 16
SEL_BLOCK = 64
SEL_TOPK = 16
WINDOW = 512
CONV_DIM = 512
N_EXPERT_GROUPS = 4
EXPERTS_PER_GROUP = 8
N_EXPERTS = N_EXPERT_GROUPS * EXPERTS_PER_GROUP
ATTN_SCALE = HEAD_DIM ** -0.5
LOG2_E = 1.4426950408889634
FORCE_BONUS = 1e4
NEG_INF = -1e30
LN_EPS = 1e-5

LANES = 128
ATT_TILE = 256
GATE_ROWS = 16
SUM_ROWS = 16
VMEM_LIMIT = 48 * 1024 * 1024

MOE_TILE = 256
EXPERT_TILE = 512
CHUNK = 16
LOCAL_ROWS = -(-(2 * MOE_TILE + N_EXPERTS * (CHUNK - 1)) // ATT_TILE) * ATT_TILE
MAX_CHUNKS = LOCAL_ROWS // CHUNK
TAB_COUNT = LANES - 1


def _cparams(n_axes):
    return pltpu.CompilerParams(dimension_semantics=("arbitrary",) * n_axes,
                                vmem_limit_bytes=VMEM_LIMIT)


def _dot(a, b):
    return jnp.dot(a, b, preferred_element_type=F32)


def _dot_nt(a, b):
    return lax.dot_general(a, b, (((1,), (1,)), ((), ())), preferred_element_type=F32)


def _dot_tn(a, b):
    return lax.dot_general(a, b, (((0,), (0,)), ((), ())), preferred_element_type=F32)


def _sigmoid(v):
    return 1.0 / (1.0 + jnp.exp(-v))


def _layer_norm(v, g, b):
    mu = jnp.mean(v, axis=-1, keepdims=True)
    d = v - mu
    var = jnp.mean(d * d, axis=-1, keepdims=True)
    return d * lax.rsqrt(var + LN_EPS) * g + b


def _inproj_kernel(tiles_per_seq, x_ref, xprev_ref, wtok_ref, wt_ref, convw_ref,
                   kcmp_ref, vcmp_ref, kslc_ref, kwin_ref, cu_ref, qt_ref, gt_ref, vslct_ref, vwint_ref):
    i = pl.program_id(0)
    tm = x_ref.shape[0]
    xb = x_ref[...].astype(BF16)

    def tok(c0, n):
        return _dot(xb, wtok_ref[:, c0:c0 + n])

    kcmp_ref[...] = tok(0, KV_DIM).astype(BF16)
    vcmp_ref[...] = tok(KV_DIM, KV_DIM).astype(BF16)
    kslc_ref[...] = tok(2 * KV_DIM, KV_DIM).astype(BF16)
    kwin_ref[...] = tok(3 * KV_DIM, KV_DIM).astype(BF16)
    c0 = 4 * KV_DIM
    cb = tok(c0, CONV_DIM)
    u = tok(c0 + CONV_DIM, CONV_DIM) * tok(c0 + 2 * CONV_DIM, CONV_DIM)
    xpb = xprev_ref[...].astype(BF16)
    up = (_dot(xpb, wtok_ref[:, c0 + CONV_DIM:c0 + 2 * CONV_DIM])
          * _dot(xpb, wtok_ref[:, c0 + 2 * CONV_DIM:c0 + 3 * CONV_DIM]))
    up = jnp.where(i % tiles_per_seq == 0, 0.0, up)
    row = lax.broadcasted_iota(jnp.int32, (tm, CONV_DIM), 0)
    u1 = jnp.where(row == 0, up[7:8, :], pltpu.roll(u, 1, 0))
    u2 = jnp.where(row == 0, up[6:7, :], jnp.where(row == 1, up[7:8, :], pltpu.roll(u, 2, 0)))
    w = convw_ref[...]
    uc = w[0:1, :] * u2 + w[1:2, :] * u1 + w[2:3, :] * u
    cu_ref[...] = (cb * uc).astype(BF16)

    qt_ref[...] = (_dot_nt(wt_ref[0:Q_DIM, :], xb) * (ATTN_SCALE * LOG2_E)).astype(BF16)
    r0 = Q_DIM
    gt_ref[...] = _sigmoid(_dot_nt(wt_ref[r0:r0 + 2 * GATE_ROWS, :], xb))
    r0 += 2 * GATE_ROWS
    vs = _dot_nt(wt_ref[r0:r0 + KV_DIM, :], xb).astype(BF16)
    vw = _dot_nt(wt_ref[r0 + KV_DIM:r0 + 2 * KV_DIM, :], xb).astype(BF16)
    for c in range(tm // ATT_TILE):
        vslct_ref[c] = vs[:, c * ATT_TILE:(c + 1) * ATT_TILE]
        vwint_ref[c] = vw[:, c * ATT_TILE:(c + 1) * ATT_TILE]


def _in_proj(x2, wtok, wt, conv_w, seq):
    T, D = x2.shape
    tm = 1024
    nt = T // tm
    n_tok = wtok.shape[1]
    n_t = wt.shape[0]
    row_blk = lambda n: pl.BlockSpec((tm, n), lambda i: (i, 0))
    out_shape = (
        jax.ShapeDtypeStruct((T, KV_DIM), BF16),
        jax.ShapeDtypeStruct((T, KV_DIM), BF16),
        jax.ShapeDtypeStruct((T, KV_DIM), BF16),
        jax.ShapeDtypeStruct((T, KV_DIM), BF16),
        jax.ShapeDtypeStruct((T, CONV_DIM), BF16),
        jax.ShapeDtypeStruct((Q_DIM, T), BF16),
        jax.ShapeDtypeStruct((2 * GATE_ROWS, T), F32),
        jax.ShapeDtypeStruct((T // ATT_TILE, KV_DIM, ATT_TILE), BF16),
        jax.ShapeDtypeStruct((T // ATT_TILE, KV_DIM, ATT_TILE), BF16),
    )
    vt_blk = pl.BlockSpec((tm // ATT_TILE, KV_DIM, ATT_TILE), lambda i: (i, 0, 0))
    return pl.pallas_call(
        functools.partial(_inproj_kernel, seq // tm),
        grid=(nt,),
        in_specs=[
            row_blk(D),
            pl.BlockSpec((8, D), lambda i: (jnp.maximum(i * (tm // 8) - 1, 0), 0)),
            pl.BlockSpec((D, n_tok), lambda i: (0, 0)),
            pl.BlockSpec((n_t, D), lambda i: (0, 0)),
            pl.BlockSpec((3, CONV_DIM), lambda i: (0, 0)),
        ],
        out_specs=(
            row_blk(KV_DIM), row_blk(KV_DIM), row_blk(KV_DIM), row_blk(KV_DIM), row_blk(CONV_DIM),
            pl.BlockSpec((Q_DIM, tm), lambda i: (0, i)),
            pl.BlockSpec((2 * GATE_ROWS, tm), lambda i: (0, i)),
            vt_blk, vt_blk,
        ),
        out_shape=out_shape,
        compiler_params=_cparams(1),
        name="in_proj",
    )(x2, x2, wtok, wt, conv_w)


def _compress_kernel(kin_ref, vin_ref, wa_ref, wb_ref, pe_ref, w1_ref, b1_ref, w2_ref, b2_ref,
                     kc_ref, vct_ref):
    def one(idx, in_ref):
        c = in_ref[0]
        a = _dot(c, wa_ref[idx])
        b = _dot(c, wb_ref[idx])
        n = b.shape[0]
        peb = _dot(pe_ref[idx], w1_ref[idx])[0:1, :]
        bias = jnp.concatenate([peb, peb], axis=1) + b1_ref[idx]
        h = a + pltpu.roll(b, n - 1, 0) + bias
        return _dot(jax.nn.gelu(h).astype(BF16), w2_ref[idx]) + b2_ref[idx]

    kc_ref[0] = one(0, kin_ref).astype(BF16)
    vct_ref[0] = one(1, vin_ref).T.astype(BF16)


def _compress(kcmp3, vcmp3, wa, wb, pe, w1, b1, w2, b2):
    B, nch, width = kcmp3.shape
    hid2 = wa.shape[2]
    full = lambda a: pl.BlockSpec(a.shape, lambda b: (0,) * a.ndim)
    in_blk = pl.BlockSpec((1, nch, width), lambda b: (b, 0, 0))
    out_blk = pl.BlockSpec((1, nch, KV_DIM), lambda b: (b, 0, 0))
    return pl.pallas_call(
        _compress_kernel,
        grid=(B,),
        in_specs=[in_blk, in_blk, full(wa), full(wb), full(pe), full(w1), full(b1), full(w2), full(b2)],
        out_specs=(out_blk, pl.BlockSpec((1, KV_DIM, nch), lambda b: (b, 0, 0))),
        out_shape=(jax.ShapeDtypeStruct((B, nch, KV_DIM), BF16),
                   jax.ShapeDtypeStruct((B, KV_DIM, nch), BF16)),
        compiler_params=_cparams(1),
        name="compress",
    )(kcmp3, vcmp3, wa, wb, pe, w1, b1, w2, b2)


def _group_q(qt_ref, hh, g_is0):
    qh = qt_ref[hh * HEAD_DIM:(hh + 1) * HEAD_DIM, :]
    z = jnp.zeros_like(qh)
    return jnp.concatenate([qh, z], axis=0) if g_is0 else jnp.concatenate([z, qh], axis=0)


def _cmpattn_kernel(n_cmp, qt_ref, kc_ref, vct_ref, gt_ref, ovt_ref, ocmpt_ref, selb_ref):
    i = pl.program_id(1)
    tq = qt_ref.shape[1]
    nc = kc_ref.shape[1]
    n_sel = ovt_ref.shape[0]
    kc = kc_ref[0]
    t_n = i * tq + lax.broadcasted_iota(jnp.int32, (nc, tq), 1)
    n_io = lax.broadcasted_iota(jnp.int32, (nc, tq), 0)
    blk_end = jnp.where(n_io < n_cmp, n_io * CMP_STRIDE + CMP_BLOCK - 1, jnp.int32(2 ** 30))
    vis = blk_end <= t_n
    j_io = lax.broadcasted_iota(jnp.int32, (n_sel, tq), 0)
    t_j = i * tq + lax.broadcasted_iota(jnp.int32, (n_sel, tq), 1)
    cur = t_j // SEL_BLOCK
    bonus = jnp.where(j_io == 0, FORCE_BONUS,
                      jnp.where(j_io == cur, FORCE_BONUS, jnp.where(j_io == cur - 1, FORCE_BONUS, 0.0)))
    valid = j_io * SEL_BLOCK <= t_j
    ktiles = selb_ref.shape[0] // N_GROUPS_KV
    blocks_per_tile = n_sel // ktiles
    for g in range(N_GROUPS_KV):
        imp = jnp.zeros((n_sel, tq), F32)
        for hh in range(HEADS_PER_GROUP):
            h = g * HEADS_PER_GROUP + hh
            s = _dot(kc, _group_q(qt_ref, h, g == 0))
            s = jnp.where(vis, s, NEG_INF)
            m = jnp.max(s, axis=0, keepdims=True)
            p = jnp.where(vis, jnp.exp2(s - m), 0.0)
            l = jnp.sum(p, axis=0, keepdims=True)
            pr = (p * jnp.where(l > 0.0, 1.0 / l, 0.0)).astype(BF16)
            o = _dot(vct_ref[0, g * HEAD_DIM:(g + 1) * HEAD_DIM, :], pr)
            gate = gt_ref[g * GATE_ROWS + hh * N_BRANCH:g * GATE_ROWS + hh * N_BRANCH + 1, :]
            ocmpt_ref[h * HEAD_DIM:(h + 1) * HEAD_DIM, :] = (o * gate).astype(BF16)
            imp = imp + _dot(ovt_ref[...], pr)
        score = jnp.where(valid, imp + bonus, -FORCE_BONUS)
        cnt = jnp.zeros((n_sel, tq), F32)
        for jp in range(n_sel):
            r = score[jp:jp + 1, :]
            cnt = cnt + jnp.where(r > score, 1.0, jnp.where(r == score, jnp.where(j_io > jp, 1.0, 0.0), 0.0))
        bias = jnp.where(cnt < float(min(SEL_TOPK, n_sel)), 0.0, NEG_INF)
        for jt in range(ktiles):
            selb_ref[g * ktiles + jt, 0:blocks_per_tile, :] = bias[jt * blocks_per_tile:(jt + 1) * blocks_per_tile, :]
            selb_ref[g * ktiles + jt, blocks_per_tile:8, :] = jnp.zeros((8 - blocks_per_tile, tq), F32)


def _cmp_attn(qt, kc, vct, gt, ovt, B, seq, n_cmp):
    T = qt.shape[1]
    tq = ATT_TILE
    nq = seq // tq
    ktiles = seq // ATT_TILE
    nc = kc.shape[1]
    return pl.pallas_call(
        functools.partial(_cmpattn_kernel, n_cmp),
        grid=(B, nq),
        in_specs=[
            pl.BlockSpec((Q_DIM, tq), lambda b, i: (0, b * nq + i)),
            pl.BlockSpec((1, nc, KV_DIM), lambda b, i: (b, 0, 0)),
            pl.BlockSpec((1, KV_DIM, nc), lambda b, i: (b, 0, 0)),
            pl.BlockSpec((2 * GATE_ROWS, tq), lambda b, i: (0, b * nq + i)),
            pl.BlockSpec(ovt.shape, lambda b, i: (0, 0)),
        ],
        out_specs=(
            pl.BlockSpec((Q_DIM, tq), lambda b, i: (0, b * nq + i)),
            pl.BlockSpec((N_GROUPS_KV * ktiles, 8, tq), lambda b, i: (b * nq + i, 0, 0)),
        ),
        out_shape=(jax.ShapeDtypeStruct((Q_DIM, T), BF16),
                   jax.ShapeDtypeStruct((B * nq * N_GROUPS_KV * ktiles, 8, tq), F32)),
        compiler_params=_cparams(2),
        name="cmp_attn",
    )(qt, kc, vct, gt, ovt)


def _nsa_kernel(qt_ref, kslc_ref, kwin_ref, vslct_ref, vwint_ref, selb_ref, gt_ref, ocmpt_ref,
                ot_ref, qpad_s, m_s, acc_s):
    i = pl.program_id(1)
    g = pl.program_id(2)
    tq = qt_ref.shape[1]
    tile = ATT_TILE
    blocks_per_tile = tile // SEL_BLOCK
    wide = HEADS_PER_GROUP * tq

    row_group = lax.broadcasted_iota(jnp.int32, (KV_DIM, tq), 0) // HEAD_DIM
    for hh in range(HEADS_PER_GROUP):
        qh = qt_ref[hh * HEAD_DIM:(hh + 1) * HEAD_DIM, :]
        q2 = jnp.concatenate([qh, qh], axis=0)
        qpad_s[:, hh * tq:(hh + 1) * tq] = jnp.where(row_group == g, q2, jnp.zeros_like(q2))

    def key_minus_query(n_keys):
        return (lax.broadcasted_iota(jnp.int32, (n_keys, wide), 0)
                - lax.broadcasted_iota(jnp.int32, (n_keys, wide), 1) % tq)

    def v_rows(ref, tiles):
        vt = jnp.concatenate([ref[j] for j in tiles], axis=1)
        return jnp.concatenate([vt, jnp.ones((SUM_ROWS, vt.shape[1]), BF16)], axis=0)

    def sel_bias(j):
        sb = selb_ref[j]
        one = jnp.concatenate(
            [jnp.broadcast_to(sb[q:q + 1, :], (SEL_BLOCK, tq)) for q in range(blocks_per_tile)], axis=0)
        return jnp.concatenate([one] * HEADS_PER_GROUP, axis=1)

    m_s[...] = jnp.full(m_s.shape, NEG_INF, F32)
    acc_s[...] = jnp.zeros(acc_s.shape, F32)

    def sel_pair(jj, causal):
        k_tile = kslc_ref[pl.ds(pl.multiple_of(jj * 2 * tile, 2 * tile), 2 * tile), :]
        s = _dot(k_tile, qpad_s[...])
        s = s + jnp.concatenate([sel_bias(2 * jj), sel_bias(2 * jj + 1)], axis=0)
        if causal:
            s = jnp.where(key_minus_query(2 * tile) <= (i - 2 * jj) * tile, s, NEG_INF)
        m_old = m_s[...]
        m_new = jnp.maximum(m_old, jnp.max(s, axis=0, keepdims=True))
        p = jnp.exp2(s - m_new).astype(BF16)
        acc_s[...] = jnp.exp2(m_old - m_new) * acc_s[...] + _dot(v_rows(vslct_ref, (2 * jj, 2 * jj + 1)), p)
        m_s[...] = m_new

    def sel_body(jj, carry):
        sel_pair(jj, False)
        return carry

    lax.fori_loop(0, i // 2, sel_body, 0)
    sel_pair(i // 2, True)
    o_slc = acc_s[0:HEAD_DIM, :] * (1.0 / acc_s[HEAD_DIM:HEAD_DIM + 1, :])

    j0 = jnp.maximum(i - WINDOW // tile, 0)
    n_win = WINDOW + tile
    s = _dot(kwin_ref[pl.ds(pl.multiple_of(j0 * tile, tile), n_win), :], qpad_s[...])
    behind = (i - j0) * tile - key_minus_query(n_win)
    s = jnp.where(pltpu.bitcast(behind, jnp.uint32) < WINDOW, s, NEG_INF)
    p = jnp.exp2(s - jnp.max(s, axis=0, keepdims=True)).astype(BF16)
    ow = _dot(v_rows(vwint_ref, (j0, j0 + 1, j0 + 2)), p)
    o_win = ow[0:HEAD_DIM, :] * (1.0 / ow[HEAD_DIM:HEAD_DIM + 1, :])

    for hh in range(HEADS_PER_GROUP):
        lanes = slice(hh * tq, (hh + 1) * tq)
        g_slc = gt_ref[hh * N_BRANCH + 1:hh * N_BRANCH + 2, :]
        g_win = gt_ref[hh * N_BRANCH + 2:hh * N_BRANCH + 3, :]
        o = (ocmpt_ref[hh * HEAD_DIM:(hh + 1) * HEAD_DIM, :].astype(F32)
             + g_slc * o_slc[:, lanes] + g_win * o_win[:, lanes])
        ot_ref[hh * HEAD_DIM:(hh + 1) * HEAD_DIM, :] = o.astype(BF16)


def _nsa_attn(qt, kslc, kwin, vslct, vwint, selb, gt, ocmpt, B, seq):
    T = qt.shape[1]
    tq = ATT_TILE
    nq = seq // tq
    ktiles = seq // ATT_TILE
    gq = HEADS_PER_GROUP * HEAD_DIM
    qblk = pl.BlockSpec((gq, tq), lambda b, i, g: (g, b * nq + i))
    kblk = pl.BlockSpec((seq, KV_DIM), lambda b, i, g: (b, 0))
    vblk = pl.BlockSpec((ktiles, HEAD_DIM, ATT_TILE), lambda b, i, g: (b, g, 0))
    return pl.pallas_call(
        _nsa_kernel,
        grid=(B, nq, N_GROUPS_KV),
        in_specs=[
            qblk, kblk, kblk, vblk, vblk,
            pl.BlockSpec((ktiles, 8, tq), lambda b, i, g: ((b * nq + i) * N_GROUPS_KV + g, 0, 0)),
            pl.BlockSpec((GATE_ROWS, tq), lambda b, i, g: (g, b * nq + i)),
            qblk,
        ],
        out_specs=qblk,
        out_shape=jax.ShapeDtypeStruct((Q_DIM, T), BF16),
        scratch_shapes=[
            pltpu.VMEM((KV_DIM, HEADS_PER_GROUP * tq), BF16),
            pltpu.VMEM((1, HEADS_PER_GROUP * tq), F32),
            pltpu.VMEM((HEAD_DIM + SUM_ROWS, HEADS_PER_GROUP * tq), F32),
        ],
        compiler_params=_cparams(3),
        name="nsa_attn",
    )(qt, kslc, kwin, vslct, vwint, selb, gt, ocmpt)


def _route(logits):
    lane = lax.broadcasted_iota(jnp.int32, logits.shape, 1)
    is_grp = (lane >= N_EXPERTS) & (lane < N_EXPERTS + N_EXPERT_GROUPS)
    gmax = jnp.max(jnp.where(is_grp, logits, NEG_INF), axis=1, keepdims=True)
    ge = jnp.where(is_grp, jnp.exp(logits - gmax), 0.0)
    gp = ge / jnp.sum(ge, axis=1, keepdims=True)
    g_top = jnp.max(gp, axis=1, keepdims=True)
    g_idx = jnp.min(jnp.where(is_grp, jnp.where(gp == g_top, lane, 2 ** 20), 2 ** 20),
                    axis=1, keepdims=True) - N_EXPERTS
    chosen = (lane < N_EXPERTS) & (lane // EXPERTS_PER_GROUP == g_idx)
    cmax = jnp.max(jnp.where(chosen, logits, NEG_INF), axis=1, keepdims=True)
    ce = jnp.where(chosen, jnp.exp(logits - cmax), 0.0)
    cp = ce / jnp.sum(ce, axis=1, keepdims=True)
    v1 = jnp.max(jnp.where(chosen, cp, -1.0), axis=1, keepdims=True)
    i1 = jnp.min(jnp.where(chosen, jnp.where(cp == v1, lane, 2 ** 20), 2 ** 20), axis=1, keepdims=True)
    rest = chosen & (lane != i1)
    v2 = jnp.max(jnp.where(rest, cp, -1.0), axis=1, keepdims=True)
    i2 = jnp.min(jnp.where(rest, jnp.where(cp == v2, lane, 2 ** 20), 2 ** 20), axis=1, keepdims=True)
    tot = v1 + v2
    return lane, i1, i2, v1 / tot * g_top, v2 / tot * g_top


R_E1, R_E2, R_C1, R_C2, R_RANK1, R_RANK2 = range(6)


def _lane_pick(lane, idx, row):
    return jnp.sum(jnp.where(lane == idx, row, 0.0), axis=1, keepdims=True)


def _merge_kernel(alpha, ot_ref, cu_ref, x_ref, wn_ref, wc_ref, wgm_ref, wo_ref, g1_ref, b1_ref,
                  wr_ref, br_ref, h1_ref, route_ref, cnt_ref):
    tm, d = x_ref.shape
    x = x_ref[...]
    xb = x.astype(BF16)
    y_nsa = _dot_tn(ot_ref[...], wn_ref[...])
    y_conv = _dot(cu_ref[...], wc_ref[...])
    merged = (_sigmoid(_dot(xb, wgm_ref[:, 0:d])) * y_nsa
              + _sigmoid(_dot(xb, wgm_ref[:, d:2 * d])) * y_conv)
    mix = _dot(merged.astype(BF16), wo_ref[...])
    h1 = _layer_norm(alpha * x + mix, g1_ref[...], b1_ref[...])
    h1_ref[...] = h1
    lane, e1, e2, c1, c2 = _route(_dot(h1.astype(BF16), wr_ref[...]) + br_ref[...])
    onehot = jnp.where(lane == e1, 1.0, jnp.where(lane == e2, 1.0, 0.0))
    tri = jnp.where(lax.broadcasted_iota(jnp.int32, (MOE_TILE, MOE_TILE), 0)
                    > lax.broadcasted_iota(jnp.int32, (MOE_TILE, MOE_TILE), 1), 1.0, 0.0).astype(BF16)
    before = []
    for sub in range(tm // MOE_TILE):
        oh = onehot[sub * MOE_TILE:(sub + 1) * MOE_TILE]
        before.append(_dot(tri, oh.astype(BF16)))
        cnt_ref[sub] = jnp.broadcast_to(jnp.sum(oh, axis=0, keepdims=True), (8, LANES))
    before = jnp.concatenate(before, axis=0)
    rec = jnp.zeros((tm, LANES), F32)
    for k, v in ((R_E1, e1.astype(F32)), (R_E2, e2.astype(F32)), (R_C1, c1), (R_C2, c2),
                 (R_RANK1, _lane_pick(lane, e1, before)), (R_RANK2, _lane_pick(lane, e2, before))):
        rec = jnp.where(lane == k, v, rec)
    route_ref[...] = rec


def _merge(ot, cu, x2, wn, wc, wgm, wo, g1, b1, wr, br, alpha):
    T, D = x2.shape
    tm = 512
    sub = tm // MOE_TILE
    full = lambda a: pl.BlockSpec(a.shape, lambda i: (0,) * a.ndim)
    return pl.pallas_call(
        functools.partial(_merge_kernel, alpha),
        grid=(T // tm,),
        in_specs=[
            pl.BlockSpec((Q_DIM, tm), lambda i: (0, i)),
            pl.BlockSpec((tm, CONV_DIM), lambda i: (i, 0)),
            pl.BlockSpec((tm, D), lambda i: (i, 0)),
            full(wn), full(wc), full(wgm), full(wo), full(g1), full(b1), full(wr), full(br),
        ],
        out_specs=(pl.BlockSpec((tm, D), lambda i: (i, 0)), pl.BlockSpec((tm, LANES), lambda i: (i, 0)),
                   pl.BlockSpec((sub, 8, LANES), lambda i: (i, 0, 0))),
        out_shape=(jax.ShapeDtypeStruct((T, D), F32), jax.ShapeDtypeStruct((T, LANES), F32),
                   jax.ShapeDtypeStruct((T // MOE_TILE, 8, LANES), F32)),
        compiler_params=_cparams(1),
        name="merge",
    )(ot, cu, x2, wn, wc, wgm, wo, g1, b1, wr, br)


def _local_slots(route_ref, lstart_ref):
    rec = route_ref[...]
    lane = lax.broadcasted_iota(jnp.int32, rec.shape, 1)
    field = lambda k: _lane_pick(lane, k, rec)
    lstart = lstart_ref[0][0:1, :]
    slot = lambda ek, rk: (_lane_pick(lane, field(ek).astype(jnp.int32), lstart) + field(rk)).astype(jnp.int32)
    return slot(R_E1, R_RANK1), slot(R_E2, R_RANK2), field(R_C1), field(R_C2)


def _chunk_rows(q):
    return pl.ds(pl.multiple_of(q * CHUNK, CHUNK), CHUNK)


def _dispatch_kernel(tab_ref, tab_m1_ref, tab_m2_ref, h1_ref, route_ref, lstart_ref, xs_init_ref, xs_ref,
                     buf, sem):
    del xs_init_ref
    i = pl.program_id(0)
    last = pl.num_programs(0) - 1
    tm = h1_ref.shape[0]
    slot = i % 2

    def drain(tab, slot_):
        def wait(q, c):
            pltpu.make_async_copy(buf.at[slot_, _chunk_rows(0)], xs_ref.at[pl.ds(0, CHUNK)], sem.at[slot_]).wait()
            return c
        lax.fori_loop(0, tab[0, 0, TAB_COUNT], wait, 0)

    s1, s2, _, _ = _local_slots(route_ref, lstart_ref)
    s_io = lax.broadcasted_iota(jnp.int32, (tm, LOCAL_ROWS), 1)
    perm = jnp.where(s_io == s1, 1.0, jnp.where(s_io == s2, 1.0, 0.0)).astype(BF16)
    srt = _dot_tn(perm, h1_ref[...].astype(BF16)).astype(BF16)

    @pl.when(i >= 2)
    def _():
        drain(tab_m2_ref, slot)

    buf[slot] = srt

    def start(q, c):
        dst = pl.multiple_of(tab_ref[0, 0, q], CHUNK)
        pltpu.make_async_copy(buf.at[slot, _chunk_rows(q)], xs_ref.at[pl.ds(dst, CHUNK)], sem.at[slot]).start()
        return c

    lax.fori_loop(0, tab_ref[0, 0, TAB_COUNT], start, 0)

    @pl.when(i == last)
    def _():
        drain(tab_ref, slot)

    @pl.when(jnp.logical_and(i == last, i >= 1))
    def _():
        drain(tab_m1_ref, 1 - slot)


def _dispatch(tab, h1, route, lstart, n_rows):
    T, D = h1.shape
    tm = MOE_TILE
    xs_init = jnp.zeros((n_rows, D), BF16)
    tab_blk = lambda back: pl.BlockSpec((1, 1, LANES), lambda i: (jnp.maximum(i - back, 0), 0, 0),
                                        memory_space=pltpu.SMEM)
    return pl.pallas_call(
        _dispatch_kernel,
        grid=(T // tm,),
        in_specs=[
            tab_blk(0), tab_blk(1), tab_blk(2),
            pl.BlockSpec((tm, D), lambda i: (i, 0)),
            pl.BlockSpec((tm, LANES), lambda i: (i, 0)),
            pl.BlockSpec((1, 8, LANES), lambda i: (i, 0, 0)),
            pl.BlockSpec(memory_space=pl.ANY),
        ],
        out_specs=pl.BlockSpec(memory_space=pl.ANY),
        out_shape=jax.ShapeDtypeStruct((n_rows, D), BF16),
        scratch_shapes=[pltpu.VMEM((2, LOCAL_ROWS, D), BF16), pltpu.SemaphoreType.DMA((2,))],
        input_output_aliases={6: 0},
        compiler_params=_cparams(1),
        name="moe_dispatch",
    )(tab, tab, tab, h1, route, lstart, xs_init)


def _experts_kernel(te_ref, nu_ref, xs_ref, wg_ref, wu_ref, wd_ref, ys_ref):
    del te_ref
    j = pl.program_id(0)

    @pl.when(j < nu_ref[0])
    def _():
        xb = xs_ref[...]
        hg = _dot(xb, wg_ref[0])
        h = hg * _sigmoid(hg) * _dot(xb, wu_ref[0])
        y = _dot(h.astype(BF16), wd_ref[0])
        ys_ref[...] = y.astype(BF16)

    @pl.when(j >= nu_ref[0])
    def _():
        ys_ref[...] = jnp.zeros(ys_ref.shape, ys_ref.dtype)


def _experts(tile_expert, n_used, xs, wg, wu, wd):
    n_rows, D = xs.shape
    hid = wg.shape[2]
    tm = EXPERT_TILE
    used = lambda j, te, nu: (jnp.minimum(j, nu[0] - 1), 0)
    grid_spec = pltpu.PrefetchScalarGridSpec(
        num_scalar_prefetch=2,
        grid=(n_rows // tm,),
        in_specs=[
            pl.BlockSpec((tm, D), used),
            pl.BlockSpec((1, D, hid), lambda j, te, nu: (te[j], 0, 0)),
            pl.BlockSpec((1, D, hid), lambda j, te, nu: (te[j], 0, 0)),
            pl.BlockSpec((1, hid, D), lambda j, te, nu: (te[j], 0, 0)),
        ],
        out_specs=pl.BlockSpec((tm, D), lambda j, te, nu: (j, 0)),
    )
    return pl.pallas_call(
        _experts_kernel,
        grid_spec=grid_spec,
        out_shape=jax.ShapeDtypeStruct((n_rows, D), BF16),
        compiler_params=_cparams(1),
        name="moe_experts",
    )(tile_expert, n_used, xs, wg, wu, wd)


def _combine_kernel(alpha, tab_ref, tab_next_ref, route_ref, lstart_ref, h1_ref, p_ref, pproj_ref, pgw_ref,
                    pgb_ref, g2_ref, b2_ref, ys_ref, out_ref, buf, sem):
    i = pl.program_id(0)
    tm = h1_ref.shape[0]
    slot = i % 2

    def gather(tab, slot_):
        def start(q, c):
            src = pl.multiple_of(tab[0, 0, q], CHUNK)
            pltpu.make_async_copy(ys_ref.at[pl.ds(src, CHUNK)], buf.at[slot_, _chunk_rows(q)],
                                  sem.at[slot_]).start()
            return c
        lax.fori_loop(0, tab[0, 0, TAB_COUNT], start, 0)

    @pl.when(i == 0)
    def _():
        buf[...] = jnp.zeros(buf.shape, buf.dtype)
        gather(tab_ref, 0)

    @pl.when(i + 1 < pl.num_programs(0))
    def _():
        gather(tab_next_ref, 1 - slot)

    h1 = h1_ref[...]
    ple = (_sigmoid(_dot(h1.astype(BF16), pgw_ref[...]) + pgb_ref[...])
           * _dot(p_ref[...].astype(BF16), pproj_ref[...]))

    def wait(q, c):
        pltpu.make_async_copy(ys_ref.at[pl.ds(0, CHUNK)], buf.at[slot, _chunk_rows(0)], sem.at[slot]).wait()
        return c

    lax.fori_loop(0, tab_ref[0, 0, TAB_COUNT], wait, 0)

    s1, s2, c1, c2 = _local_slots(route_ref, lstart_ref)
    s_io = lax.broadcasted_iota(jnp.int32, (tm, LOCAL_ROWS), 1)
    weights = jnp.where(s_io == s1, c1, jnp.where(s_io == s2, c2, 0.0)).astype(BF16)
    ffn = _dot(weights, buf[slot])
    out_ref[...] = _layer_norm(alpha * h1 + ffn + ple, g2_ref[...], b2_ref[...])


def _combine(tab, route, lstart, h1, p2, pproj, pgw, pgb, g2, b2, ys, alpha):
    T, D = h1.shape
    tm = MOE_TILE
    nt = T // tm
    full = lambda a: pl.BlockSpec(a.shape, lambda i: (0,) * a.ndim)
    return pl.pallas_call(
        functools.partial(_combine_kernel, alpha),
        grid=(nt,),
        in_specs=[
            pl.BlockSpec((1, 1, LANES), lambda i: (i, 0, 0), memory_space=pltpu.SMEM),
            pl.BlockSpec((1, 1, LANES), lambda i: (jnp.minimum(i + 1, nt - 1), 0, 0), memory_space=pltpu.SMEM),
            pl.BlockSpec((tm, LANES), lambda i: (i, 0)),
            pl.BlockSpec((1, 8, LANES), lambda i: (i, 0, 0)),
            pl.BlockSpec((tm, D), lambda i: (i, 0)),
            pl.BlockSpec((tm, p2.shape[1]), lambda i: (i, 0)),
            full(pproj), full(pgw), full(pgb), full(g2), full(b2),
            pl.BlockSpec(memory_space=pl.ANY),
        ],
        out_specs=pl.BlockSpec((tm, D), lambda i: (i, 0)),
        out_shape=jax.ShapeDtypeStruct((T, D), F32),
        scratch_shapes=[pltpu.VMEM((2, LOCAL_ROWS, D), BF16), pltpu.SemaphoreType.DMA((2,))],
        compiler_params=_cparams(1),
        name="moe_combine",
    )(tab, tab, route, lstart, h1, p2, pproj, pgw, pgb, g2, b2, ys)


def _moe_plan(cnt_tiles, n_expert_tiles):
    cnt = cnt_tiles[:, 0, :N_EXPERTS].astype(jnp.int32)
    cnt8 = (cnt + CHUNK - 1) // CHUNK * CHUNK
    lend = jnp.cumsum(cnt8, axis=1)
    lstart = lend - cnt8
    gend = jnp.cumsum(cnt8, axis=0)
    region = (gend[-1] + EXPERT_TILE - 1) // EXPERT_TILE * EXPERT_TILE
    oend = jnp.cumsum(region)
    gstart = (oend - region)[None, :] + gend - cnt8
    q8 = jnp.arange(MAX_CHUNKS, dtype=jnp.int32) * CHUNK
    eq = jnp.minimum(jnp.sum((lend[:, None, :] <= q8[None, :, None]).astype(jnp.int32), axis=-1), N_EXPERTS - 1)
    shift = jnp.sum(jnp.where(eq[:, :, None] == jnp.arange(N_EXPERTS)[None, None, :],
                              (gstart - lstart)[:, None, :], 0), axis=-1)
    dstq = shift + q8[None, :]
    nt = cnt.shape[0]
    tab = jnp.zeros((nt, LANES), jnp.int32).at[:, :MAX_CHUNKS].set(dstq).at[:, TAB_COUNT].set(lend[:, -1] // CHUNK)
    tile_row = jnp.arange(n_expert_tiles, dtype=jnp.int32) * EXPERT_TILE
    tile_expert = jnp.minimum(jnp.sum((oend[None, :] <= tile_row[:, None]).astype(jnp.int32), axis=1),
                              N_EXPERTS - 1)
    n_used = (oend[-1] // EXPERT_TILE).reshape(1)
    lstart_f = jnp.pad(lstart.astype(F32), ((0, 0), (0, LANES - N_EXPERTS)))
    return (tab.reshape(nt, 1, LANES), jnp.broadcast_to(lstart_f[:, None, :], (nt, 8, LANES)),
            tile_expert, n_used)


def _split_w_in(w_in, D):
    sizes = [Q_DIM] + [KV_DIM] * 6 + [N_HEADS * N_BRANCH] + [CONV_DIM] * 3 + [D] * 2
    offs = np.concatenate([[0], np.cumsum(sizes)])
    names = ["q", "k_cmp", "v_cmp", "k_slc", "v_slc", "k_win", "v_win", "g_nsa",
             "conv_b", "conv_c", "conv_h", "g_m_nsa", "g_m_conv"]
    return {n: w_in[:, int(offs[k]):int(offs[k + 1])] for k, n in enumerate(names)}


def _layer(x2, p2, B, seq, depth, w_in, cmp_pe, cmp_w1, cmp_b1, cmp_w2, cmp_b2, conv_w, w_nsa_out,
           w_conv_out, w_o, ln1_g, ln1_b, rg_w, rg_b, re_w, re_b, e_wg, e_wu, e_wd, ple_proj,
           ple_gate_w, ple_gate_b, ln2_g, ln2_b):
    T, D = x2.shape
    alpha = (2.0 * depth) ** 0.25
    w = _split_w_in(w_in, D)
    wtok = jnp.concatenate([w["k_cmp"], w["v_cmp"], w["k_slc"], w["k_win"],
                            w["conv_b"], w["conv_c"], w["conv_h"]], axis=1).astype(BF16)
    gcols = w["g_nsa"].reshape(D, N_GROUPS_KV, HEADS_PER_GROUP * N_BRANCH)
    gcols = jnp.pad(gcols, ((0, 0), (0, 0), (0, GATE_ROWS - HEADS_PER_GROUP * N_BRANCH)))
    wt = jnp.concatenate([w["q"], gcols.reshape(D, N_GROUPS_KV * GATE_ROWS), w["v_slc"], w["v_win"]],
                         axis=1).T.astype(BF16)
    kcmp, vcmp, kslc, kwin, cu, qt, gt, vslct, vwint = _in_proj(x2, wtok, wt, conv_w, seq)

    half = CMP_BLOCK // 2
    n_chunks = seq // CMP_STRIDE
    n_cmp = (seq - CMP_BLOCK) // CMP_STRIDE + 1
    hidden = cmp_w1.shape[-1]
    eye = jnp.eye(N_GROUPS_KV, dtype=F32)
    w1r = cmp_w1.reshape(2, CMP_BLOCK, HEAD_DIM, hidden)
    expand = lambda m: jnp.einsum("ildh,gk->ilgdkh", m, eye).reshape(
        2, half * KV_DIM, N_GROUPS_KV * hidden).astype(BF16)
    wa, wb = expand(w1r[:, :half]), expand(w1r[:, half:])
    pe = jnp.broadcast_to(cmp_pe.reshape(2, 1, CMP_BLOCK * HEAD_DIM), (2, 8, CMP_BLOCK * HEAD_DIM)).astype(BF16)
    b1t = jnp.tile(cmp_b1.reshape(2, 1, hidden), (1, 1, N_GROUPS_KV))
    w2b = jnp.einsum("ihd,gk->ighkd", cmp_w2, eye).reshape(2, N_GROUPS_KV * hidden, KV_DIM).astype(BF16)
    b2t = jnp.tile(cmp_b2.reshape(2, 1, HEAD_DIM), (1, 1, N_GROUPS_KV))
    kc, vct = _compress(kcmp.reshape(B, n_chunks, CMP_STRIDE * KV_DIM),
                        vcmp.reshape(B, n_chunks, CMP_STRIDE * KV_DIM),
                        wa, wb, pe, cmp_w1.astype(BF16), b1t, w2b, b2t)

    n_sel = seq // SEL_BLOCK
    c_start = np.arange(n_chunks) * CMP_STRIDE
    s_start = np.arange(n_sel) * SEL_BLOCK
    overlap = ((c_start[None, :] <= s_start[:, None] + SEL_BLOCK - 1)
               & (c_start[None, :] + CMP_BLOCK - 1 >= s_start[:, None])).astype(np.float32)
    ocmpt, selb = _cmp_attn(qt, kc, vct, gt, jnp.asarray(overlap, BF16), B, seq, n_cmp)
    ot = _nsa_attn(qt, kslc, kwin, vslct, vwint, selb, gt, ocmpt, B, seq)

    wgm = jnp.concatenate([w["g_m_nsa"], w["g_m_conv"]], axis=1).astype(BF16)
    wr = jnp.pad(jnp.concatenate([re_w, rg_w], axis=1), ((0, 0), (0, LANES - N_EXPERTS - N_EXPERT_GROUPS)))
    br = jnp.pad(jnp.concatenate([re_b, rg_b]), (0, LANES - N_EXPERTS - N_EXPERT_GROUPS)).reshape(1, LANES)
    h1, route, cnt_tiles = _merge(ot, cu, x2, w_nsa_out.astype(BF16), w_conv_out.astype(BF16), wgm,
                                  w_o.astype(BF16), ln1_g.reshape(1, D), ln1_b.reshape(1, D),
                                  wr.astype(BF16), br, alpha)
    n_tok_tiles = T // MOE_TILE
    max_rows = 2 * T + n_tok_tiles * N_EXPERTS * (CHUNK - 1) + N_EXPERTS * (EXPERT_TILE - 1)
    n_expert_tiles = -(-max_rows // EXPERT_TILE)
    tab, lstart, tile_expert, n_used = _moe_plan(cnt_tiles, n_expert_tiles)
    xs = _dispatch(tab, h1, route, lstart, n_expert_tiles * EXPERT_TILE)
    ys = _experts(tile_expert, n_used, xs, e_wg.astype(BF16), e_wu.astype(BF16), e_wd.astype(BF16))
    return _combine(tab, route, lstart, h1, p2, ple_proj.astype(BF16), ple_gate_w.astype(BF16),
                    ple_gate_b.reshape(1, D), ln2_g.reshape(1, D), ln2_b.reshape(1, D), ys, alpha)


def kernel(x, p, w_in, cmp_pe, cmp_w1, cmp_b1, cmp_w2, cmp_b2, conv_w, w_nsa_out, w_conv_out, w_o, ln1_g, ln1_b, router_group_w, router_group_b, router_expert_w, router_expert_b, expert_w_gate, expert_w_up, expert_w_down, ple_proj, ple_gate_w, ple_gate_b, ln2_g, ln2_b):
    B, seq, D = x.shape
    depth = w_in.shape[0]
    x2 = x.reshape(B * seq, D)
    for i in range(depth):
        x2 = _layer(x2, p[i].reshape(B * seq, -1), B, seq, depth, w_in[i], cmp_pe[i], cmp_w1[i], cmp_b1[i],
                    cmp_w2[i], cmp_b2[i], conv_w[i], w_nsa_out[i], w_conv_out[i], w_o[i], ln1_g[i], ln1_b[i],
                    router_group_w[i], router_group_b[i], router_expert_w[i], router_expert_b[i],
                    expert_w_gate[i], expert_w_up[i], expert_w_down[i], ple_proj[i], ple_gate_w[i],
                    ple_gate_b[i], ln2_g[i], ln2_b[i])
    return x2.reshape(B, seq, D)
```

```python
import functools

import jax
import jax.numpy as jnp
import numpy as np
from jax import lax
from jax.experimental import pallas as pl
from jax.experimental.pallas import tpu as pltpu

F32 = jnp.float32
BF16 = jnp.bfloat16

N_HEADS = 8
N_GROUPS_KV = 2
HEADS_PER_GROUP = N_HEADS // N_GROUPS_KV
HEAD_DIM = 64
Q_DIM = N_HEADS * HEAD_DIM
KV_DIM = N_GROUPS_KV * HEAD_DIM
N_BRANCH = 3
CMP_BLOCK = 32
CMP_STRIDE = 16
SEL_BLOCK = 64
SEL_TOPK = 16
WINDOW = 512
CONV_DIM = 512
N_EXPERT_GROUPS = 4
EXPERTS_PER_GROUP = 8
N_EXPERTS = N_EXPERT_GROUPS * EXPERTS_PER_GROUP
ATTN_SCALE = HEAD_DIM ** -0.5
LOG2_E = 1.4426950408889634
FORCE_BONUS = 1e4
NEG_INF = -1e30
LN_EPS = 1e-5

LANES = 128
ATT_TILE = 256
GATE_ROWS = 16
SUM_ROWS = 16
VMEM_LIMIT = 48 * 1024 * 1024

MOE_TILE = 256
EXPERT_TILE = 512
CHUNK = 16
LOCAL_ROWS = -(-(2 * MOE_TILE + N_EXPERTS * (CHUNK - 1)) // ATT_TILE) * ATT_TILE
MAX_CHUNKS = LOCAL_ROWS // CHUNK
TAB_COUNT = LANES - 1


def _cparams(n_axes):
    return pltpu.CompilerParams(dimension_semantics=("arbitrary",) * n_axes,
                                vmem_limit_bytes=VMEM_LIMIT)


def _dot(a, b):
    return jnp.dot(a, b, preferred_element_type=F32)


def _dot_nt(a, b):
    return lax.dot_general(a, b, (((1,), (1,)), ((), ())), preferred_element_type=F32)


def _dot_tn(a, b):
    return lax.dot_general(a, b, (((0,), (0,)), ((), ())), preferred_element_type=F32)


def _sigmoid(v):
    return 1.0 / (1.0 + jnp.exp(-v))


def _layer_norm(v, g, b):
    mu = jnp.mean(v, axis=-1, keepdims=True)
    d = v - mu
    var = jnp.mean(d * d, axis=-1, keepdims=True)
    return d * lax.rsqrt(var + LN_EPS) * g + b


def _inproj_kernel(seq, x_ref, xprev_ref, wtok_ref, wt_ref, convw_ref,
                   kcmp_ref, vcmp_ref, kslc_ref, kwin_ref, cu_ref, qt_ref, gt_ref, vslct_ref, vwint_ref):
    i = pl.program_id(0)
    tm = x_ref.shape[0]
    xb = x_ref[...].astype(BF16)

    c0 = 4 * KV_DIM
    kv = _dot(xb, wtok_ref[:, 0:c0]).astype(BF16)
    kcmp_ref[...] = kv[:, 0:KV_DIM]
    vcmp_ref[...] = kv[:, KV_DIM:2 * KV_DIM]
    kwin_ref[...] = kv[:, 3 * KV_DIM:4 * KV_DIM]
    n_sel = seq // SEL_BLOCK
    blk = ((i * tm + lax.broadcasted_iota(jnp.int32, (tm, KV_DIM), 0)) // SEL_BLOCK) % n_sel
    onehot = jnp.where(lax.broadcasted_iota(jnp.int32, (tm, KV_DIM), 1) == blk, 1.0, 0.0).astype(BF16)
    kslc_ref[...] = jnp.concatenate([kv[:, 2 * KV_DIM:3 * KV_DIM], onehot], axis=1)
    conv = _dot(xb, wtok_ref[:, c0:c0 + 3 * CONV_DIM])
    cb = conv[:, 0:CONV_DIM]
    u = conv[:, CONV_DIM:2 * CONV_DIM] * conv[:, 2 * CONV_DIM:3 * CONV_DIM]
    xpb = xprev_ref[...].astype(BF16)
    up = (_dot(xpb, wtok_ref[:, c0 + CONV_DIM:c0 + 2 * CONV_DIM])
          * _dot(xpb, wtok_ref[:, c0 + 2 * CONV_DIM:c0 + 3 * CONV_DIM]))
    up = jnp.where(i % (seq // tm) == 0, 0.0, up)
    row = lax.broadcasted_iota(jnp.int32, (tm, CONV_DIM), 0)
    u1 = jnp.where(row == 0, up[7:8, :], pltpu.roll(u, 1, 0))
    u2 = jnp.where(row == 0, up[6:7, :], jnp.where(row == 1, up[7:8, :], pltpu.roll(u, 2, 0)))
    w = convw_ref[...]
    uc = w[0:1, :] * u2 + w[1:2, :] * u1 + w[2:3, :] * u
    cu_ref[...] = (cb * uc).astype(BF16)

    zt = _dot_nt(wt_ref[...], xb)
    qt_ref[...] = (zt[0:Q_DIM, :] * (ATTN_SCALE * LOG2_E)).astype(BF16)
    r0 = Q_DIM
    gt_ref[...] = _sigmoid(zt[r0:r0 + 2 * GATE_ROWS, :])
    r0 += 2 * GATE_ROWS
    vs = zt[r0:r0 + KV_DIM, :].astype(BF16)
    vw = zt[r0 + KV_DIM:r0 + 2 * KV_DIM, :].astype(BF16)
    for c in range(tm // ATT_TILE):
        vslct_ref[c] = vs[:, c * ATT_TILE:(c + 1) * ATT_TILE]
        vwint_ref[c] = vw[:, c * ATT_TILE:(c + 1) * ATT_TILE]


def _in_proj(x2, wtok, wt, conv_w, seq):
    T, D = x2.shape
    tm = 1024
    nt = T // tm
    n_tok = wtok.shape[1]
    n_t = wt.shape[0]
    row_blk = lambda n: pl.BlockSpec((tm, n), lambda i: (i, 0))
    out_shape = (
        jax.ShapeDtypeStruct((T, KV_DIM), BF16),
        jax.ShapeDtypeStruct((T, KV_DIM), BF16),
        jax.ShapeDtypeStruct((T, 2 * KV_DIM), BF16),
        jax.ShapeDtypeStruct((T, KV_DIM), BF16),
        jax.ShapeDtypeStruct((T, CONV_DIM), BF16),
        jax.ShapeDtypeStruct((Q_DIM, T), BF16),
        jax.ShapeDtypeStruct((2 * GATE_ROWS, T), F32),
        jax.ShapeDtypeStruct((T // ATT_TILE, KV_DIM, ATT_TILE), BF16),
        jax.ShapeDtypeStruct((T // ATT_TILE, KV_DIM, ATT_TILE), BF16),
    )
    vt_blk = pl.BlockSpec((tm // ATT_TILE, KV_DIM, ATT_TILE), lambda i: (i, 0, 0))
    return pl.pallas_call(
        functools.partial(_inproj_kernel, seq),
        grid=(nt,),
        in_specs=[
            row_blk(D),
            pl.BlockSpec((8, D), lambda i: (jnp.maximum(i * (tm // 8) - 1, 0), 0)),
            pl.BlockSpec((D, n_tok), lambda i: (0, 0)),
            pl.BlockSpec((n_t, D), lambda i: (0, 0)),
            pl.BlockSpec((3, CONV_DIM), lambda i: (0, 0)),
        ],
        out_specs=(
            row_blk(KV_DIM), row_blk(KV_DIM), row_blk(2 * KV_DIM), row_blk(KV_DIM), row_blk(CONV_DIM),
            pl.BlockSpec((Q_DIM, tm), lambda i: (0, i)),
            pl.BlockSpec((2 * GATE_ROWS, tm), lambda i: (0, i)),
            vt_blk, vt_blk,
        ),
        out_shape=out_shape,
        compiler_params=_cparams(1),
        name="in_proj",
    )(x2, x2, wtok, wt, conv_w)


def _compress_kernel(kin_ref, vin_ref, wa_ref, wb_ref, pe_ref, w1_ref, b1_ref, w2_ref, b2_ref,
                     kc_ref, vct_ref):
    def one(idx, in_ref):
        c = in_ref[0]
        a = _dot(c, wa_ref[idx])
        b = _dot(c, wb_ref[idx])
        n = b.shape[0]
        peb = _dot(pe_ref[idx], w1_ref[idx])[0:1, :]
        bias = jnp.concatenate([peb, peb], axis=1) + b1_ref[idx]
        h = a + pltpu.roll(b, n - 1, 0) + bias
        return _dot(jax.nn.gelu(h).astype(BF16), w2_ref[idx]) + b2_ref[idx]

    kc_ref[0] = one(0, kin_ref).astype(BF16)
    vct_ref[0] = one(1, vin_ref).T.astype(BF16)


def _compress(kcmp3, vcmp3, wa, wb, pe, w1, b1, w2, b2):
    B, nch, width = kcmp3.shape
    hid2 = wa.shape[2]
    full = lambda a: pl.BlockSpec(a.shape, lambda b: (0,) * a.ndim)
    in_blk = pl.BlockSpec((1, nch, width), lambda b: (b, 0, 0))
    out_blk = pl.BlockSpec((1, nch, KV_DIM), lambda b: (b, 0, 0))
    return pl.pallas_call(
        _compress_kernel,
        grid=(B,),
        in_specs=[in_blk, in_blk, full(wa), full(wb), full(pe), full(w1), full(b1), full(w2), full(b2)],
        out_specs=(out_blk, pl.BlockSpec((1, KV_DIM, nch), lambda b: (b, 0, 0))),
        out_shape=(jax.ShapeDtypeStruct((B, nch, KV_DIM), BF16),
                   jax.ShapeDtypeStruct((B, KV_DIM, nch), BF16)),
        compiler_params=_cparams(1),
        name="compress",
    )(kcmp3, vcmp3, wa, wb, pe, w1, b1, w2, b2)


def _group_q(qt_ref, hh, g_is0):
    qh = qt_ref[hh * HEAD_DIM:(hh + 1) * HEAD_DIM, :]
    z = jnp.zeros_like(qh)
    return jnp.concatenate([qh, z], axis=0) if g_is0 else jnp.concatenate([z, qh], axis=0)


def _cmpattn_kernel(n_cmp, qt_ref, kc_ref, vct_ref, gt_ref, ovt_ref, ocmpt_ref, selb_ref):
    i = pl.program_id(1)
    tq = qt_ref.shape[1]
    nc = kc_ref.shape[1]
    n_sel = ovt_ref.shape[0]
    kc = kc_ref[0]
    t_n = i * tq + lax.broadcasted_iota(jnp.int32, (nc, tq), 1)
    n_io = lax.broadcasted_iota(jnp.int32, (nc, tq), 0)
    blk_end = jnp.where(n_io < n_cmp, n_io * CMP_STRIDE + CMP_BLOCK - 1, jnp.int32(2 ** 30))
    vis = blk_end <= t_n
    j_io = lax.broadcasted_iota(jnp.int32, (n_sel, tq), 0)
    t_j = i * tq + lax.broadcasted_iota(jnp.int32, (n_sel, tq), 1)
    cur = t_j // SEL_BLOCK
    bonus = jnp.where(j_io == 0, FORCE_BONUS,
                      jnp.where(j_io == cur, FORCE_BONUS, jnp.where(j_io == cur - 1, FORCE_BONUS, 0.0)))
    valid = j_io * SEL_BLOCK <= t_j
    for g in range(N_GROUPS_KV):
        imp = jnp.zeros((n_sel, tq), F32)
        for hh in range(HEADS_PER_GROUP):
            h = g * HEADS_PER_GROUP + hh
            s = _dot(kc, _group_q(qt_ref, h, g == 0))
            s = jnp.where(vis, s, NEG_INF)
            m = jnp.max(s, axis=0, keepdims=True)
            p = jnp.where(vis, jnp.exp2(s - m), 0.0)
            l = jnp.sum(p, axis=0, keepdims=True)
            pr = (p * jnp.where(l > 0.0, 1.0 / l, 0.0)).astype(BF16)
            o = _dot(vct_ref[0, g * HEAD_DIM:(g + 1) * HEAD_DIM, :], pr)
            gate = gt_ref[g * GATE_ROWS + hh * N_BRANCH:g * GATE_ROWS + hh * N_BRANCH + 1, :]
            ocmpt_ref[h * HEAD_DIM:(h + 1) * HEAD_DIM, :] = (o * gate).astype(BF16)
            imp = imp + _dot(ovt_ref[...], pr)
        score = jnp.where(valid, imp + bonus, -FORCE_BONUS)
        cnt = jnp.zeros((n_sel, tq), F32)
        for jp in range(n_sel):
            r = score[jp:jp + 1, :]
            cnt = cnt + jnp.where(r > score, 1.0, jnp.where(r == score, jnp.where(j_io > jp, 1.0, 0.0), 0.0))
        selb_ref[0, g] = jnp.where(cnt < float(min(SEL_TOPK, n_sel)), 0.0, NEG_INF).astype(BF16)


def _cmp_attn(qt, kc, vct, gt, ovt, B, seq, n_cmp):
    T = qt.shape[1]
    tq = ATT_TILE
    nq = seq // tq
    n_sel = ovt.shape[0]
    nc = kc.shape[1]
    return pl.pallas_call(
        functools.partial(_cmpattn_kernel, n_cmp),
        grid=(B, nq),
        in_specs=[
            pl.BlockSpec((Q_DIM, tq), lambda b, i: (0, b * nq + i)),
            pl.BlockSpec((1, nc, KV_DIM), lambda b, i: (b, 0, 0)),
            pl.BlockSpec((1, KV_DIM, nc), lambda b, i: (b, 0, 0)),
            pl.BlockSpec((2 * GATE_ROWS, tq), lambda b, i: (0, b * nq + i)),
            pl.BlockSpec(ovt.shape, lambda b, i: (0, 0)),
        ],
        out_specs=(
            pl.BlockSpec((Q_DIM, tq), lambda b, i: (0, b * nq + i)),
            pl.BlockSpec((1, N_GROUPS_KV, n_sel, tq), lambda b, i: (b * nq + i, 0, 0, 0)),
        ),
        out_shape=(jax.ShapeDtypeStruct((Q_DIM, T), BF16),
                   jax.ShapeDtypeStruct((B * nq, N_GROUPS_KV, n_sel, tq), BF16)),
        compiler_params=_cparams(2),
        name="cmp_attn",
    )(qt, kc, vct, gt, ovt)


def _nsa_kernel(qt_ref, kslc_ref, kwin_ref, vslct_ref, vwint_ref, selb_ref, gt_ref, ocmpt_ref, kmq_ref,
                ot_ref, q_s, s_s, m_s, acc_s):
    i = pl.program_id(1)
    tq = qt_ref.shape[1]
    tile = ATT_TILE
    pair = 2 * tile
    n_sel = selb_ref.shape[2]
    groups = range(N_GROUPS_KV)
    gq = HEADS_PER_GROUP * HEAD_DIM

    for g in groups:
        for hh in range(HEADS_PER_GROUP):
            lanes = slice(hh * tq, (hh + 1) * tq)
            qh = qt_ref[g * gq + hh * HEAD_DIM:g * gq + (hh + 1) * HEAD_DIM, :]
            zero = jnp.zeros_like(qh)
            q_s[g, 0:KV_DIM, lanes] = jnp.concatenate([qh, zero] if g == 0 else [zero, qh], axis=0)
            q_s[g, KV_DIM:KV_DIM + n_sel, lanes] = selb_ref[0, g]
            q_s[g, KV_DIM + n_sel:2 * KV_DIM, lanes] = jnp.zeros((KV_DIM - n_sel, tq), BF16)

    def v_rows(ref, g, tiles):
        vt = jnp.concatenate([ref[j, g * HEAD_DIM:(g + 1) * HEAD_DIM, :] for j in tiles], axis=1)
        return jnp.concatenate([vt, jnp.ones((SUM_ROWS, vt.shape[1]), BF16)], axis=0)

    def sel_scores(g, jj):
        keys = kslc_ref[pl.ds(pl.multiple_of(jj * pair, pair), pair), :]
        return _dot(keys, q_s[g])

    def sel_update(g, s, jj):
        m_old = m_s[g]
        m_new = jnp.maximum(m_old, jnp.max(s, axis=0, keepdims=True))
        p = jnp.exp2(s - m_new).astype(BF16)
        acc_s[g] = jnp.exp2(m_old - m_new) * acc_s[g] + _dot(v_rows(vslct_ref, g, (2 * jj, 2 * jj + 1)), p)
        m_s[g] = m_new

    m_s[...] = jnp.full(m_s.shape, NEG_INF, F32)
    acc_s[...] = jnp.zeros(acc_s.shape, F32)
    n_full = i // 2
    for g in groups:
        s_s[g] = sel_scores(g, 0)

    def sel_body(jj, carry):
        for g in groups:
            s = s_s[g]
            s_s[g] = sel_scores(g, jj + 1)
            sel_update(g, s, jj)
        return carry

    lax.fori_loop(0, n_full, sel_body, 0)

    j0 = jnp.maximum(i - WINDOW // tile, 0)
    n_win = WINDOW + tile
    behind = (i - j0) * tile - kmq_ref[...]
    in_window = pltpu.bitcast(behind, jnp.uint32) < WINDOW
    causal = kmq_ref[0:pair, :] <= (i - 2 * n_full) * tile
    for g in groups:
        sel_update(g, jnp.where(causal, s_s[g], NEG_INF), n_full)
        o_slc = acc_s[g, 0:HEAD_DIM, :] * (1.0 / acc_s[g, HEAD_DIM:HEAD_DIM + 1, :])
        s = _dot(kwin_ref[pl.ds(pl.multiple_of(j0 * tile, tile), n_win), :], q_s[g, 0:KV_DIM, :])
        s = jnp.where(in_window, s, NEG_INF)
        p = jnp.exp2(s - jnp.max(s, axis=0, keepdims=True)).astype(BF16)
        ow = _dot(v_rows(vwint_ref, g, (j0, j0 + 1, j0 + 2)), p)
        o_win = ow[0:HEAD_DIM, :] * (1.0 / ow[HEAD_DIM:HEAD_DIM + 1, :])
        for hh in range(HEADS_PER_GROUP):
            lanes = slice(hh * tq, (hh + 1) * tq)
            rows = slice(g * gq + hh * HEAD_DIM, g * gq + (hh + 1) * HEAD_DIM)
            gate = g * GATE_ROWS + hh * N_BRANCH
            o = (ocmpt_ref[rows, :].astype(F32) + gt_ref[gate + 1:gate + 2, :] * o_slc[:, lanes]
                 + gt_ref[gate + 2:gate + 3, :] * o_win[:, lanes])
            ot_ref[rows, :] = o.astype(BF16)


def _nsa_attn(qt, kslc, kwin, vslct, vwint, selb, gt, ocmpt, B, seq):
    T = qt.shape[1]
    tq = ATT_TILE
    nq = seq // tq
    ktiles = seq // ATT_TILE
    n_sel = selb.shape[2]
    wide = HEADS_PER_GROUP * tq
    n_win = WINDOW + ATT_TILE
    kmq = jnp.asarray(np.arange(n_win)[:, None] - (np.arange(wide) % tq)[None, :], jnp.int32)
    qblk = pl.BlockSpec((Q_DIM, tq), lambda b, i: (0, b * nq + i))
    vblk = pl.BlockSpec((ktiles, KV_DIM, ATT_TILE), lambda b, i: (b, 0, 0))
    return pl.pallas_call(
        _nsa_kernel,
        grid=(B, nq),
        in_specs=[
            qblk,
            pl.BlockSpec((seq, 2 * KV_DIM), lambda b, i: (b, 0)),
            pl.BlockSpec((seq, KV_DIM), lambda b, i: (b, 0)),
            vblk, vblk,
            pl.BlockSpec((1, N_GROUPS_KV, n_sel, tq), lambda b, i: (b * nq + i, 0, 0, 0)),
            pl.BlockSpec((N_GROUPS_KV * GATE_ROWS, tq), lambda b, i: (0, b * nq + i)),
            qblk,
            pl.BlockSpec((n_win, wide), lambda b, i: (0, 0)),
        ],
        out_specs=qblk,
        out_shape=jax.ShapeDtypeStruct((Q_DIM, T), BF16),
        scratch_shapes=[
            pltpu.VMEM((N_GROUPS_KV, 2 * KV_DIM, wide), BF16),
            pltpu.VMEM((N_GROUPS_KV, 2 * ATT_TILE, wide), F32),
            pltpu.VMEM((N_GROUPS_KV, 1, wide), F32),
            pltpu.VMEM((N_GROUPS_KV, HEAD_DIM + SUM_ROWS, wide), F32),
        ],
        compiler_params=_cparams(2),
        name="nsa_attn",
    )(qt, kslc, kwin, vslct, vwint, selb, gt, ocmpt, kmq)


def _route(logits):
    lane = lax.broadcasted_iota(jnp.int32, logits.shape, 1)
    big = 2 ** 20
    grp = jnp.where((lane >= N_EXPERTS) & (lane < N_EXPERTS + N_EXPERT_GROUPS), logits, NEG_INF)
    gmax = jnp.max(grp, axis=1, keepdims=True)
    g_idx = jnp.min(jnp.where(grp == gmax, lane, big), axis=1, keepdims=True) - N_EXPERTS
    p_group = 1.0 / jnp.sum(jnp.exp(grp - gmax), axis=1, keepdims=True)
    own = jnp.where((lane < N_EXPERTS) & (lane // EXPERTS_PER_GROUP == g_idx), logits, NEG_INF)
    l1 = jnp.max(own, axis=1, keepdims=True)
    e1 = jnp.min(jnp.where(own == l1, lane, big), axis=1, keepdims=True)
    rest = jnp.where(lane == e1, NEG_INF, own)
    l2 = jnp.max(rest, axis=1, keepdims=True)
    e2 = jnp.min(jnp.where(rest == l2, lane, big), axis=1, keepdims=True)
    r = jnp.exp(l2 - l1)
    c1 = p_group / (1.0 + r)
    return lane, e1, e2, c1, c1 * r


R_SLOT1, R_SLOT2, R_C1, R_C2 = range(4)


def _lane_pick(lane, idx, row):
    return jnp.sum(jnp.where(lane == idx, row, 0.0), axis=1, keepdims=True)


def _merge_kernel(alpha, ot_ref, cu_ref, x_ref, wn_ref, wc_ref, wgm_ref, wo_ref, g1_ref, b1_ref,
                  wr_ref, br_ref, h1_ref, route_ref, cnt_ref):
    tm, d = x_ref.shape
    iota = lambda shape, dim: lax.broadcasted_iota(jnp.int32, shape, dim)
    sq = (MOE_TILE, MOE_TILE)
    earlier_token = jnp.where(iota(sq, 0) > iota(sq, 1), 1.0, 0.0).astype(BF16)
    lower_expert = jnp.where(iota((LANES, LANES), 0) < iota((LANES, LANES), 1), 1.0, 0.0).astype(BF16)
    x = x_ref[...]
    xb = x.astype(BF16)
    y_nsa = _dot_tn(ot_ref[...], wn_ref[...])
    y_conv = _dot(cu_ref[...], wc_ref[...])
    merged = (_sigmoid(_dot(xb, wgm_ref[:, 0:d])) * y_nsa
              + _sigmoid(_dot(xb, wgm_ref[:, d:2 * d])) * y_conv)
    mix = _dot(merged.astype(BF16), wo_ref[...])
    h1 = _layer_norm(alpha * x + mix, g1_ref[...], b1_ref[...])
    h1_ref[...] = h1
    logits = _dot(h1.astype(BF16), wr_ref[...]) + br_ref[...]
    for sub in range(tm // MOE_TILE):
        rows = slice(sub * MOE_TILE, (sub + 1) * MOE_TILE)
        lane, e1, e2, c1, c2 = _route(logits[rows, :])
        onehot = jnp.where(lane == e1, 1.0, jnp.where(lane == e2, 1.0, 0.0))
        before = _dot(earlier_token, onehot.astype(BF16))
        cnt = jnp.sum(onehot, axis=0, keepdims=True)
        chunks = jnp.floor((cnt + (CHUNK - 1.0)) * (1.0 / CHUNK))
        run_start = _dot(jnp.broadcast_to(chunks, (8, LANES)).astype(BF16), lower_expert)[0:1, :] * CHUNK
        pos = run_start + before
        rec = jnp.zeros((MOE_TILE, LANES), F32)
        for k, v in ((R_SLOT1, _lane_pick(lane, e1, pos)), (R_SLOT2, _lane_pick(lane, e2, pos)),
                     (R_C1, c1), (R_C2, c2)):
            rec = jnp.where(lane == k, v, rec)
        route_ref[rows, :] = rec
        cnt_ref[sub] = jnp.broadcast_to(cnt, (8, LANES))


def _merge(ot, cu, x2, wn, wc, wgm, wo, g1, b1, wr, br, alpha):
    T, D = x2.shape
    tm = 512
    sub = tm // MOE_TILE
    full = lambda a: pl.BlockSpec(a.shape, lambda i: (0,) * a.ndim)
    return pl.pallas_call(
        functools.partial(_merge_kernel, alpha),
        grid=(T // tm,),
        in_specs=[
            pl.BlockSpec((Q_DIM, tm), lambda i: (0, i)),
            pl.BlockSpec((tm, CONV_DIM), lambda i: (i, 0)),
            pl.BlockSpec((tm, D), lambda i: (i, 0)),
            full(wn), full(wc), full(wgm), full(wo), full(g1), full(b1), full(wr), full(br),
        ],
        out_specs=(pl.BlockSpec((tm, D), lambda i: (i, 0)), pl.BlockSpec((tm, LANES), lambda i: (i, 0)),
                   pl.BlockSpec((sub, 8, LANES), lambda i: (i, 0, 0))),
        out_shape=(jax.ShapeDtypeStruct((T, D), F32), jax.ShapeDtypeStruct((T, LANES), F32),
                   jax.ShapeDtypeStruct((T // MOE_TILE, 8, LANES), F32)),
        compiler_params=_cparams(1),
        name="merge",
    )(ot, cu, x2, wn, wc, wgm, wo, g1, b1, wr, br)


def _route_field(route_ref, k):
    rec = route_ref[...]
    return _lane_pick(lax.broadcasted_iota(jnp.int32, rec.shape, 1), k, rec)


def _chunk_rows(q):
    return pl.ds(pl.multiple_of(q * CHUNK, CHUNK), CHUNK)


def _dispatch_kernel(tab_ref, tab_m1_ref, tab_m2_ref, tails_ref, nused_ref, h1_ref, route_ref, xs_ref,
                     buf, zero_s, sem):
    i = pl.program_id(0)
    last = pl.num_programs(0) - 1
    tm = h1_ref.shape[0]
    slot = i % 2

    def drain(tab, slot_):
        def wait(q, c):
            pltpu.make_async_copy(buf.at[slot_, _chunk_rows(0)], xs_ref.at[pl.ds(0, CHUNK)], sem.at[slot_]).wait()
            return c
        lax.fori_loop(0, tab[0, 0, TAB_COUNT], wait, 0)

    s1 = _route_field(route_ref, R_SLOT1).astype(jnp.int32)
    s2 = _route_field(route_ref, R_SLOT2).astype(jnp.int32)
    s_io = lax.broadcasted_iota(jnp.int32, (tm, LOCAL_ROWS), 1)
    perm = jnp.where(s_io == s1, 1.0, jnp.where(s_io == s2, 1.0, 0.0)).astype(BF16)
    srt = _dot_tn(perm, h1_ref[...].astype(BF16)).astype(BF16)

    @pl.when(i >= 2)
    def _():
        drain(tab_m2_ref, slot)

    buf[slot] = srt

    def start(q, c):
        dst = pl.multiple_of(tab_ref[0, 0, q], CHUNK)
        pltpu.make_async_copy(buf.at[slot, _chunk_rows(q)], xs_ref.at[pl.ds(dst, CHUNK)], sem.at[slot]).start()
        return c

    lax.fori_loop(0, tab_ref[0, 0, TAB_COUNT], start, 0)

    @pl.when(i == last)
    def _():
        drain(tab_ref, slot)

    @pl.when(jnp.logical_and(i == last, i >= 1))
    def _():
        drain(tab_m1_ref, 1 - slot)

    @pl.when(i == last)
    def _():
        zero_s[...] = jnp.zeros(zero_s.shape, zero_s.dtype)

        def tail_copy(q):
            dst = pl.multiple_of(jnp.maximum(tails_ref[q], 0), CHUNK)
            return pltpu.make_async_copy(zero_s.at[pl.ds(0, CHUNK)], xs_ref.at[pl.ds(dst, CHUNK)], sem.at[2])

        def tile_copy(j):
            dst = pl.multiple_of(j * EXPERT_TILE, EXPERT_TILE)
            return pltpu.make_async_copy(zero_s, xs_ref.at[pl.ds(dst, EXPERT_TILE)], sem.at[2])

        def start_tail(q, c):
            @pl.when(tails_ref[q] >= 0)
            def _():
                tail_copy(q).start()
            return c

        def wait_tail(q, c):
            @pl.when(tails_ref[q] >= 0)
            def _():
                tail_copy(q).wait()
            return c

        def start_tile(j, c):
            tile_copy(j).start()
            return c

        def wait_tile(j, c):
            tile_copy(j).wait()
            return c

        n_tiles = xs_ref.shape[0] // EXPERT_TILE
        lax.fori_loop(0, tails_ref.shape[0], start_tail, 0)
        lax.fori_loop(nused_ref[0], n_tiles, start_tile, 0)
        lax.fori_loop(0, tails_ref.shape[0], wait_tail, 0)
        lax.fori_loop(nused_ref[0], n_tiles, wait_tile, 0)


def _dispatch(tab, tails, n_used, h1, route, n_rows):
    T, D = h1.shape
    tm = MOE_TILE
    tab_blk = lambda back: pl.BlockSpec((1, 1, LANES), lambda i: (jnp.maximum(i - back, 0), 0, 0),
                                        memory_space=pltpu.SMEM)
    return pl.pallas_call(
        _dispatch_kernel,
        grid=(T // tm,),
        in_specs=[
            tab_blk(0), tab_blk(1), tab_blk(2),
            pl.BlockSpec(memory_space=pltpu.SMEM),
            pl.BlockSpec(memory_space=pltpu.SMEM),
            pl.BlockSpec((tm, D), lambda i: (i, 0)),
            pl.BlockSpec((tm, LANES), lambda i: (i, 0)),
        ],
        out_specs=pl.BlockSpec(memory_space=pl.ANY),
        out_shape=jax.ShapeDtypeStruct((n_rows, D), BF16),
        scratch_shapes=[pltpu.VMEM((2, LOCAL_ROWS, D), BF16), pltpu.VMEM((EXPERT_TILE, D), BF16),
                        pltpu.SemaphoreType.DMA((3,))],
        compiler_params=_cparams(1),
        name="moe_dispatch",
    )(tab, tab, tab, tails, n_used, h1, route)


def _experts_kernel(te_ref, nu_ref, xs_ref, wg_ref, wu_ref, wd_ref, ys_ref):
    del te_ref
    j = pl.program_id(0)

    @pl.when(j < nu_ref[0])
    def _():
        xb = xs_ref[...]
        hg = _dot(xb, wg_ref[0])
        h = hg * _sigmoid(hg) * _dot(xb, wu_ref[0])
        y = _dot(h.astype(BF16), wd_ref[0])
        ys_ref[...] = y.astype(BF16)

    @pl.when(j >= nu_ref[0])
    def _():
        ys_ref[...] = jnp.zeros(ys_ref.shape, ys_ref.dtype)


def _experts(tile_expert, n_used, xs, wg, wu, wd):
    n_rows, D = xs.shape
    hid = wg.shape[2]
    tm = EXPERT_TILE
    used = lambda j, te, nu: (jnp.minimum(j, nu[0] - 1), 0)
    grid_spec = pltpu.PrefetchScalarGridSpec(
        num_scalar_prefetch=2,
        grid=(n_rows // tm,),
        in_specs=[
            pl.BlockSpec((tm, D), used),
            pl.BlockSpec((1, D, hid), lambda j, te, nu: (te[j], 0, 0)),
            pl.BlockSpec((1, D, hid), lambda j, te, nu: (te[j], 0, 0)),
            pl.BlockSpec((1, hid, D), lambda j, te, nu: (te[j], 0, 0)),
        ],
        out_specs=pl.BlockSpec((tm, D), lambda j, te, nu: (j, 0)),
    )
    return pl.pallas_call(
        _experts_kernel,
        grid_spec=grid_spec,
        out_shape=jax.ShapeDtypeStruct((n_rows, D), BF16),
        compiler_params=_cparams(1),
        name="moe_experts",
    )(tile_expert, n_used, xs, wg, wu, wd)


def _combine_kernel(alpha, tab_ref, tab_next_ref, route_ref, h1_ref, p_ref, pproj_ref, pgw_ref,
                    pgb_ref, g2_ref, b2_ref, ys_ref, out_ref, buf, sem):
    i = pl.program_id(0)
    tm = h1_ref.shape[0]
    slot = i % 2

    def gather(tab, slot_):
        def start(q, c):
            src = pl.multiple_of(tab[0, 0, q], CHUNK)
            pltpu.make_async_copy(ys_ref.at[pl.ds(src, CHUNK)], buf.at[slot_, _chunk_rows(q)],
                                  sem.at[slot_]).start()
            return c
        lax.fori_loop(0, tab[0, 0, TAB_COUNT], start, 0)

    @pl.when(i == 0)
    def _():
        buf[...] = jnp.zeros(buf.shape, buf.dtype)
        gather(tab_ref, 0)

    @pl.when(i + 1 < pl.num_programs(0))
    def _():
        gather(tab_next_ref, 1 - slot)

    h1 = h1_ref[...]
    ple = (_sigmoid(_dot(h1.astype(BF16), pgw_ref[...]) + pgb_ref[...])
           * _dot(p_ref[...].astype(BF16), pproj_ref[...]))

    def wait(q, c):
        pltpu.make_async_copy(ys_ref.at[pl.ds(0, CHUNK)], buf.at[slot, _chunk_rows(0)], sem.at[slot]).wait()
        return c

    lax.fori_loop(0, tab_ref[0, 0, TAB_COUNT], wait, 0)

    s1 = _route_field(route_ref, R_SLOT1).astype(jnp.int32)
    s2 = _route_field(route_ref, R_SLOT2).astype(jnp.int32)
    c1 = _route_field(route_ref, R_C1)
    c2 = _route_field(route_ref, R_C2)
    s_io = lax.broadcasted_iota(jnp.int32, (tm, LOCAL_ROWS), 1)
    weights = jnp.where(s_io == s1, c1, jnp.where(s_io == s2, c2, 0.0)).astype(BF16)
    ffn = _dot(weights, buf[slot])
    out_ref[...] = _layer_norm(alpha * h1 + ffn + ple, g2_ref[...], b2_ref[...])


def _combine(tab, route, h1, p2, pproj, pgw, pgb, g2, b2, ys, alpha):
    T, D = h1.shape
    tm = MOE_TILE
    nt = T // tm
    full = lambda a: pl.BlockSpec(a.shape, lambda i: (0,) * a.ndim)
    return pl.pallas_call(
        functools.partial(_combine_kernel, alpha),
        grid=(nt,),
        in_specs=[
            pl.BlockSpec((1, 1, LANES), lambda i: (i, 0, 0), memory_space=pltpu.SMEM),
            pl.BlockSpec((1, 1, LANES), lambda i: (jnp.minimum(i + 1, nt - 1), 0, 0), memory_space=pltpu.SMEM),
            pl.BlockSpec((tm, LANES), lambda i: (i, 0)),
            pl.BlockSpec((tm, D), lambda i: (i, 0)),
            pl.BlockSpec((tm, p2.shape[1]), lambda i: (i, 0)),
            full(pproj), full(pgw), full(pgb), full(g2), full(b2),
            pl.BlockSpec(memory_space=pl.ANY),
        ],
        out_specs=pl.BlockSpec((tm, D), lambda i: (i, 0)),
        out_shape=jax.ShapeDtypeStruct((T, D), F32),
        scratch_shapes=[pltpu.VMEM((2, LOCAL_ROWS, D), BF16), pltpu.SemaphoreType.DMA((2,))],
        compiler_params=_cparams(1),
        name="moe_combine",
    )(tab, tab, route, h1, p2, pproj, pgw, pgb, g2, b2, ys)


def _moe_plan(cnt_tiles, n_expert_tiles):
    cnt = cnt_tiles[:, 0, :N_EXPERTS].astype(jnp.int32)
    cnt8 = (cnt + CHUNK - 1) // CHUNK * CHUNK
    lend = jnp.cumsum(cnt8, axis=1)
    lstart = lend - cnt8
    gend = jnp.cumsum(cnt8, axis=0)
    region = (gend[-1] + EXPERT_TILE - 1) // EXPERT_TILE * EXPERT_TILE
    oend = jnp.cumsum(region)
    gstart = (oend - region)[None, :] + gend - cnt8
    q8 = jnp.arange(MAX_CHUNKS, dtype=jnp.int32) * CHUNK
    eq = jnp.minimum(jnp.sum((lend[:, None, :] <= q8[None, :, None]).astype(jnp.int32), axis=-1), N_EXPERTS - 1)
    shift = jnp.sum(jnp.where(eq[:, :, None] == jnp.arange(N_EXPERTS)[None, None, :],
                              (gstart - lstart)[:, None, :], 0), axis=-1)
    dstq = shift + q8[None, :]
    nt = cnt.shape[0]
    tab = jnp.zeros((nt, LANES), jnp.int32).at[:, :MAX_CHUNKS].set(dstq).at[:, TAB_COUNT].set(lend[:, -1] // CHUNK)
    tile_row = jnp.arange(n_expert_tiles, dtype=jnp.int32) * EXPERT_TILE
    tile_expert = jnp.minimum(jnp.sum((oend[None, :] <= tile_row[:, None]).astype(jnp.int32), axis=1),
                              N_EXPERTS - 1)
    n_used = (oend[-1] // EXPERT_TILE).reshape(1)
    c = jnp.arange(EXPERT_TILE // CHUNK, dtype=jnp.int32)[None, :] * CHUNK
    tail_start = (oend - region + gend[-1])[:, None] + c
    tails = jnp.where(tail_start < oend[:, None], tail_start, -1).reshape(-1)
    return tab.reshape(nt, 1, LANES), tails, tile_expert, n_used


def _split_w_in(w_in, D):
    sizes = [Q_DIM] + [KV_DIM] * 6 + [N_HEADS * N_BRANCH] + [CONV_DIM] * 3 + [D] * 2
    offs = np.concatenate([[0], np.cumsum(sizes)])
    names = ["q", "k_cmp", "v_cmp", "k_slc", "v_slc", "k_win", "v_win", "g_nsa",
             "conv_b", "conv_c", "conv_h", "g_m_nsa", "g_m_conv"]
    return {n: w_in[:, int(offs[k]):int(offs[k + 1])] for k, n in enumerate(names)}


def _layer(x2, p2, B, seq, depth, w_in, cmp_pe, cmp_w1, cmp_b1, cmp_w2, cmp_b2, conv_w, w_nsa_out,
           w_conv_out, w_o, ln1_g, ln1_b, rg_w, rg_b, re_w, re_b, e_wg, e_wu, e_wd, ple_proj,
           ple_gate_w, ple_gate_b, ln2_g, ln2_b):
    T, D = x2.shape
    alpha = (2.0 * depth) ** 0.25
    w = _split_w_in(w_in, D)
    wtok = jnp.concatenate([w["k_cmp"], w["v_cmp"], w["k_slc"], w["k_win"],
                            w["conv_b"], w["conv_c"], w["conv_h"]], axis=1).astype(BF16)
    gcols = w["g_nsa"].reshape(D, N_GROUPS_KV, HEADS_PER_GROUP * N_BRANCH)
    gcols = jnp.pad(gcols, ((0, 0), (0, 0), (0, GATE_ROWS - HEADS_PER_GROUP * N_BRANCH)))
    wt = jnp.concatenate([w["q"], gcols.reshape(D, N_GROUPS_KV * GATE_ROWS), w["v_slc"], w["v_win"]],
                         axis=1).T.astype(BF16)
    kcmp, vcmp, kslc, kwin, cu, qt, gt, vslct, vwint = _in_proj(x2, wtok, wt, conv_w, seq)

    half = CMP_BLOCK // 2
    n_chunks = seq // CMP_STRIDE
    n_cmp = (seq - CMP_BLOCK) // CMP_STRIDE + 1
    hidden = cmp_w1.shape[-1]
    eye = jnp.eye(N_GROUPS_KV, dtype=F32)
    w1r = cmp_w1.reshape(2, CMP_BLOCK, HEAD_DIM, hidden)
    expand = lambda m: jnp.einsum("ildh,gk->ilgdkh", m, eye).reshape(
        2, half * KV_DIM, N_GROUPS_KV * hidden).astype(BF16)
    wa, wb = expand(w1r[:, :half]), expand(w1r[:, half:])
    pe = jnp.broadcast_to(cmp_pe.reshape(2, 1, CMP_BLOCK * HEAD_DIM), (2, 8, CMP_BLOCK * HEAD_DIM)).astype(BF16)
    b1t = jnp.tile(cmp_b1.reshape(2, 1, hidden), (1, 1, N_GROUPS_KV))
    w2b = jnp.einsum("ihd,gk->ighkd", cmp_w2, eye).reshape(2, N_GROUPS_KV * hidden, KV_DIM).astype(BF16)
    b2t = jnp.tile(cmp_b2.reshape(2, 1, HEAD_DIM), (1, 1, N_GROUPS_KV))
    kc, vct = _compress(kcmp.reshape(B, n_chunks, CMP_STRIDE * KV_DIM),
                        vcmp.reshape(B, n_chunks, CMP_STRIDE * KV_DIM),
                        wa, wb, pe, cmp_w1.astype(BF16), b1t, w2b, b2t)

    n_sel = seq // SEL_BLOCK
    c_start = np.arange(n_chunks) * CMP_STRIDE
    s_start = np.arange(n_sel) * SEL_BLOCK
    overlap = ((c_start[None, :] <= s_start[:, None] + SEL_BLOCK - 1)
               & (c_start[None, :] + CMP_BLOCK - 1 >= s_start[:, None])).astype(np.float32)
    ocmpt, selb = _cmp_attn(qt, kc, vct, gt, jnp.asarray(overlap, BF16), B, seq, n_cmp)
    ot = _nsa_attn(qt, kslc, kwin, vslct, vwint, selb, gt, ocmpt, B, seq)

    wgm = jnp.concatenate([w["g_m_nsa"], w["g_m_conv"]], axis=1).astype(BF16)
    wr = jnp.pad(jnp.concatenate([re_w, rg_w], axis=1), ((0, 0), (0, LANES - N_EXPERTS - N_EXPERT_GROUPS)))
    br = jnp.pad(jnp.concatenate([re_b, rg_b]), (0, LANES - N_EXPERTS - N_EXPERT_GROUPS)).reshape(1, LANES)
    h1, route, cnt_tiles = _merge(ot, cu, x2, w_nsa_out.astype(BF16), w_conv_out.astype(BF16), wgm,
                                  w_o.astype(BF16), ln1_g.reshape(1, D), ln1_b.reshape(1, D),
                                  wr.astype(BF16), br, alpha)
    n_tok_tiles = T // MOE_TILE
    max_rows = 2 * T + n_tok_tiles * N_EXPERTS * (CHUNK - 1) + N_EXPERTS * (EXPERT_TILE - 1)
    n_expert_tiles = -(-max_rows // EXPERT_TILE)
    tab, tails, tile_expert, n_used = _moe_plan(cnt_tiles, n_expert_tiles)
    xs = _dispatch(tab, tails, n_used, h1, route, n_expert_tiles * EXPERT_TILE)
    ys = _experts(tile_expert, n_used, xs, e_wg.astype(BF16), e_wu.astype(BF16), e_wd.astype(BF16))
    return _combine(tab, route, h1, p2, ple_proj.astype(BF16), ple_gate_w.astype(BF16),
                    ple_gate_b.reshape(1, D), ln2_g.reshape(1, D), ln2_b.reshape(1, D), ys, alpha)


def kernel(x, p, w_in, cmp_pe, cmp_w1, cmp_b1, cmp_w2, cmp_b2, conv_w, w_nsa_out, w_conv_out, w_o, ln1_g, ln1_b, router_group_w, router_group_b, router_expert_w, router_expert_b, expert_w_gate, expert_w_up, expert_w_down, ple_proj, ple_gate_w, ple_gate_b, ln2_g, ln2_b):
    B, seq, D = x.shape
    depth = w_in.shape[0]
    x2 = x.reshape(B * seq, D)
    for i in range(depth):
        x2 = _layer(x2, p[i].reshape(B * seq, -1), B, seq, depth, w_in[i], cmp_pe[i], cmp_w1[i], cmp_b1[i],
                    cmp_w2[i], cmp_b2[i], conv_w[i], w_nsa_out[i], w_conv_out[i], w_o[i], ln1_g[i], ln1_b[i],
                    router_group_w[i], router_group_b[i], router_expert_w[i], router_expert_b[i],
                    expert_w_gate[i], expert_w_up[i], expert_w_down[i], ple_proj[i], ple_gate_w[i],
                    ple_gate_b[i], ln2_g[i], ln2_b[i])
    return x2.reshape(B, seq, D)
```

```python
import functools

import jax
import jax.numpy as jnp
import numpy as np
from jax import lax
from jax.experimental import pallas as pl
from jax.experimental.pallas import tpu as pltpu

F32 = jnp.float32
BF16 = jnp.bfloat16

N_HEADS = 8
N_GROUPS_KV = 2
HEADS_PER_GROUP = N_HEADS // N_GROUPS_KV
HEAD_DIM = 64
Q_DIM = N_HEADS * HEAD_DIM
KV_DIM = N_GROUPS_KV * HEAD_DIM
N_BRANCH = 3
CMP_BLOCK = 32
CMP_STRIDE = 16
SEL_BLOCK = 64
SEL_TOPK = 16
WINDOW = 512
CONV_DIM = 512
N_EXPERT_GROUPS = 4
EXPERTS_PER_GROUP = 8
N_EXPERTS = N_EXPERT_GROUPS * EXPERTS_PER_GROUP
ATTN_SCALE = HEAD_DIM ** -0.5
LOG2_E = 1.4426950408889634
FORCE_BONUS = 1e4
NEG_INF = -1e30
LN_EPS = 1e-5

LANES = 128
ATT_TILE = 256
GATE_ROWS = 16
SUM_ROWS = 16
VMEM_LIMIT = 48 * 1024 * 1024

MOE_TILE = 256
EXPERT_TILE = 1024
CHUNK = 16
LOCAL_ROWS = -(-(2 * MOE_TILE + N_EXPERTS * (CHUNK - 1)) // ATT_TILE) * ATT_TILE
MAX_CHUNKS = LOCAL_ROWS // CHUNK
TAB_COUNT = LANES - 1


def _cparams(n_axes):
    return pltpu.CompilerParams(dimension_semantics=("arbitrary",) * n_axes,
                                vmem_limit_bytes=VMEM_LIMIT)


def _dot(a, b):
    return jnp.dot(a, b, preferred_element_type=F32)


def _dot_nt(a, b):
    return lax.dot_general(a, b, (((1,), (1,)), ((), ())), preferred_element_type=F32)


def _dot_tn(a, b):
    return lax.dot_general(a, b, (((0,), (0,)), ((), ())), preferred_element_type=F32)


def _sigmoid(v):
    return 1.0 / (1.0 + jnp.exp(-v))


def _layer_norm(v, g, b):
    mu = jnp.mean(v, axis=-1, keepdims=True)
    d = v - mu
    var = jnp.mean(d * d, axis=-1, keepdims=True)
    return d * lax.rsqrt(var + LN_EPS) * g + b


def _inproj_kernel(seq, x_ref, xprev_ref, wtok_ref, wt_ref, convw_ref,
                   kcmp_ref, vcmp_ref, kslc_ref, kwin_ref, cu_ref, qt_ref, gt_ref, vslct_ref, vwint_ref, cmp_s):
    i = pl.program_id(0)
    tm = x_ref.shape[0]
    xb = x_ref[...].astype(BF16)

    c0 = 4 * KV_DIM
    kvf = _dot(xb, wtok_ref[:, 0:c0])
    kv = kvf.astype(BF16)
    cmp_s[0] = kvf[:, 0:KV_DIM]
    cmp_s[1] = kvf[:, KV_DIM:2 * KV_DIM]
    for l in range(CMP_STRIDE):
        rows = pl.ds(l, tm // CMP_STRIDE, stride=CMP_STRIDE)
        kcmp_ref[:, l * KV_DIM:(l + 1) * KV_DIM] = cmp_s[0, rows, :].astype(BF16)
        vcmp_ref[:, l * KV_DIM:(l + 1) * KV_DIM] = cmp_s[1, rows, :].astype(BF16)
    kwin_ref[...] = kv[:, 3 * KV_DIM:4 * KV_DIM]
    n_sel = seq // SEL_BLOCK
    blk = ((i * tm + lax.broadcasted_iota(jnp.int32, (tm, KV_DIM), 0)) // SEL_BLOCK) % n_sel
    onehot = jnp.where(lax.broadcasted_iota(jnp.int32, (tm, KV_DIM), 1) == blk, 1.0, 0.0).astype(BF16)
    kslc_ref[...] = jnp.concatenate([kv[:, 2 * KV_DIM:3 * KV_DIM], onehot], axis=1)
    conv = _dot(xb, wtok_ref[:, c0:c0 + 3 * CONV_DIM])
    cb = conv[:, 0:CONV_DIM]
    u = conv[:, CONV_DIM:2 * CONV_DIM] * conv[:, 2 * CONV_DIM:3 * CONV_DIM]
    xpb = xprev_ref[...].astype(BF16)
    up = (_dot(xpb, wtok_ref[:, c0 + CONV_DIM:c0 + 2 * CONV_DIM])
          * _dot(xpb, wtok_ref[:, c0 + 2 * CONV_DIM:c0 + 3 * CONV_DIM]))
    up = jnp.where(i % (seq // tm) == 0, 0.0, up)
    row = lax.broadcasted_iota(jnp.int32, (tm, CONV_DIM), 0)
    u1 = jnp.where(row == 0, up[7:8, :], pltpu.roll(u, 1, 0))
    u2 = jnp.where(row == 0, up[6:7, :], jnp.where(row == 1, up[7:8, :], pltpu.roll(u, 2, 0)))
    w = convw_ref[...]
    uc = w[0:1, :] * u2 + w[1:2, :] * u1 + w[2:3, :] * u
    cu_ref[...] = (cb * uc).astype(BF16)

    zt = _dot_nt(wt_ref[...], xb)
    qt_ref[...] = (zt[0:Q_DIM, :] * (ATTN_SCALE * LOG2_E)).astype(BF16)
    r0 = Q_DIM
    gt_ref[...] = _sigmoid(zt[r0:r0 + 2 * GATE_ROWS, :])
    r0 += 2 * GATE_ROWS
    vs = zt[r0:r0 + KV_DIM, :].astype(BF16)
    vw = zt[r0 + KV_DIM:r0 + 2 * KV_DIM, :].astype(BF16)
    for c in range(tm // ATT_TILE):
        vslct_ref[c] = vs[:, c * ATT_TILE:(c + 1) * ATT_TILE]
        vwint_ref[c] = vw[:, c * ATT_TILE:(c + 1) * ATT_TILE]


def _in_proj(x2, wtok, wt, conv_w, seq):
    T, D = x2.shape
    tm = 1024
    nt = T // tm
    n_tok = wtok.shape[1]
    n_t = wt.shape[0]
    row_blk = lambda n: pl.BlockSpec((tm, n), lambda i: (i, 0))
    out_shape = (
        jax.ShapeDtypeStruct((T // CMP_STRIDE, CMP_STRIDE * KV_DIM), BF16),
        jax.ShapeDtypeStruct((T // CMP_STRIDE, CMP_STRIDE * KV_DIM), BF16),
        jax.ShapeDtypeStruct((T, 2 * KV_DIM), BF16),
        jax.ShapeDtypeStruct((T, KV_DIM), BF16),
        jax.ShapeDtypeStruct((T, CONV_DIM), BF16),
        jax.ShapeDtypeStruct((Q_DIM, T), BF16),
        jax.ShapeDtypeStruct((2 * GATE_ROWS, T), F32),
        jax.ShapeDtypeStruct((T // ATT_TILE, KV_DIM, ATT_TILE), BF16),
        jax.ShapeDtypeStruct((T // ATT_TILE, KV_DIM, ATT_TILE), BF16),
    )
    vt_blk = pl.BlockSpec((tm // ATT_TILE, KV_DIM, ATT_TILE), lambda i: (i, 0, 0))
    chunk_blk = pl.BlockSpec((tm // CMP_STRIDE, CMP_STRIDE * KV_DIM), lambda i: (i, 0))
    return pl.pallas_call(
        functools.partial(_inproj_kernel, seq),
        grid=(nt,),
        in_specs=[
            row_blk(D),
            pl.BlockSpec((8, D), lambda i: (jnp.maximum(i * (tm // 8) - 1, 0), 0)),
            pl.BlockSpec((D, n_tok), lambda i: (0, 0)),
            pl.BlockSpec((n_t, D), lambda i: (0, 0)),
            pl.BlockSpec((3, CONV_DIM), lambda i: (0, 0)),
        ],
        out_specs=(
            chunk_blk, chunk_blk, row_blk(2 * KV_DIM), row_blk(KV_DIM), row_blk(CONV_DIM),
            pl.BlockSpec((Q_DIM, tm), lambda i: (0, i)),
            pl.BlockSpec((2 * GATE_ROWS, tm), lambda i: (0, i)),
            vt_blk, vt_blk,
        ),
        out_shape=out_shape,
        scratch_shapes=[pltpu.VMEM((2, tm, KV_DIM), F32)],
        compiler_params=_cparams(1),
        name="in_proj",
    )(x2, x2, wtok, wt, conv_w)


def _compress_kernel(kin_ref, vin_ref, wa_ref, wb_ref, pe_ref, w1_ref, b1_ref, w2_ref, b2_ref,
                     kc_ref, vct_ref):
    def one(idx, in_ref):
        c = in_ref[0]
        a = _dot(c, wa_ref[idx])
        b = _dot(c, wb_ref[idx])
        n = b.shape[0]
        peb = _dot(pe_ref[idx], w1_ref[idx])[0:1, :]
        bias = jnp.concatenate([peb, peb], axis=1) + b1_ref[idx]
        h = a + pltpu.roll(b, n - 1, 0) + bias
        return _dot(jax.nn.gelu(h).astype(BF16), w2_ref[idx]) + b2_ref[idx]

    kc_ref[0] = one(0, kin_ref).astype(BF16)
    vct_ref[0] = one(1, vin_ref).T.astype(BF16)


def _compress(kcmp3, vcmp3, wa, wb, pe, w1, b1, w2, b2):
    B, nch, width = kcmp3.shape
    hid2 = wa.shape[2]
    full = lambda a: pl.BlockSpec(a.shape, lambda b: (0,) * a.ndim)
    in_blk = pl.BlockSpec((1, nch, width), lambda b: (b, 0, 0))
    out_blk = pl.BlockSpec((1, nch, KV_DIM), lambda b: (b, 0, 0))
    return pl.pallas_call(
        _compress_kernel,
        grid=(B,),
        in_specs=[in_blk, in_blk, full(wa), full(wb), full(pe), full(w1), full(b1), full(w2), full(b2)],
        out_specs=(out_blk, pl.BlockSpec((1, KV_DIM, nch), lambda b: (b, 0, 0))),
        out_shape=(jax.ShapeDtypeStruct((B, nch, KV_DIM), BF16),
                   jax.ShapeDtypeStruct((B, KV_DIM, nch), BF16)),
        compiler_params=_cparams(1),
        name="compress",
    )(kcmp3, vcmp3, wa, wb, pe, w1, b1, w2, b2)


def _group_q(qt_ref, hh, g_is0):
    qh = qt_ref[hh * HEAD_DIM:(hh + 1) * HEAD_DIM, :]
    z = jnp.zeros_like(qh)
    return jnp.concatenate([qh, z], axis=0) if g_is0 else jnp.concatenate([z, qh], axis=0)


def _cmpattn_kernel(n_cmp, qt_ref, kc_ref, vct_ref, gt_ref, ovt_ref, ocmpt_ref, selb_ref):
    i = pl.program_id(1)
    tq = qt_ref.shape[1]
    nc = kc_ref.shape[1]
    n_sel = ovt_ref.shape[0]
    kc = kc_ref[0]
    t_n = i * tq + lax.broadcasted_iota(jnp.int32, (nc, tq), 1)
    n_io = lax.broadcasted_iota(jnp.int32, (nc, tq), 0)
    blk_end = jnp.where(n_io < n_cmp, n_io * CMP_STRIDE + CMP_BLOCK - 1, jnp.int32(2 ** 30))
    vis = blk_end <= t_n
    j_io = lax.broadcasted_iota(jnp.int32, (n_sel, tq), 0)
    t_j = i * tq + lax.broadcasted_iota(jnp.int32, (n_sel, tq), 1)
    cur = t_j // SEL_BLOCK
    bonus = jnp.where(j_io == 0, FORCE_BONUS,
                      jnp.where(j_io == cur, FORCE_BONUS, jnp.where(j_io == cur - 1, FORCE_BONUS, 0.0)))
    valid = j_io * SEL_BLOCK <= t_j
    for g in range(N_GROUPS_KV):
        imp = jnp.zeros((n_sel, tq), F32)
        for hh in range(HEADS_PER_GROUP):
            h = g * HEADS_PER_GROUP + hh
            s = _dot(kc, _group_q(qt_ref, h, g == 0))
            s = jnp.where(vis, s, NEG_INF)
            m = jnp.max(s, axis=0, keepdims=True)
            p = jnp.where(vis, jnp.exp2(s - m), 0.0)
            l = jnp.sum(p, axis=0, keepdims=True)
            pr = (p * jnp.where(l > 0.0, 1.0 / l, 0.0)).astype(BF16)
            o = _dot(vct_ref[0, g * HEAD_DIM:(g + 1) * HEAD_DIM, :], pr)
            gate = gt_ref[g * GATE_ROWS + hh * N_BRANCH:g * GATE_ROWS + hh * N_BRANCH + 1, :]
            ocmpt_ref[h * HEAD_DIM:(h + 1) * HEAD_DIM, :] = (o * gate).astype(BF16)
            imp = imp + _dot(ovt_ref[...], pr)
        score = jnp.where(valid, imp + bonus, -FORCE_BONUS)
        cnt = jnp.zeros((n_sel, tq), F32)
        for jp in range(n_sel):
            r = score[jp:jp + 1, :]
            cnt = cnt + jnp.where(r > score, 1.0, jnp.where(r == score, jnp.where(j_io > jp, 1.0, 0.0), 0.0))
        selb_ref[0, g] = jnp.where(cnt < float(min(SEL_TOPK, n_sel)), 0.0, NEG_INF).astype(BF16)


def _cmp_attn(qt, kc, vct, gt, ovt, B, seq, n_cmp):
    T = qt.shape[1]
    tq = ATT_TILE
    nq = seq // tq
    n_sel = ovt.shape[0]
    nc = kc.shape[1]
    return pl.pallas_call(
        functools.partial(_cmpattn_kernel, n_cmp),
        grid=(B, nq),
        in_specs=[
            pl.BlockSpec((Q_DIM, tq), lambda b, i: (0, b * nq + i)),
            pl.BlockSpec((1, nc, KV_DIM), lambda b, i: (b, 0, 0)),
            pl.BlockSpec((1, KV_DIM, nc), lambda b, i: (b, 0, 0)),
            pl.BlockSpec((2 * GATE_ROWS, tq), lambda b, i: (0, b * nq + i)),
            pl.BlockSpec(ovt.shape, lambda b, i: (0, 0)),
        ],
        out_specs=(
            pl.BlockSpec((Q_DIM, tq), lambda b, i: (0, b * nq + i)),
            pl.BlockSpec((1, N_GROUPS_KV, n_sel, tq), lambda b, i: (b * nq + i, 0, 0, 0)),
        ),
        out_shape=(jax.ShapeDtypeStruct((Q_DIM, T), BF16),
                   jax.ShapeDtypeStruct((B * nq, N_GROUPS_KV, n_sel, tq), BF16)),
        compiler_params=_cparams(2),
        name="cmp_attn",
    )(qt, kc, vct, gt, ovt)


def _nsa_kernel(qt_ref, kslc_ref, kwin_ref, vslct_ref, vwint_ref, selb_ref, gt_ref, ocmpt_ref, kmq_ref,
                ot_ref, q_s, s_s, m_s, acc_s):
    i = pl.program_id(1)
    tq = qt_ref.shape[1]
    tile = ATT_TILE
    pair = 2 * tile
    n_sel = selb_ref.shape[2]
    groups = range(N_GROUPS_KV)
    gq = HEADS_PER_GROUP * HEAD_DIM

    for g in groups:
        for hh in range(HEADS_PER_GROUP):
            lanes = slice(hh * tq, (hh + 1) * tq)
            qh = qt_ref[g * gq + hh * HEAD_DIM:g * gq + (hh + 1) * HEAD_DIM, :]
            zero = jnp.zeros_like(qh)
            q_s[g, 0:KV_DIM, lanes] = jnp.concatenate([qh, zero] if g == 0 else [zero, qh], axis=0)
            q_s[g, KV_DIM:KV_DIM + n_sel, lanes] = selb_ref[0, g]
            q_s[g, KV_DIM + n_sel:2 * KV_DIM, lanes] = jnp.zeros((KV_DIM - n_sel, tq), BF16)

    def v_rows(ref, g, tiles):
        vt = jnp.concatenate([ref[j, g * HEAD_DIM:(g + 1) * HEAD_DIM, :] for j in tiles], axis=1)
        return jnp.concatenate([vt, jnp.ones((SUM_ROWS, vt.shape[1]), BF16)], axis=0)

    def sel_scores(g, jj):
        keys = kslc_ref[pl.ds(pl.multiple_of(jj * pair, pair), pair), :]
        return _dot(keys, q_s[g])

    def sel_update(g, s, jj):
        m_old = m_s[g]
        m_new = jnp.maximum(m_old, jnp.max(s, axis=0, keepdims=True))
        p = jnp.exp2(s - m_new).astype(BF16)
        acc_s[g] = jnp.exp2(m_old - m_new) * acc_s[g] + _dot(v_rows(vslct_ref, g, (2 * jj, 2 * jj + 1)), p)
        m_s[g] = m_new

    m_s[...] = jnp.full(m_s.shape, NEG_INF, F32)
    acc_s[...] = jnp.zeros(acc_s.shape, F32)
    n_full = i // 2
    for g in groups:
        s_s[g] = sel_scores(g, 0)

    def sel_body(jj, carry):
        for g in groups:
            s = s_s[g]
            s_s[g] = sel_scores(g, jj + 1)
            sel_update(g, s, jj)
        return carry

    lax.fori_loop(0, n_full, sel_body, 0)

    j0 = jnp.maximum(i - WINDOW // tile, 0)
    n_win = WINDOW + tile
    behind = (i - j0) * tile - kmq_ref[...]
    in_window = pltpu.bitcast(behind, jnp.uint32) < WINDOW
    causal = kmq_ref[0:pair, :] <= (i - 2 * n_full) * tile
    for g in groups:
        sel_update(g, jnp.where(causal, s_s[g], NEG_INF), n_full)
        o_slc = acc_s[g, 0:HEAD_DIM, :] * (1.0 / acc_s[g, HEAD_DIM:HEAD_DIM + 1, :])
        s = _dot(kwin_ref[pl.ds(pl.multiple_of(j0 * tile, tile), n_win), :], q_s[g, 0:KV_DIM, :])
        s = jnp.where(in_window, s, NEG_INF)
        p = jnp.exp2(s - jnp.max(s, axis=0, keepdims=True)).astype(BF16)
        ow = _dot(v_rows(vwint_ref, g, (j0, j0 + 1, j0 + 2)), p)
        o_win = ow[0:HEAD_DIM, :] * (1.0 / ow[HEAD_DIM:HEAD_DIM + 1, :])
        for hh in range(HEADS_PER_GROUP):
            lanes = slice(hh * tq, (hh + 1) * tq)
            rows = slice(g * gq + hh * HEAD_DIM, g * gq + (hh + 1) * HEAD_DIM)
            gate = g * GATE_ROWS + hh * N_BRANCH
            o = (ocmpt_ref[rows, :].astype(F32) + gt_ref[gate + 1:gate + 2, :] * o_slc[:, lanes]
                 + gt_ref[gate + 2:gate + 3, :] * o_win[:, lanes])
            ot_ref[rows, :] = o.astype(BF16)


def _nsa_attn(qt, kslc, kwin, vslct, vwint, selb, gt, ocmpt, B, seq):
    T = qt.shape[1]
    tq = ATT_TILE
    nq = seq // tq
    ktiles = seq // ATT_TILE
    n_sel = selb.shape[2]
    wide = HEADS_PER_GROUP * tq
    n_win = WINDOW + ATT_TILE
    kmq = jnp.asarray(np.arange(n_win)[:, None] - (np.arange(wide) % tq)[None, :], jnp.int32)
    qblk = pl.BlockSpec((Q_DIM, tq), lambda b, i: (0, b * nq + i))
    vblk = pl.BlockSpec((ktiles, KV_DIM, ATT_TILE), lambda b, i: (b, 0, 0))
    return pl.pallas_call(
        _nsa_kernel,
        grid=(B, nq),
        in_specs=[
            qblk,
            pl.BlockSpec((seq, 2 * KV_DIM), lambda b, i: (b, 0)),
            pl.BlockSpec((seq, KV_DIM), lambda b, i: (b, 0)),
            vblk, vblk,
            pl.BlockSpec((1, N_GROUPS_KV, n_sel, tq), lambda b, i: (b * nq + i, 0, 0, 0)),
            pl.BlockSpec((N_GROUPS_KV * GATE_ROWS, tq), lambda b, i: (0, b * nq + i)),
            qblk,
            pl.BlockSpec((n_win, wide), lambda b, i: (0, 0)),
        ],
        out_specs=qblk,
        out_shape=jax.ShapeDtypeStruct((Q_DIM, T), BF16),
        scratch_shapes=[
            pltpu.VMEM((N_GROUPS_KV, 2 * KV_DIM, wide), BF16),
            pltpu.VMEM((N_GROUPS_KV, 2 * ATT_TILE, wide), F32),
            pltpu.VMEM((N_GROUPS_KV, 1, wide), F32),
            pltpu.VMEM((N_GROUPS_KV, HEAD_DIM + SUM_ROWS, wide), F32),
        ],
        compiler_params=_cparams(2),
        name="nsa_attn",
    )(qt, kslc, kwin, vslct, vwint, selb, gt, ocmpt, kmq)


def _route(lt):
    row = lax.broadcasted_iota(jnp.int32, lt.shape, 0)
    big = 2 ** 20
    col = lambda f, v: f(v, axis=0, keepdims=True)
    grp = jnp.where((row >= N_EXPERTS) & (row < N_EXPERTS + N_EXPERT_GROUPS), lt, NEG_INF)
    gmax = col(jnp.max, grp)
    g_idx = col(jnp.min, jnp.where(grp == gmax, row, big)) - N_EXPERTS
    p_group = 1.0 / col(jnp.sum, jnp.exp(grp - gmax))
    own = jnp.where((row < N_EXPERTS) & (row // EXPERTS_PER_GROUP == g_idx), lt, NEG_INF)
    l1 = col(jnp.max, own)
    e1 = col(jnp.min, jnp.where(own == l1, row, big))
    rest = jnp.where(row == e1, NEG_INF, own)
    l2 = col(jnp.max, rest)
    e2 = col(jnp.min, jnp.where(rest == l2, row, big))
    r = jnp.exp(l2 - l1)
    c1 = p_group / (1.0 + r)
    return e1, e2, c1, c1 * r


R_SLOT1, R_SLOT2, R_C1, R_C2 = range(4)
ROUTE_ROWS = 40


def _merge_kernel(alpha, ot_ref, cu_ref, x_ref, wn_ref, wc_ref, wgm_ref, wo_ref, g1_ref, b1_ref,
                  wrt_ref, brt_ref, h1_ref, routet_ref, cnt_ref):
    tm, d = x_ref.shape
    iota = lambda shape, dim: lax.broadcasted_iota(jnp.int32, shape, dim)
    sq = (MOE_TILE, MOE_TILE)
    earlier_token = jnp.where(iota(sq, 0) < iota(sq, 1), 1.0, 0.0).astype(BF16)
    lower_expert = jnp.where(iota((LANES, LANES), 0) < iota((LANES, LANES), 1), 1.0, 0.0).astype(BF16)
    x = x_ref[...]
    xb = x.astype(BF16)
    y_nsa = _dot_tn(ot_ref[...], wn_ref[...])
    y_conv = _dot(cu_ref[...], wc_ref[...])
    merged = (_sigmoid(_dot(xb, wgm_ref[:, 0:d])) * y_nsa
              + _sigmoid(_dot(xb, wgm_ref[:, d:2 * d])) * y_conv)
    mix = _dot(merged.astype(BF16), wo_ref[...])
    h1 = _layer_norm(alpha * x + mix, g1_ref[...], b1_ref[...])
    h1_ref[...] = h1
    lt = (_dot_nt(wrt_ref[...], h1.astype(BF16)) + brt_ref[...])[0:ROUTE_ROWS, :]
    e1, e2, c1, c2 = _route(lt)
    row = iota((LANES, tm), 0)
    pick1 = row == e1
    pick2 = row == e2
    one = lambda m: jnp.where(m, 1.0, 0.0).astype(BF16)
    ones8 = jnp.ones((8, MOE_TILE), BF16)
    recs = []
    for sub in range(tm // MOE_TILE):
        cols = slice(sub * MOE_TILE, (sub + 1) * MOE_TILE)
        oh1, oh2 = one(pick1[:, cols]), one(pick2[:, cols])
        oh = oh1 + oh2
        before = _dot(oh, earlier_token)
        cnt = _dot_nt(ones8, oh)
        chunks = jnp.floor((cnt + (CHUNK - 1.0)) * (1.0 / CHUNK))
        run_start = (_dot(chunks.astype(BF16), lower_expert) * CHUNK).astype(BF16)
        slot1 = (_dot(run_start, oh1)[0:1, :]
                 + jnp.sum(jnp.where(pick1[:, cols], before, 0.0), axis=0, keepdims=True))
        slot2 = (_dot(run_start, oh2)[0:1, :]
                 + jnp.sum(jnp.where(pick2[:, cols], before, 0.0), axis=0, keepdims=True))
        recs.append(jnp.concatenate([slot1, slot2, c1[:, cols], c2[:, cols],
                                     jnp.zeros((4, MOE_TILE), F32)], axis=0))
        cnt_ref[sub] = cnt
    routet_ref[...] = jnp.concatenate(recs, axis=1)


def _merge(ot, cu, x2, wn, wc, wgm, wo, g1, b1, wrt, brt, alpha):
    T, D = x2.shape
    tm = brt.shape[1]
    sub = tm // MOE_TILE
    full = lambda a: pl.BlockSpec(a.shape, lambda i: (0,) * a.ndim)
    return pl.pallas_call(
        functools.partial(_merge_kernel, alpha),
        grid=(T // tm,),
        in_specs=[
            pl.BlockSpec((Q_DIM, tm), lambda i: (0, i)),
            pl.BlockSpec((tm, CONV_DIM), lambda i: (i, 0)),
            pl.BlockSpec((tm, D), lambda i: (i, 0)),
            full(wn), full(wc), full(wgm), full(wo), full(g1), full(b1), full(wrt), full(brt),
        ],
        out_specs=(pl.BlockSpec((tm, D), lambda i: (i, 0)),
                   pl.BlockSpec((8, tm), lambda i: (0, i)),
                   pl.BlockSpec((sub, 8, LANES), lambda i: (i, 0, 0))),
        out_shape=(jax.ShapeDtypeStruct((T, D), F32),
                   jax.ShapeDtypeStruct((8, T), F32),
                   jax.ShapeDtypeStruct((T // MOE_TILE, 8, LANES), F32)),
        compiler_params=_cparams(1),
        name="merge",
    )(ot, cu, x2, wn, wc, wgm, wo, g1, b1, wrt, brt)


def _chunk_rows(q):
    return pl.ds(pl.multiple_of(q * CHUNK, CHUNK), CHUNK)


def _dispatch_kernel(tab_ref, tab_m1_ref, tab_m2_ref, tails_ref, nused_ref, h1_ref, routet_ref, xs_ref,
                     buf, zero_s, sem):
    i = pl.program_id(0)
    last = pl.num_programs(0) - 1
    tm = h1_ref.shape[0]
    slot = i % 2

    def drain(tab, slot_):
        def wait(q, c):
            pltpu.make_async_copy(buf.at[slot_, _chunk_rows(0)], xs_ref.at[pl.ds(0, CHUNK)], sem.at[slot_]).wait()
            return c
        lax.fori_loop(0, tab[0, 0, TAB_COUNT], wait, 0)

    s1 = routet_ref[R_SLOT1:R_SLOT1 + 1, :].astype(jnp.int32)
    s2 = routet_ref[R_SLOT2:R_SLOT2 + 1, :].astype(jnp.int32)
    r_io = lax.broadcasted_iota(jnp.int32, (LOCAL_ROWS, tm), 0)
    perm = jnp.where(r_io == s1, 1.0, jnp.where(r_io == s2, 1.0, 0.0)).astype(BF16)
    srt = _dot(perm, h1_ref[...].astype(BF16)).astype(BF16)

    @pl.when(i >= 2)
    def _():
        drain(tab_m2_ref, slot)

    buf[slot] = srt

    def start(q, c):
        dst = pl.multiple_of(tab_ref[0, 0, q], CHUNK)
        pltpu.make_async_copy(buf.at[slot, _chunk_rows(q)], xs_ref.at[pl.ds(dst, CHUNK)], sem.at[slot]).start()
        return c

    lax.fori_loop(0, tab_ref[0, 0, TAB_COUNT], start, 0)

    @pl.when(i == last)
    def _():
        drain(tab_ref, slot)

    @pl.when(jnp.logical_and(i == last, i >= 1))
    def _():
        drain(tab_m1_ref, 1 - slot)

    @pl.when(i == last)
    def _():
        zero_s[...] = jnp.zeros(zero_s.shape, zero_s.dtype)

        def tail_copy(q):
            dst = pl.multiple_of(jnp.maximum(tails_ref[q], 0), CHUNK)
            return pltpu.make_async_copy(zero_s.at[pl.ds(0, CHUNK)], xs_ref.at[pl.ds(dst, CHUNK)], sem.at[2])

        def tile_copy(j):
            dst = pl.multiple_of(j * EXPERT_TILE, EXPERT_TILE)
            return pltpu.make_async_copy(zero_s, xs_ref.at[pl.ds(dst, EXPERT_TILE)], sem.at[2])

        def start_tail(q, c):
            @pl.when(tails_ref[q] >= 0)
            def _():
                tail_copy(q).start()
            return c

        def wait_tail(q, c):
            @pl.when(tails_ref[q] >= 0)
            def _():
                tail_copy(q).wait()
            return c

        def start_tile(j, c):
            tile_copy(j).start()
            return c

        def wait_tile(j, c):
            tile_copy(j).wait()
            return c

        n_tiles = xs_ref.shape[0] // EXPERT_TILE
        lax.fori_loop(0, tails_ref.shape[0], start_tail, 0)
        lax.fori_loop(nused_ref[0], n_tiles, start_tile, 0)
        lax.fori_loop(0, tails_ref.shape[0], wait_tail, 0)
        lax.fori_loop(nused_ref[0], n_tiles, wait_tile, 0)


def _dispatch(tab, tails, n_used, h1, route_t, n_rows):
    T, D = h1.shape
    tm = MOE_TILE
    tab_blk = lambda back: pl.BlockSpec((1, 1, LANES), lambda i: (jnp.maximum(i - back, 0), 0, 0),
                                        memory_space=pltpu.SMEM)
    return pl.pallas_call(
        _dispatch_kernel,
        grid=(T // tm,),
        in_specs=[
            tab_blk(0), tab_blk(1), tab_blk(2),
            pl.BlockSpec(memory_space=pltpu.SMEM),
            pl.BlockSpec(memory_space=pltpu.SMEM),
            pl.BlockSpec((tm, D), lambda i: (i, 0)),
            pl.BlockSpec((8, tm), lambda i: (0, i)),
        ],
        out_specs=pl.BlockSpec(memory_space=pl.ANY),
        out_shape=jax.ShapeDtypeStruct((n_rows, D), BF16),
        scratch_shapes=[pltpu.VMEM((2, LOCAL_ROWS, D), BF16), pltpu.VMEM((EXPERT_TILE, D), BF16),
                        pltpu.SemaphoreType.DMA((3,))],
        compiler_params=_cparams(1),
        name="moe_dispatch",
    )(tab, tab, tab, tails, n_used, h1, route_t)


def _experts_kernel(te_ref, nu_ref, xs_ref, wg_ref, wu_ref, wd_ref, ys_ref):
    del te_ref
    j = pl.program_id(0)

    @pl.when(j < nu_ref[0])
    def _():
        xb = xs_ref[...]
        hg = _dot(xb, wg_ref[0])
        h = hg * _sigmoid(hg) * _dot(xb, wu_ref[0])
        y = _dot(h.astype(BF16), wd_ref[0])
        ys_ref[...] = y.astype(BF16)

    @pl.when(j >= nu_ref[0])
    def _():
        ys_ref[...] = jnp.zeros(ys_ref.shape, ys_ref.dtype)


def _experts(tile_expert, n_used, xs, wg, wu, wd):
    n_rows, D = xs.shape
    hid = wg.shape[2]
    tm = EXPERT_TILE
    used = lambda j, te, nu: (jnp.minimum(j, nu[0] - 1), 0)
    grid_spec = pltpu.PrefetchScalarGridSpec(
        num_scalar_prefetch=2,
        grid=(n_rows // tm,),
        in_specs=[
            pl.BlockSpec((tm, D), used),
            pl.BlockSpec((1, D, hid), lambda j, te, nu: (te[j], 0, 0)),
            pl.BlockSpec((1, D, hid), lambda j, te, nu: (te[j], 0, 0)),
            pl.BlockSpec((1, hid, D), lambda j, te, nu: (te[j], 0, 0)),
        ],
        out_specs=pl.BlockSpec((tm, D), lambda j, te, nu: (j, 0)),
    )
    return pl.pallas_call(
        _experts_kernel,
        grid_spec=grid_spec,
        out_shape=jax.ShapeDtypeStruct((n_rows, D), BF16),
        compiler_params=_cparams(1),
        name="moe_experts",
    )(tile_expert, n_used, xs, wg, wu, wd)


def _combine_kernel(alpha, tab_ref, tab_next_ref, routet_ref, h1_ref, p_ref, pproj_ref, pgw_ref,
                    pgb_ref, g2_ref, b2_ref, ys_ref, out_ref, buf, sem):
    i = pl.program_id(0)
    tm = h1_ref.shape[0]
    slot = i % 2

    def gather(tab, slot_):
        def start(q, c):
            src = pl.multiple_of(tab[0, 0, q], CHUNK)
            pltpu.make_async_copy(ys_ref.at[pl.ds(src, CHUNK)], buf.at[slot_, _chunk_rows(q)],
                                  sem.at[slot_]).start()
            return c
        lax.fori_loop(0, tab[0, 0, TAB_COUNT], start, 0)

    @pl.when(i == 0)
    def _():
        buf[...] = jnp.zeros(buf.shape, buf.dtype)
        gather(tab_ref, 0)

    @pl.when(i + 1 < pl.num_programs(0))
    def _():
        gather(tab_next_ref, 1 - slot)

    h1 = h1_ref[...]
    ple = (_sigmoid(_dot(h1.astype(BF16), pgw_ref[...]) + pgb_ref[...])
           * _dot(p_ref[...].astype(BF16), pproj_ref[...]))

    def wait(q, c):
        pltpu.make_async_copy(ys_ref.at[pl.ds(0, CHUNK)], buf.at[slot, _chunk_rows(0)], sem.at[slot]).wait()
        return c

    lax.fori_loop(0, tab_ref[0, 0, TAB_COUNT], wait, 0)

    s1 = routet_ref[R_SLOT1:R_SLOT1 + 1, :].astype(jnp.int32)
    s2 = routet_ref[R_SLOT2:R_SLOT2 + 1, :].astype(jnp.int32)
    c1 = routet_ref[R_C1:R_C1 + 1, :]
    c2 = routet_ref[R_C2:R_C2 + 1, :]
    r_io = lax.broadcasted_iota(jnp.int32, (LOCAL_ROWS, tm), 0)
    weights = jnp.where(r_io == s1, c1, jnp.where(r_io == s2, c2, 0.0)).astype(BF16)
    ffn = _dot_tn(weights, buf[slot])
    out_ref[...] = _layer_norm(alpha * h1 + ffn + ple, g2_ref[...], b2_ref[...])


def _combine(tab, route_t, h1, p2, pproj, pgw, pgb, g2, b2, ys, alpha):
    T, D = h1.shape
    tm = MOE_TILE
    nt = T // tm
    full = lambda a: pl.BlockSpec(a.shape, lambda i: (0,) * a.ndim)
    return pl.pallas_call(
        functools.partial(_combine_kernel, alpha),
        grid=(nt,),
        in_specs=[
            pl.BlockSpec((1, 1, LANES), lambda i: (i, 0, 0), memory_space=pltpu.SMEM),
            pl.BlockSpec((1, 1, LANES), lambda i: (jnp.minimum(i + 1, nt - 1), 0, 0), memory_space=pltpu.SMEM),
            pl.BlockSpec((8, tm), lambda i: (0, i)),
            pl.BlockSpec((tm, D), lambda i: (i, 0)),
            pl.BlockSpec((tm, p2.shape[1]), lambda i: (i, 0)),
            full(pproj), full(pgw), full(pgb), full(g2), full(b2),
            pl.BlockSpec(memory_space=pl.ANY),
        ],
        out_specs=pl.BlockSpec((tm, D), lambda i: (i, 0)),
        out_shape=jax.ShapeDtypeStruct((T, D), F32),
        scratch_shapes=[pltpu.VMEM((2, LOCAL_ROWS, D), BF16), pltpu.SemaphoreType.DMA((2,))],
        compiler_params=_cparams(1),
        name="moe_combine",
    )(tab, tab, route_t, h1, p2, pproj, pgw, pgb, g2, b2, ys)


def _moe_plan(cnt_tiles, n_expert_tiles):
    cnt = cnt_tiles[:, 0, :N_EXPERTS].astype(jnp.int32)
    cnt8 = (cnt + CHUNK - 1) // CHUNK * CHUNK
    lend = jnp.cumsum(cnt8, axis=1)
    lstart = lend - cnt8
    gend = jnp.cumsum(cnt8, axis=0)
    region = (gend[-1] + EXPERT_TILE - 1) // EXPERT_TILE * EXPERT_TILE
    oend = jnp.cumsum(region)
    gstart = (oend - region)[None, :] + gend - cnt8
    q8 = jnp.arange(MAX_CHUNKS, dtype=jnp.int32) * CHUNK
    eq = jnp.minimum(jnp.sum((lend[:, None, :] <= q8[None, :, None]).astype(jnp.int32), axis=-1), N_EXPERTS - 1)
    shift = jnp.sum(jnp.where(eq[:, :, None] == jnp.arange(N_EXPERTS)[None, None, :],
                              (gstart - lstart)[:, None, :], 0), axis=-1)
    dstq = shift + q8[None, :]
    nt = cnt.shape[0]
    tab = jnp.zeros((nt, LANES), jnp.int32).at[:, :MAX_CHUNKS].set(dstq).at[:, TAB_COUNT].set(lend[:, -1] // CHUNK)
    tile_row = jnp.arange(n_expert_tiles, dtype=jnp.int32) * EXPERT_TILE
    tile_expert = jnp.minimum(jnp.sum((oend[None, :] <= tile_row[:, None]).astype(jnp.int32), axis=1),
                              N_EXPERTS - 1)
    n_used = (oend[-1] // EXPERT_TILE).reshape(1)
    c = jnp.arange(EXPERT_TILE // CHUNK, dtype=jnp.int32)[None, :] * CHUNK
    tail_start = (oend - region + gend[-1])[:, None] + c
    tails = jnp.where(tail_start < oend[:, None], tail_start, -1).reshape(-1)
    return tab.reshape(nt, 1, LANES), tails, tile_expert, n_used


def _split_w_in(w_in, D):
    sizes = [Q_DIM] + [KV_DIM] * 6 + [N_HEADS * N_BRANCH] + [CONV_DIM] * 3 + [D] * 2
    offs = np.concatenate([[0], np.cumsum(sizes)])
    names = ["q", "k_cmp", "v_cmp", "k_slc", "v_slc", "k_win", "v_win", "g_nsa",
             "conv_b", "conv_c", "conv_h", "g_m_nsa", "g_m_conv"]
    return {n: w_in[:, int(offs[k]):int(offs[k + 1])] for k, n in enumerate(names)}


def _layer(x2, p2, B, seq, depth, w_in, cmp_pe, cmp_w1, cmp_b1, cmp_w2, cmp_b2, conv_w, w_nsa_out,
           w_conv_out, w_o, ln1_g, ln1_b, rg_w, rg_b, re_w, re_b, e_wg, e_wu, e_wd, ple_proj,
           ple_gate_w, ple_gate_b, ln2_g, ln2_b):
    T, D = x2.shape
    alpha = (2.0 * depth) ** 0.25
    w = _split_w_in(w_in, D)
    wtok = jnp.concatenate([w["k_cmp"], w["v_cmp"], w["k_slc"], w["k_win"],
                            w["conv_b"], w["conv_c"], w["conv_h"]], axis=1).astype(BF16)
    gcols = w["g_nsa"].reshape(D, N_GROUPS_KV, HEADS_PER_GROUP * N_BRANCH)
    gcols = jnp.pad(gcols, ((0, 0), (0, 0), (0, GATE_ROWS - HEADS_PER_GROUP * N_BRANCH)))
    wt = jnp.concatenate([w["q"], gcols.reshape(D, N_GROUPS_KV * GATE_ROWS), w["v_slc"], w["v_win"]],
                         axis=1).T.astype(BF16)
    kcmp, vcmp, kslc, kwin, cu, qt, gt, vslct, vwint = _in_proj(x2, wtok, wt, conv_w, seq)

    half = CMP_BLOCK // 2
    n_chunks = seq // CMP_STRIDE
    n_cmp = (seq - CMP_BLOCK) // CMP_STRIDE + 1
    hidden = cmp_w1.shape[-1]
    eye = jnp.eye(N_GROUPS_KV, dtype=F32)
    w1r = cmp_w1.reshape(2, CMP_BLOCK, HEAD_DIM, hidden)
    expand = lambda m: jnp.einsum("ildh,gk->ilgdkh", m, eye).reshape(
        2, half * KV_DIM, N_GROUPS_KV * hidden).astype(BF16)
    wa, wb = expand(w1r[:, :half]), expand(w1r[:, half:])
    pe = jnp.broadcast_to(cmp_pe.reshape(2, 1, CMP_BLOCK * HEAD_DIM), (2, 8, CMP_BLOCK * HEAD_DIM)).astype(BF16)
    b1t = jnp.tile(cmp_b1.reshape(2, 1, hidden), (1, 1, N_GROUPS_KV))
    w2b = jnp.einsum("ihd,gk->ighkd", cmp_w2, eye).reshape(2, N_GROUPS_KV * hidden, KV_DIM).astype(BF16)
    b2t = jnp.tile(cmp_b2.reshape(2, 1, HEAD_DIM), (1, 1, N_GROUPS_KV))
    kc, vct = _compress(kcmp.reshape(B, n_chunks, CMP_STRIDE * KV_DIM),
                        vcmp.reshape(B, n_chunks, CMP_STRIDE * KV_DIM),
                        wa, wb, pe, cmp_w1.astype(BF16), b1t, w2b, b2t)

    n_sel = seq // SEL_BLOCK
    c_start = np.arange(n_chunks) * CMP_STRIDE
    s_start = np.arange(n_sel) * SEL_BLOCK
    overlap = ((c_start[None, :] <= s_start[:, None] + SEL_BLOCK - 1)
               & (c_start[None, :] + CMP_BLOCK - 1 >= s_start[:, None])).astype(np.float32)
    ocmpt, selb = _cmp_attn(qt, kc, vct, gt, jnp.asarray(overlap, BF16), B, seq, n_cmp)
    ot = _nsa_attn(qt, kslc, kwin, vslct, vwint, selb, gt, ocmpt, B, seq)

    wgm = jnp.concatenate([w["g_m_nsa"], w["g_m_conv"]], axis=1).astype(BF16)
    merge_tile = 512
    wrt = jnp.pad(jnp.concatenate([re_w, rg_w], axis=1).T, ((0, LANES - N_EXPERTS - N_EXPERT_GROUPS), (0, 0)))
    brt = jnp.pad(jnp.concatenate([re_b, rg_b]), (0, LANES - N_EXPERTS - N_EXPERT_GROUPS))
    brt = jnp.broadcast_to(brt[:, None], (LANES, merge_tile))
    h1, route_t, cnt_tiles = _merge(ot, cu, x2, w_nsa_out.astype(BF16), w_conv_out.astype(BF16), wgm,
                                    w_o.astype(BF16), ln1_g.reshape(1, D), ln1_b.reshape(1, D),
                                    wrt.astype(BF16), brt, alpha)
    n_tok_tiles = T // MOE_TILE
    max_rows = 2 * T + n_tok_tiles * N_EXPERTS * (CHUNK - 1) + N_EXPERTS * (EXPERT_TILE - 1)
    n_expert_tiles = -(-max_rows // EXPERT_TILE)
    tab, tails, tile_expert, n_used = _moe_plan(cnt_tiles, n_expert_tiles)
    xs = _dispatch(tab, tails, n_used, h1, route_t, n_expert_tiles * EXPERT_TILE)
    ys = _experts(tile_expert, n_used, xs, e_wg.astype(BF16), e_wu.astype(BF16), e_wd.astype(BF16))
    return _combine(tab, route_t, h1, p2, ple_proj.astype(BF16), ple_gate_w.astype(BF16),
                    ple_gate_b.reshape(1, D), ln2_g.reshape(1, D), ln2_b.reshape(1, D), ys, alpha)


def kernel(x, p, w_in, cmp_pe, cmp_w1, cmp_b1, cmp_w2, cmp_b2, conv_w, w_nsa_out, w_conv_out, w_o, ln1_g, ln1_b, router_group_w, router_group_b, router_expert_w, router_expert_b, expert_w_gate, expert_w_up, expert_w_down, ple_proj, ple_gate_w, ple_gate_b, ln2_g, ln2_b):
    B, seq, D = x.shape
    depth = w_in.shape[0]
    x2 = x.reshape(B * seq, D)
    for i in range(depth):
        x2 = _layer(x2, p[i].reshape(B * seq, -1), B, seq, depth, w_in[i], cmp_pe[i], cmp_w1[i], cmp_b1[i],
                    cmp_w2[i], cmp_b2[i], conv_w[i], w_nsa_out[i], w_conv_out[i], w_o[i], ln1_g[i], ln1_b[i],
                    router_group_w[i], router_group_b[i], router_expert_w[i], router_expert_b[i],
                    expert_w_gate[i], expert_w_up[i], expert_w_down[i], ple_proj[i], ple_gate_w[i],
                    ple_gate_b[i], ln2_g[i], ln2_b[i])
    return x2.reshape(B, seq, D)
```

```python
import functools

import jax
import jax.numpy as jnp
import numpy as np
from jax import lax
from jax.experimental import pallas as pl
from jax.experimental.pallas import tpu as pltpu

F32 = jnp.float32
BF16 = jnp.bfloat16

N_HEADS = 8
N_GROUPS_KV = 2
HEADS_PER_GROUP = N_HEADS // N_GROUPS_KV
HEAD_DIM = 64
Q_DIM = N_HEADS * HEAD_DIM
KV_DIM = N_GROUPS_KV * HEAD_DIM
N_BRANCH = 3
CMP_BLOCK = 32
CMP_STRIDE = 16
SEL_BLOCK = 64
SEL_TOPK = 16
WINDOW = 512
CONV_DIM = 512
N_EXPERT_GROUPS = 4
EXPERTS_PER_GROUP = 8
N_EXPERTS = N_EXPERT_GROUPS * EXPERTS_PER_GROUP
ATTN_SCALE = HEAD_DIM ** -0.5
LOG2_E = 1.4426950408889634
FORCE_BONUS = 1e4
NEG_INF = -1e30
LN_EPS = 1e-5

LANES = 128
ATT_TILE = 256
GATE_ROWS = 16
SUM_ROWS = 16
VMEM_LIMIT = 48 * 1024 * 1024

MOE_TILE = 256
EXPERT_TILE = 1024
CHUNK = 16
LOCAL_ROWS = -(-(2 * MOE_TILE + N_EXPERTS * (CHUNK - 1)) // ATT_TILE) * ATT_TILE
MAX_CHUNKS = LOCAL_ROWS // CHUNK
TAB_COUNT = LANES - 1


def _cparams(n_axes):
    return pltpu.CompilerParams(dimension_semantics=("arbitrary",) * n_axes,
                                vmem_limit_bytes=VMEM_LIMIT)


def _dot(a, b):
    return jnp.dot(a, b, preferred_element_type=F32)


def _dot_nt(a, b):
    return lax.dot_general(a, b, (((1,), (1,)), ((), ())), preferred_element_type=F32)


def _dot_tn(a, b):
    return lax.dot_general(a, b, (((0,), (0,)), ((), ())), preferred_element_type=F32)


def _sigmoid(v):
    return 1.0 / (1.0 + jnp.exp(-v))


def _layer_norm(v, g, b):
    mu = jnp.mean(v, axis=-1, keepdims=True)
    d = v - mu
    var = jnp.mean(d * d, axis=-1, keepdims=True)
    return d * lax.rsqrt(var + LN_EPS) * g + b


def _inproj_kernel(seq, x_ref, xprev_ref, wtok_ref, wt_ref, convw_ref,
                   kcmp_ref, vcmp_ref, kslc_ref, kwin_ref, cu_ref, qt_ref, gt_ref, vslct_ref, vwint_ref, cmp_s):
    i = pl.program_id(0)
    tm = x_ref.shape[0]
    xb = x_ref[...].astype(BF16)

    c0 = 4 * KV_DIM
    kvf = _dot(xb, wtok_ref[:, 0:c0])
    kv = kvf.astype(BF16)
    cmp_s[0] = kvf[:, 0:KV_DIM]
    cmp_s[1] = kvf[:, KV_DIM:2 * KV_DIM]
    for l in range(CMP_STRIDE):
        rows = pl.ds(l, tm // CMP_STRIDE, stride=CMP_STRIDE)
        kcmp_ref[:, l * KV_DIM:(l + 1) * KV_DIM] = cmp_s[0, rows, :].astype(BF16)
        vcmp_ref[:, l * KV_DIM:(l + 1) * KV_DIM] = cmp_s[1, rows, :].astype(BF16)
    kwin_ref[...] = kv[:, 3 * KV_DIM:4 * KV_DIM]
    n_sel = seq // SEL_BLOCK
    blk = ((i * tm + lax.broadcasted_iota(jnp.int32, (tm, KV_DIM), 0)) // SEL_BLOCK) % n_sel
    onehot = jnp.where(lax.broadcasted_iota(jnp.int32, (tm, KV_DIM), 1) == blk, 1.0, 0.0).astype(BF16)
    kslc_ref[...] = jnp.concatenate([kv[:, 2 * KV_DIM:3 * KV_DIM], onehot], axis=1)
    conv = _dot(xb, wtok_ref[:, c0:c0 + 3 * CONV_DIM])
    cb = conv[:, 0:CONV_DIM]
    u = conv[:, CONV_DIM:2 * CONV_DIM] * conv[:, 2 * CONV_DIM:3 * CONV_DIM]
    xpb = xprev_ref[...].astype(BF16)
    up = (_dot(xpb, wtok_ref[:, c0 + CONV_DIM:c0 + 2 * CONV_DIM])
          * _dot(xpb, wtok_ref[:, c0 + 2 * CONV_DIM:c0 + 3 * CONV_DIM]))
    up = jnp.where(i % (seq // tm) == 0, 0.0, up)
    row = lax.broadcasted_iota(jnp.int32, (tm, CONV_DIM), 0)
    u1 = jnp.where(row == 0, up[7:8, :], pltpu.roll(u, 1, 0))
    u2 = jnp.where(row == 0, up[6:7, :], jnp.where(row == 1, up[7:8, :], pltpu.roll(u, 2, 0)))
    w = convw_ref[...]
    uc = w[0:1, :] * u2 + w[1:2, :] * u1 + w[2:3, :] * u
    cu_ref[...] = (cb * uc).astype(BF16)

    zt = _dot_nt(wt_ref[...], xb)
    qt_ref[...] = (zt[0:Q_DIM, :] * (ATTN_SCALE * LOG2_E)).astype(BF16)
    r0 = Q_DIM
    gt_ref[...] = _sigmoid(zt[r0:r0 + 2 * GATE_ROWS, :])
    r0 += 2 * GATE_ROWS
    vs = zt[r0:r0 + KV_DIM, :].astype(BF16)
    vw = zt[r0 + KV_DIM:r0 + 2 * KV_DIM, :].astype(BF16)
    for c in range(tm // ATT_TILE):
        vslct_ref[c] = vs[:, c * ATT_TILE:(c + 1) * ATT_TILE]
        vwint_ref[c] = vw[:, c * ATT_TILE:(c + 1) * ATT_TILE]


def _in_proj(x2, wtok, wt, conv_w, seq):
    T, D = x2.shape
    tm = 1024
    nt = T // tm
    n_tok = wtok.shape[1]
    n_t = wt.shape[0]
    row_blk = lambda n: pl.BlockSpec((tm, n), lambda i: (i, 0))
    out_shape = (
        jax.ShapeDtypeStruct((T // CMP_STRIDE, CMP_STRIDE * KV_DIM), BF16),
        jax.ShapeDtypeStruct((T // CMP_STRIDE, CMP_STRIDE * KV_DIM), BF16),
        jax.ShapeDtypeStruct((T, 2 * KV_DIM), BF16),
        jax.ShapeDtypeStruct((T, KV_DIM), BF16),
        jax.ShapeDtypeStruct((T, CONV_DIM), BF16),
        jax.ShapeDtypeStruct((Q_DIM, T), BF16),
        jax.ShapeDtypeStruct((2 * GATE_ROWS, T), F32),
        jax.ShapeDtypeStruct((T // ATT_TILE, KV_DIM, ATT_TILE), BF16),
        jax.ShapeDtypeStruct((T // ATT_TILE, KV_DIM, ATT_TILE), BF16),
    )
    vt_blk = pl.BlockSpec((tm // ATT_TILE, KV_DIM, ATT_TILE), lambda i: (i, 0, 0))
    chunk_blk = pl.BlockSpec((tm // CMP_STRIDE, CMP_STRIDE * KV_DIM), lambda i: (i, 0))
    return pl.pallas_call(
        functools.partial(_inproj_kernel, seq),
        grid=(nt,),
        in_specs=[
            row_blk(D),
            pl.BlockSpec((8, D), lambda i: (jnp.maximum(i * (tm // 8) - 1, 0), 0)),
            pl.BlockSpec((D, n_tok), lambda i: (0, 0)),
            pl.BlockSpec((n_t, D), lambda i: (0, 0)),
            pl.BlockSpec((3, CONV_DIM), lambda i: (0, 0)),
        ],
        out_specs=(
            chunk_blk, chunk_blk, row_blk(2 * KV_DIM), row_blk(KV_DIM), row_blk(CONV_DIM),
            pl.BlockSpec((Q_DIM, tm), lambda i: (0, i)),
            pl.BlockSpec((2 * GATE_ROWS, tm), lambda i: (0, i)),
            vt_blk, vt_blk,
        ),
        out_shape=out_shape,
        scratch_shapes=[pltpu.VMEM((2, tm, KV_DIM), F32)],
        compiler_params=_cparams(1),
        name="in_proj",
    )(x2, x2, wtok, wt, conv_w)


def _compress_kernel(kin_ref, vin_ref, wa_ref, wb_ref, pe_ref, w1_ref, b1_ref, w2_ref, b2_ref,
                     kc_ref, vct_ref):
    def one(idx, in_ref):
        c = in_ref[0]
        a = _dot(c, wa_ref[idx])
        b = _dot(c, wb_ref[idx])
        n = b.shape[0]
        peb = _dot(pe_ref[idx], w1_ref[idx])[0:1, :]
        bias = jnp.concatenate([peb, peb], axis=1) + b1_ref[idx]
        h = a + pltpu.roll(b, n - 1, 0) + bias
        return _dot(jax.nn.gelu(h).astype(BF16), w2_ref[idx]) + b2_ref[idx]

    kc_ref[0] = one(0, kin_ref).astype(BF16)
    vct_ref[0] = one(1, vin_ref).T.astype(BF16)


def _compress(kcmp3, vcmp3, wa, wb, pe, w1, b1, w2, b2):
    B, nch, width = kcmp3.shape
    hid2 = wa.shape[2]
    full = lambda a: pl.BlockSpec(a.shape, lambda b: (0,) * a.ndim)
    in_blk = pl.BlockSpec((1, nch, width), lambda b: (b, 0, 0))
    out_blk = pl.BlockSpec((1, nch, KV_DIM), lambda b: (b, 0, 0))
    return pl.pallas_call(
        _compress_kernel,
        grid=(B,),
        in_specs=[in_blk, in_blk, full(wa), full(wb), full(pe), full(w1), full(b1), full(w2), full(b2)],
        out_specs=(out_blk, pl.BlockSpec((1, KV_DIM, nch), lambda b: (b, 0, 0))),
        out_shape=(jax.ShapeDtypeStruct((B, nch, KV_DIM), BF16),
                   jax.ShapeDtypeStruct((B, KV_DIM, nch), BF16)),
        compiler_params=_cparams(1),
        name="compress",
    )(kcmp3, vcmp3, wa, wb, pe, w1, b1, w2, b2)


def _group_q(qt_ref, hh, g_is0):
    qh = qt_ref[hh * HEAD_DIM:(hh + 1) * HEAD_DIM, :]
    z = jnp.zeros_like(qh)
    return jnp.concatenate([qh, z], axis=0) if g_is0 else jnp.concatenate([z, qh], axis=0)


def _cmpattn_kernel(n_cmp, qt_ref, kc_ref, vct_ref, gt_ref, ovt_ref, ocmpt_ref, selb_ref):
    i = pl.program_id(1)
    tq = qt_ref.shape[1]
    nc = kc_ref.shape[1]
    n_sel = ovt_ref.shape[0]
    kc = kc_ref[0]
    t_n = i * tq + lax.broadcasted_iota(jnp.int32, (nc, tq), 1)
    n_io = lax.broadcasted_iota(jnp.int32, (nc, tq), 0)
    blk_end = jnp.where(n_io < n_cmp, n_io * CMP_STRIDE + CMP_BLOCK - 1, jnp.int32(2 ** 30))
    vis = blk_end <= t_n
    j_io = lax.broadcasted_iota(jnp.int32, (n_sel, tq), 0)
    t_j = i * tq + lax.broadcasted_iota(jnp.int32, (n_sel, tq), 1)
    cur = t_j // SEL_BLOCK
    bonus = jnp.where(j_io == 0, FORCE_BONUS,
                      jnp.where(j_io == cur, FORCE_BONUS, jnp.where(j_io == cur - 1, FORCE_BONUS, 0.0)))
    valid = j_io * SEL_BLOCK <= t_j
    for g in range(N_GROUPS_KV):
        imp = jnp.zeros((n_sel, tq), F32)
        for hh in range(HEADS_PER_GROUP):
            h = g * HEADS_PER_GROUP + hh
            s = _dot(kc, _group_q(qt_ref, h, g == 0))
            s = jnp.where(vis, s, NEG_INF)
            m = jnp.max(s, axis=0, keepdims=True)
            p = jnp.where(vis, jnp.exp2(s - m), 0.0)
            l = jnp.sum(p, axis=0, keepdims=True)
            pr = (p * jnp.where(l > 0.0, 1.0 / l, 0.0)).astype(BF16)
            o = _dot(vct_ref[0, g * HEAD_DIM:(g + 1) * HEAD_DIM, :], pr)
            gate = gt_ref[g * GATE_ROWS + hh * N_BRANCH:g * GATE_ROWS + hh * N_BRANCH + 1, :]
            ocmpt_ref[h * HEAD_DIM:(h + 1) * HEAD_DIM, :] = (o * gate).astype(BF16)
            imp = imp + _dot(ovt_ref[...], pr)
        score = jnp.where(valid, imp + bonus, -FORCE_BONUS)
        cnt = jnp.zeros((n_sel, tq), F32)
        for jp in range(n_sel):
            r = score[jp:jp + 1, :]
            cnt = cnt + jnp.where(r > score, 1.0, jnp.where(r == score, jnp.where(j_io > jp, 1.0, 0.0), 0.0))
        selb_ref[0, g] = jnp.where(cnt < float(min(SEL_TOPK, n_sel)), 0.0, NEG_INF).astype(BF16)


def _cmp_attn(qt, kc, vct, gt, ovt, B, seq, n_cmp):
    T = qt.shape[1]
    tq = ATT_TILE
    nq = seq // tq
    n_sel = ovt.shape[0]
    nc = kc.shape[1]
    return pl.pallas_call(
        functools.partial(_cmpattn_kernel, n_cmp),
        grid=(B, nq),
        in_specs=[
            pl.BlockSpec((Q_DIM, tq), lambda b, i: (0, b * nq + i)),
            pl.BlockSpec((1, nc, KV_DIM), lambda b, i: (b, 0, 0)),
            pl.BlockSpec((1, KV_DIM, nc), lambda b, i: (b, 0, 0)),
            pl.BlockSpec((2 * GATE_ROWS, tq), lambda b, i: (0, b * nq + i)),
            pl.BlockSpec(ovt.shape, lambda b, i: (0, 0)),
        ],
        out_specs=(
            pl.BlockSpec((Q_DIM, tq), lambda b, i: (0, b * nq + i)),
            pl.BlockSpec((1, N_GROUPS_KV, n_sel, tq), lambda b, i: (b * nq + i, 0, 0, 0)),
        ),
        out_shape=(jax.ShapeDtypeStruct((Q_DIM, T), BF16),
                   jax.ShapeDtypeStruct((B * nq, N_GROUPS_KV, n_sel, tq), BF16)),
        compiler_params=_cparams(2),
        name="cmp_attn",
    )(qt, kc, vct, gt, ovt)


def _nsa_kernel(qt_ref, kslc_ref, kwin_ref, vslct_ref, vwint_ref, selb_ref, gt_ref, ocmpt_ref, causal_ref,
                window_ref, ot_ref, q_s, s_s, m_s, acc_s):
    i = pl.program_id(1)
    tq = qt_ref.shape[1]
    tile = ATT_TILE
    pair = 2 * tile
    n_sel = selb_ref.shape[2]
    groups = range(N_GROUPS_KV)
    gq = HEADS_PER_GROUP * HEAD_DIM

    for g in groups:
        for hh in range(HEADS_PER_GROUP):
            lanes = slice(hh * tq, (hh + 1) * tq)
            qh = qt_ref[g * gq + hh * HEAD_DIM:g * gq + (hh + 1) * HEAD_DIM, :]
            zero = jnp.zeros_like(qh)
            q_s[g, 0:KV_DIM, lanes] = jnp.concatenate([qh, zero] if g == 0 else [zero, qh], axis=0)
            q_s[g, KV_DIM:KV_DIM + n_sel, lanes] = selb_ref[0, g]
            q_s[g, KV_DIM + n_sel:2 * KV_DIM, lanes] = jnp.zeros((KV_DIM - n_sel, tq), BF16)

    def v_rows(ref, g, tiles):
        vt = jnp.concatenate([ref[j, g * HEAD_DIM:(g + 1) * HEAD_DIM, :] for j in tiles], axis=1)
        return jnp.concatenate([vt, jnp.ones((SUM_ROWS, vt.shape[1]), BF16)], axis=0)

    def sel_scores(g, jj):
        keys = kslc_ref[pl.ds(pl.multiple_of(jj * pair, pair), pair), :]
        return _dot(keys, q_s[g])

    def sel_update(g, s, jj):
        m_old = m_s[g]
        m_new = jnp.maximum(m_old, jnp.max(s, axis=0, keepdims=True))
        p = jnp.exp2(s - m_new).astype(BF16)
        acc_s[g] = jnp.exp2(m_old - m_new) * acc_s[g] + _dot(v_rows(vslct_ref, g, (2 * jj, 2 * jj + 1)), p)
        m_s[g] = m_new

    m_s[...] = jnp.full(m_s.shape, NEG_INF, F32)
    acc_s[...] = jnp.zeros(acc_s.shape, F32)
    n_full = i // 2
    for g in groups:
        s_s[g] = sel_scores(g, 0)

    def sel_body(jj, carry):
        for g in groups:
            s = s_s[g]
            s_s[g] = sel_scores(g, jj + 1)
            sel_update(g, s, jj)
        return carry

    lax.fori_loop(0, n_full, sel_body, 0)

    j0 = jnp.maximum(i - WINDOW // tile, 0)
    n_win = WINDOW + tile
    for g in groups:
        sel_update(g, s_s[g] + causal_ref[0], n_full)
        o_slc = acc_s[g, 0:HEAD_DIM, :] * (1.0 / acc_s[g, HEAD_DIM:HEAD_DIM + 1, :])
        s = _dot(kwin_ref[pl.ds(pl.multiple_of(j0 * tile, tile), n_win), :], q_s[g, 0:KV_DIM, :])
        s = s + window_ref[0]
        p = jnp.exp2(s - jnp.max(s, axis=0, keepdims=True)).astype(BF16)
        ow = _dot(v_rows(vwint_ref, g, (j0, j0 + 1, j0 + 2)), p)
        o_win = ow[0:HEAD_DIM, :] * (1.0 / ow[HEAD_DIM:HEAD_DIM + 1, :])
        for hh in range(HEADS_PER_GROUP):
            lanes = slice(hh * tq, (hh + 1) * tq)
            rows = slice(g * gq + hh * HEAD_DIM, g * gq + (hh + 1) * HEAD_DIM)
            gate = g * GATE_ROWS + hh * N_BRANCH
            o = (ocmpt_ref[rows, :].astype(F32) + gt_ref[gate + 1:gate + 2, :] * o_slc[:, lanes]
                 + gt_ref[gate + 2:gate + 3, :] * o_win[:, lanes])
            ot_ref[rows, :] = o.astype(BF16)


def _nsa_attn(qt, kslc, kwin, vslct, vwint, selb, gt, ocmpt, B, seq):
    T = qt.shape[1]
    tq = ATT_TILE
    nq = seq // tq
    ktiles = seq // ATT_TILE
    n_sel = selb.shape[2]
    wide = HEADS_PER_GROUP * tq
    n_win = WINDOW + ATT_TILE
    kmq = (lax.broadcasted_iota(jnp.int32, (1, n_win, wide), 1)
           - lax.broadcasted_iota(jnp.int32, (1, n_win, wide), 2) % tq)
    par = lax.broadcasted_iota(jnp.int32, (2, 1, 1), 0)
    causal = jnp.where(kmq[:, :2 * ATT_TILE] <= par * ATT_TILE, 0.0, NEG_INF).astype(F32)
    behind = lax.broadcasted_iota(jnp.int32, (n_win // ATT_TILE, 1, 1), 0) * ATT_TILE - kmq
    window = jnp.where((behind >= 0) & (behind < WINDOW), 0.0, NEG_INF).astype(F32)
    qblk = pl.BlockSpec((Q_DIM, tq), lambda b, i: (0, b * nq + i))
    vblk = pl.BlockSpec((ktiles, KV_DIM, ATT_TILE), lambda b, i: (b, 0, 0))
    return pl.pallas_call(
        _nsa_kernel,
        grid=(B, nq),
        in_specs=[
            qblk,
            pl.BlockSpec((seq, 2 * KV_DIM), lambda b, i: (b, 0)),
            pl.BlockSpec((seq, KV_DIM), lambda b, i: (b, 0)),
            vblk, vblk,
            pl.BlockSpec((1, N_GROUPS_KV, n_sel, tq), lambda b, i: (b * nq + i, 0, 0, 0)),
            pl.BlockSpec((N_GROUPS_KV * GATE_ROWS, tq), lambda b, i: (0, b * nq + i)),
            qblk,
            pl.BlockSpec((1, 2 * ATT_TILE, wide), lambda b, i: (i % 2, 0, 0)),
            pl.BlockSpec((1, n_win, wide), lambda b, i: (jnp.minimum(i, WINDOW // ATT_TILE), 0, 0)),
        ],
        out_specs=qblk,
        out_shape=jax.ShapeDtypeStruct((Q_DIM, T), BF16),
        scratch_shapes=[
            pltpu.VMEM((N_GROUPS_KV, 2 * KV_DIM, wide), BF16),
            pltpu.VMEM((N_GROUPS_KV, 2 * ATT_TILE, wide), F32),
            pltpu.VMEM((N_GROUPS_KV, 1, wide), F32),
            pltpu.VMEM((N_GROUPS_KV, HEAD_DIM + SUM_ROWS, wide), F32),
        ],
        compiler_params=_cparams(2),
        name="nsa_attn",
    )(qt, kslc, kwin, vslct, vwint, selb, gt, ocmpt, causal, window)


def _route(lt):
    row = lax.broadcasted_iota(jnp.int32, lt.shape, 0)
    big = 2 ** 20
    col = lambda f, v: f(v, axis=0, keepdims=True)
    grp = jnp.where((row >= N_EXPERTS) & (row < N_EXPERTS + N_EXPERT_GROUPS), lt, NEG_INF)
    gmax = col(jnp.max, grp)
    g_idx = col(jnp.min, jnp.where(grp == gmax, row, big)) - N_EXPERTS
    p_group = 1.0 / col(jnp.sum, jnp.exp(grp - gmax))
    own = jnp.where((row < N_EXPERTS) & (row // EXPERTS_PER_GROUP == g_idx), lt, NEG_INF)
    l1 = col(jnp.max, own)
    e1 = col(jnp.min, jnp.where(own == l1, row, big))
    rest = jnp.where(row == e1, NEG_INF, own)
    l2 = col(jnp.max, rest)
    e2 = col(jnp.min, jnp.where(rest == l2, row, big))
    r = jnp.exp(l2 - l1)
    c1 = p_group / (1.0 + r)
    return e1, e2, c1, c1 * r


R_SLOT1, R_SLOT2, R_C1, R_C2 = range(4)
ROUTE_ROWS = 40


def _merge_kernel(alpha, ot_ref, cu_ref, x_ref, wn_ref, wc_ref, wgm_ref, wo_ref, g1_ref, b1_ref,
                  wrt_ref, brt_ref, h1_ref, routet_ref, cnt_ref):
    tm, d = x_ref.shape
    iota = lambda shape, dim: lax.broadcasted_iota(jnp.int32, shape, dim)
    sq = (MOE_TILE, MOE_TILE)
    earlier_token = jnp.where(iota(sq, 0) < iota(sq, 1), 1.0, 0.0).astype(BF16)
    lower_expert = jnp.where(iota((LANES, LANES), 0) < iota((LANES, LANES), 1), 1.0, 0.0).astype(BF16)
    x = x_ref[...]
    xb = x.astype(BF16)
    y_nsa = _dot_tn(ot_ref[...], wn_ref[...])
    y_conv = _dot(cu_ref[...], wc_ref[...])
    merged = (_sigmoid(_dot(xb, wgm_ref[:, 0:d])) * y_nsa
              + _sigmoid(_dot(xb, wgm_ref[:, d:2 * d])) * y_conv)
    mix = _dot(merged.astype(BF16), wo_ref[...])
    h1 = _layer_norm(alpha * x + mix, g1_ref[...], b1_ref[...])
    h1_ref[...] = h1
    lt = (_dot_nt(wrt_ref[...], h1.astype(BF16)) + brt_ref[...])[0:ROUTE_ROWS, :]
    e1, e2, c1, c2 = _route(lt)
    row = iota((LANES, tm), 0)
    pick1 = row == e1
    pick2 = row == e2
    one = lambda m: jnp.where(m, 1.0, 0.0).astype(BF16)
    ones8 = jnp.ones((8, MOE_TILE), BF16)
    recs = []
    for sub in range(tm // MOE_TILE):
        cols = slice(sub * MOE_TILE, (sub + 1) * MOE_TILE)
        oh1, oh2 = one(pick1[:, cols]), one(pick2[:, cols])
        oh = oh1 + oh2
        before = _dot(oh, earlier_token)
        cnt = _dot_nt(ones8, oh)
        chunks = jnp.floor((cnt + (CHUNK - 1.0)) * (1.0 / CHUNK))
        run_start = (_dot(chunks.astype(BF16), lower_expert) * CHUNK).astype(BF16)
        slot1 = (_dot(run_start, oh1)[0:1, :]
                 + jnp.sum(jnp.where(pick1[:, cols], before, 0.0), axis=0, keepdims=True))
        slot2 = (_dot(run_start, oh2)[0:1, :]
                 + jnp.sum(jnp.where(pick2[:, cols], before, 0.0), axis=0, keepdims=True))
        recs.append(jnp.concatenate([slot1, slot2, c1[:, cols], c2[:, cols],
                                     jnp.zeros((4, MOE_TILE), F32)], axis=0))
        cnt_ref[sub] = cnt
    routet_ref[...] = jnp.concatenate(recs, axis=1)


def _merge(ot, cu, x2, wn, wc, wgm, wo, g1, b1, wrt, brt, alpha):
    T, D = x2.shape
    tm = brt.shape[1]
    sub = tm // MOE_TILE
    full = lambda a: pl.BlockSpec(a.shape, lambda i: (0,) * a.ndim)
    return pl.pallas_call(
        functools.partial(_merge_kernel, alpha),
        grid=(T // tm,),
        in_specs=[
            pl.BlockSpec((Q_DIM, tm), lambda i: (0, i)),
            pl.BlockSpec((tm, CONV_DIM), lambda i: (i, 0)),
            pl.BlockSpec((tm, D), lambda i: (i, 0)),
            full(wn), full(wc), full(wgm), full(wo), full(g1), full(b1), full(wrt), full(brt),
        ],
        out_specs=(pl.BlockSpec((tm, D), lambda i: (i, 0)),
                   pl.BlockSpec((8, tm), lambda i: (0, i)),
                   pl.BlockSpec((sub, 8, LANES), lambda i: (i, 0, 0))),
        out_shape=(jax.ShapeDtypeStruct((T, D), F32),
                   jax.ShapeDtypeStruct((8, T), F32),
                   jax.ShapeDtypeStruct((T // MOE_TILE, 8, LANES), F32)),
        compiler_params=_cparams(1),
        name="merge",
    )(ot, cu, x2, wn, wc, wgm, wo, g1, b1, wrt, brt)


def _chunk_rows(q):
    return pl.ds(pl.multiple_of(q * CHUNK, CHUNK), CHUNK)


def _wait_chunks(n, copy_of_rows):
    size = MAX_CHUNKS
    while size >= 1:
        @pl.when((n & size) != 0)
        def _(size=size):
            copy_of_rows(size * CHUNK).wait()
        size //= 2


def _dispatch_kernel(tab_ref, tab_m1_ref, tab_m2_ref, tails_ref, nused_ref, h1_ref, routet_ref, xs_ref,
                     buf, zero_s, sem):
    i = pl.program_id(0)
    last = pl.num_programs(0) - 1
    tm = h1_ref.shape[0]
    slot = i % 2

    def drain(tab, slot_):
        _wait_chunks(tab[0, 0, TAB_COUNT],
                     lambda rows: pltpu.make_async_copy(buf.at[slot_, pl.ds(0, rows)], xs_ref.at[pl.ds(0, rows)],
                                                        sem.at[slot_]))

    s1 = routet_ref[R_SLOT1:R_SLOT1 + 1, :].astype(jnp.int32)
    s2 = routet_ref[R_SLOT2:R_SLOT2 + 1, :].astype(jnp.int32)
    r_io = lax.broadcasted_iota(jnp.int32, (LOCAL_ROWS, tm), 0)
    perm = jnp.where(r_io == s1, 1.0, jnp.where(r_io == s2, 1.0, 0.0)).astype(BF16)
    srt = _dot(perm, h1_ref[...].astype(BF16)).astype(BF16)

    @pl.when(i >= 2)
    def _():
        drain(tab_m2_ref, slot)

    buf[slot] = srt

    def start(q, c):
        dst = pl.multiple_of(tab_ref[0, 0, q], CHUNK)
        pltpu.make_async_copy(buf.at[slot, _chunk_rows(q)], xs_ref.at[pl.ds(dst, CHUNK)], sem.at[slot]).start()
        return c

    lax.fori_loop(0, tab_ref[0, 0, TAB_COUNT], start, 0)

    @pl.when(i == last)
    def _():
        drain(tab_ref, slot)

    @pl.when(jnp.logical_and(i == last, i >= 1))
    def _():
        drain(tab_m1_ref, 1 - slot)

    @pl.when(i == last)
    def _():
        zero_s[...] = jnp.zeros(zero_s.shape, zero_s.dtype)

        def tail_copy(q):
            dst = pl.multiple_of(jnp.maximum(tails_ref[q], 0), CHUNK)
            return pltpu.make_async_copy(zero_s.at[pl.ds(0, CHUNK)], xs_ref.at[pl.ds(dst, CHUNK)], sem.at[2])

        def tile_copy(j):
            dst = pl.multiple_of(j * EXPERT_TILE, EXPERT_TILE)
            return pltpu.make_async_copy(zero_s, xs_ref.at[pl.ds(dst, EXPERT_TILE)], sem.at[2])

        def start_tail(q, c):
            @pl.when(tails_ref[q] >= 0)
            def _():
                tail_copy(q).start()
            return c

        def wait_tail(q, c):
            @pl.when(tails_ref[q] >= 0)
            def _():
                tail_copy(q).wait()
            return c

        def start_tile(j, c):
            tile_copy(j).start()
            return c

        def wait_tile(j, c):
            tile_copy(j).wait()
            return c

        n_tiles = xs_ref.shape[0] // EXPERT_TILE
        lax.fori_loop(0, tails_ref.shape[0], start_tail, 0)
        lax.fori_loop(nused_ref[0], n_tiles, start_tile, 0)
        lax.fori_loop(0, tails_ref.shape[0], wait_tail, 0)
        lax.fori_loop(nused_ref[0], n_tiles, wait_tile, 0)


def _dispatch(tab, tails, n_used, h1, route_t, n_rows):
    T, D = h1.shape
    tm = MOE_TILE
    tab_blk = lambda back: pl.BlockSpec((1, 1, LANES), lambda i: (jnp.maximum(i - back, 0), 0, 0),
                                        memory_space=pltpu.SMEM)
    return pl.pallas_call(
        _dispatch_kernel,
        grid=(T // tm,),
        in_specs=[
            tab_blk(0), tab_blk(1), tab_blk(2),
            pl.BlockSpec(memory_space=pltpu.SMEM),
            pl.BlockSpec(memory_space=pltpu.SMEM),
            pl.BlockSpec((tm, D), lambda i: (i, 0)),
            pl.BlockSpec((8, tm), lambda i: (0, i)),
        ],
        out_specs=pl.BlockSpec(memory_space=pl.ANY),
        out_shape=jax.ShapeDtypeStruct((n_rows, D), BF16),
        scratch_shapes=[pltpu.VMEM((2, LOCAL_ROWS, D), BF16), pltpu.VMEM((EXPERT_TILE, D), BF16),
                        pltpu.SemaphoreType.DMA((3,))],
        compiler_params=_cparams(1),
        name="moe_dispatch",
    )(tab, tab, tab, tails, n_used, h1, route_t)


def _experts_kernel(te_ref, nu_ref, xs_ref, wg_ref, wu_ref, wd_ref, ys_ref):
    del te_ref
    j = pl.program_id(0)

    @pl.when(j < nu_ref[0])
    def _():
        xb = xs_ref[...]
        hg = _dot(xb, wg_ref[0].astype(BF16))
        h = hg * _sigmoid(hg) * _dot(xb, wu_ref[0].astype(BF16))
        y = _dot(h.astype(BF16), wd_ref[0].astype(BF16))
        ys_ref[...] = y.astype(BF16)

    @pl.when(j >= nu_ref[0])
    def _():
        ys_ref[...] = jnp.zeros(ys_ref.shape, ys_ref.dtype)


def _experts(tile_expert, n_used, xs, wg, wu, wd):
    n_rows, D = xs.shape
    hid = wg.shape[2]
    tm = EXPERT_TILE
    used = lambda j, te, nu: (jnp.minimum(j, nu[0] - 1), 0)
    grid_spec = pltpu.PrefetchScalarGridSpec(
        num_scalar_prefetch=2,
        grid=(n_rows // tm,),
        in_specs=[
            pl.BlockSpec((tm, D), used),
            pl.BlockSpec((1, D, hid), lambda j, te, nu: (te[j], 0, 0)),
            pl.BlockSpec((1, D, hid), lambda j, te, nu: (te[j], 0, 0)),
            pl.BlockSpec((1, hid, D), lambda j, te, nu: (te[j], 0, 0)),
        ],
        out_specs=pl.BlockSpec((tm, D), lambda j, te, nu: (j, 0)),
    )
    return pl.pallas_call(
        _experts_kernel,
        grid_spec=grid_spec,
        out_shape=jax.ShapeDtypeStruct((n_rows, D), BF16),
        compiler_params=_cparams(1),
        name="moe_experts",
    )(tile_expert, n_used, xs, wg, wu, wd)


def _combine_kernel(alpha, tab_ref, tab_next_ref, routet_ref, h1_ref, p_ref, pproj_ref, pgw_ref,
                    pgb_ref, g2_ref, b2_ref, ys_ref, out_ref, buf, sem):
    i = pl.program_id(0)
    tm = h1_ref.shape[0]
    slot = i % 2

    def gather(tab, slot_):
        def start(q, c):
            src = pl.multiple_of(tab[0, 0, q], CHUNK)
            pltpu.make_async_copy(ys_ref.at[pl.ds(src, CHUNK)], buf.at[slot_, _chunk_rows(q)],
                                  sem.at[slot_]).start()
            return c
        lax.fori_loop(0, tab[0, 0, TAB_COUNT], start, 0)

    @pl.when(i == 0)
    def _():
        buf[...] = jnp.zeros(buf.shape, buf.dtype)
        gather(tab_ref, 0)

    @pl.when(i + 1 < pl.num_programs(0))
    def _():
        gather(tab_next_ref, 1 - slot)

    h1 = h1_ref[...]
    ple = (_sigmoid(_dot(h1.astype(BF16), pgw_ref[...]) + pgb_ref[...])
           * _dot(p_ref[...].astype(BF16), pproj_ref[...]))

    _wait_chunks(tab_ref[0, 0, TAB_COUNT],
                 lambda rows: pltpu.make_async_copy(ys_ref.at[pl.ds(0, rows)], buf.at[slot, pl.ds(0, rows)],
                                                    sem.at[slot]))

    s1 = routet_ref[R_SLOT1:R_SLOT1 + 1, :].astype(jnp.int32)
    s2 = routet_ref[R_SLOT2:R_SLOT2 + 1, :].astype(jnp.int32)
    c1 = routet_ref[R_C1:R_C1 + 1, :]
    c2 = routet_ref[R_C2:R_C2 + 1, :]
    r_io = lax.broadcasted_iota(jnp.int32, (LOCAL_ROWS, tm), 0)
    weights = jnp.where(r_io == s1, c1, jnp.where(r_io == s2, c2, 0.0)).astype(BF16)
    ffn = _dot_tn(weights, buf[slot])
    out_ref[...] = _layer_norm(alpha * h1 + ffn + ple, g2_ref[...], b2_ref[...])


def _combine(tab, route_t, h1, p2, pproj, pgw, pgb, g2, b2, ys, alpha):
    T, D = h1.shape
    tm = MOE_TILE
    nt = T // tm
    full = lambda a: pl.BlockSpec(a.shape, lambda i: (0,) * a.ndim)
    return pl.pallas_call(
        functools.partial(_combine_kernel, alpha),
        grid=(nt,),
        in_specs=[
            pl.BlockSpec((1, 1, LANES), lambda i: (i, 0, 0), memory_space=pltpu.SMEM),
            pl.BlockSpec((1, 1, LANES), lambda i: (jnp.minimum(i + 1, nt - 1), 0, 0), memory_space=pltpu.SMEM),
            pl.BlockSpec((8, tm), lambda i: (0, i)),
            pl.BlockSpec((tm, D), lambda i: (i, 0)),
            pl.BlockSpec((tm, p2.shape[1]), lambda i: (i, 0)),
            full(pproj), full(pgw), full(pgb), full(g2), full(b2),
            pl.BlockSpec(memory_space=pl.ANY),
        ],
        out_specs=pl.BlockSpec((tm, D), lambda i: (i, 0)),
        out_shape=jax.ShapeDtypeStruct((T, D), F32),
        scratch_shapes=[pltpu.VMEM((2, LOCAL_ROWS, D), BF16), pltpu.SemaphoreType.DMA((2,))],
        compiler_params=_cparams(1),
        name="moe_combine",
    )(tab, tab, route_t, h1, p2, pproj, pgw, pgb, g2, b2, ys)


def _moe_plan(cnt_tiles, n_expert_tiles):
    cnt = cnt_tiles[:, 0, :N_EXPERTS].astype(jnp.int32)
    cnt8 = (cnt + CHUNK - 1) // CHUNK * CHUNK
    lend = jnp.cumsum(cnt8, axis=1)
    lstart = lend - cnt8
    gend = jnp.cumsum(cnt8, axis=0)
    region = (gend[-1] + EXPERT_TILE - 1) // EXPERT_TILE * EXPERT_TILE
    oend = jnp.cumsum(region)
    gstart = (oend - region)[None, :] + gend - cnt8
    q8 = jnp.arange(MAX_CHUNKS, dtype=jnp.int32) * CHUNK
    eq = jnp.minimum(jnp.sum((lend[:, None, :] <= q8[None, :, None]).astype(jnp.int32), axis=-1), N_EXPERTS - 1)
    shift = jnp.sum(jnp.where(eq[:, :, None] == jnp.arange(N_EXPERTS)[None, None, :],
                              (gstart - lstart)[:, None, :], 0), axis=-1)
    dstq = shift + q8[None, :]
    nt = cnt.shape[0]
    tab = jnp.zeros((nt, LANES), jnp.int32).at[:, :MAX_CHUNKS].set(dstq).at[:, TAB_COUNT].set(lend[:, -1] // CHUNK)
    tile_row = jnp.arange(n_expert_tiles, dtype=jnp.int32) * EXPERT_TILE
    tile_expert = jnp.minimum(jnp.sum((oend[None, :] <= tile_row[:, None]).astype(jnp.int32), axis=1),
                              N_EXPERTS - 1)
    n_used = (oend[-1] // EXPERT_TILE).reshape(1)
    c = jnp.arange(EXPERT_TILE // CHUNK, dtype=jnp.int32)[None, :] * CHUNK
    tail_start = (oend - region + gend[-1])[:, None] + c
    tails = jnp.where(tail_start < oend[:, None], tail_start, -1).reshape(-1)
    return tab.reshape(nt, 1, LANES), tails, tile_expert, n_used


def _split_w_in(w_in, D):
    sizes = [Q_DIM] + [KV_DIM] * 6 + [N_HEADS * N_BRANCH] + [CONV_DIM] * 3 + [D] * 2
    offs = np.concatenate([[0], np.cumsum(sizes)])
    names = ["q", "k_cmp", "v_cmp", "k_slc", "v_slc", "k_win", "v_win", "g_nsa",
             "conv_b", "conv_c", "conv_h", "g_m_nsa", "g_m_conv"]
    return {n: w_in[:, int(offs[k]):int(offs[k + 1])] for k, n in enumerate(names)}


def _layer(x2, p2, B, seq, depth, w_in, cmp_pe, cmp_w1, cmp_b1, cmp_w2, cmp_b2, conv_w, w_nsa_out,
           w_conv_out, w_o, ln1_g, ln1_b, rg_w, rg_b, re_w, re_b, e_wg, e_wu, e_wd, ple_proj,
           ple_gate_w, ple_gate_b, ln2_g, ln2_b):
    T, D = x2.shape
    alpha = (2.0 * depth) ** 0.25
    w = _split_w_in(w_in, D)
    wtok = jnp.concatenate([w["k_cmp"], w["v_cmp"], w["k_slc"], w["k_win"],
                            w["conv_b"], w["conv_c"], w["conv_h"]], axis=1).astype(BF16)
    gcols = w["g_nsa"].reshape(D, N_GROUPS_KV, HEADS_PER_GROUP * N_BRANCH)
    gcols = jnp.pad(gcols, ((0, 0), (0, 0), (0, GATE_ROWS - HEADS_PER_GROUP * N_BRANCH)))
    wt = jnp.concatenate([w["q"], gcols.reshape(D, N_GROUPS_KV * GATE_ROWS), w["v_slc"], w["v_win"]],
                         axis=1).T.astype(BF16)
    kcmp, vcmp, kslc, kwin, cu, qt, gt, vslct, vwint = _in_proj(x2, wtok, wt, conv_w, seq)

    half = CMP_BLOCK // 2
    n_chunks = seq // CMP_STRIDE
    n_cmp = (seq - CMP_BLOCK) // CMP_STRIDE + 1
    hidden = cmp_w1.shape[-1]
    eye = jnp.eye(N_GROUPS_KV, dtype=F32)
    w1r = cmp_w1.reshape(2, CMP_BLOCK, HEAD_DIM, hidden)
    expand = lambda m: jnp.einsum("ildh,gk->ilgdkh", m, eye).reshape(
        2, half * KV_DIM, N_GROUPS_KV * hidden).astype(BF16)
    wa, wb = expand(w1r[:, :half]), expand(w1r[:, half:])
    pe = jnp.broadcast_to(cmp_pe.reshape(2, 1, CMP_BLOCK * HEAD_DIM), (2, 8, CMP_BLOCK * HEAD_DIM)).astype(BF16)
    b1t = jnp.tile(cmp_b1.reshape(2, 1, hidden), (1, 1, N_GROUPS_KV))
    w2b = jnp.einsum("ihd,gk->ighkd", cmp_w2, eye).reshape(2, N_GROUPS_KV * hidden, KV_DIM).astype(BF16)
    b2t = jnp.tile(cmp_b2.reshape(2, 1, HEAD_DIM), (1, 1, N_GROUPS_KV))
    kc, vct = _compress(kcmp.reshape(B, n_chunks, CMP_STRIDE * KV_DIM),
                        vcmp.reshape(B, n_chunks, CMP_STRIDE * KV_DIM),
                        wa, wb, pe, cmp_w1.astype(BF16), b1t, w2b, b2t)

    n_sel = seq // SEL_BLOCK
    c_start = np.arange(n_chunks) * CMP_STRIDE
    s_start = np.arange(n_sel) * SEL_BLOCK
    overlap = ((c_start[None, :] <= s_start[:, None] + SEL_BLOCK - 1)
               & (c_start[None, :] + CMP_BLOCK - 1 >= s_start[:, None])).astype(np.float32)
    ocmpt, selb = _cmp_attn(qt, kc, vct, gt, jnp.asarray(overlap, BF16), B, seq, n_cmp)
    ot = _nsa_attn(qt, kslc, kwin, vslct, vwint, selb, gt, ocmpt, B, seq)

    wgm = jnp.concatenate([w["g_m_nsa"], w["g_m_conv"]], axis=1).astype(BF16)
    merge_tile = 512
    wrt = jnp.pad(jnp.concatenate([re_w, rg_w], axis=1).T, ((0, LANES - N_EXPERTS - N_EXPERT_GROUPS), (0, 0)))
    brt = jnp.pad(jnp.concatenate([re_b, rg_b]), (0, LANES - N_EXPERTS - N_EXPERT_GROUPS))
    brt = jnp.broadcast_to(brt[:, None], (LANES, merge_tile))
    h1, route_t, cnt_tiles = _merge(ot, cu, x2, w_nsa_out.astype(BF16), w_conv_out.astype(BF16), wgm,
                                    w_o.astype(BF16), ln1_g.reshape(1, D), ln1_b.reshape(1, D),
                                    wrt.astype(BF16), brt, alpha)
    n_tok_tiles = T // MOE_TILE
    max_rows = 2 * T + n_tok_tiles * N_EXPERTS * (CHUNK - 1) + N_EXPERTS * (EXPERT_TILE - 1)
    n_expert_tiles = -(-max_rows // EXPERT_TILE)
    tab, tails, tile_expert, n_used = _moe_plan(cnt_tiles, n_expert_tiles)
    xs = _dispatch(tab, tails, n_used, h1, route_t, n_expert_tiles * EXPERT_TILE)
    ys = _experts(tile_expert, n_used, xs, e_wg, e_wu, e_wd)
    return _combine(tab, route_t, h1, p2, ple_proj.astype(BF16), ple_gate_w.astype(BF16),
                    ple_gate_b.reshape(1, D), ln2_g.reshape(1, D), ln2_b.reshape(1, D), ys, alpha)


def kernel(x, p, w_in, cmp_pe, cmp_w1, cmp_b1, cmp_w2, cmp_b2, conv_w, w_nsa_out, w_conv_out, w_o, ln1_g, ln1_b, router_group_w, router_group_b, router_expert_w, router_expert_b, expert_w_gate, expert_w_up, expert_w_down, ple_proj, ple_gate_w, ple_gate_b, ln2_g, ln2_b):
    B, seq, D = x.shape
    depth = w_in.shape[0]
    x2 = x.reshape(B * seq, D)
    for i in range(depth):
        x2 = _layer(x2, p[i].reshape(B * seq, -1), B, seq, depth, w_in[i], cmp_pe[i], cmp_w1[i], cmp_b1[i],
                    cmp_w2[i], cmp_b2[i], conv_w[i], w_nsa_out[i], w_conv_out[i], w_o[i], ln1_g[i], ln1_b[i],
                    router_group_w[i], router_group_b[i], router_expert_w[i], router_expert_b[i],
                    expert_w_gate[i], expert_w_up[i], expert_w_down[i], ple_proj[i], ple_gate_w[i],
                    ple_gate_b[i], ln2_g[i], ln2_b[i])
    return x2.reshape(B, seq, D)
```

```python
import functools

import jax
import jax.numpy as jnp
import numpy as np
from jax import lax
from jax.experimental import pallas as pl
from jax.experimental.pallas import tpu as pltpu

F32 = jnp.float32
BF16 = jnp.bfloat16

N_HEADS = 8
N_GROUPS_KV = 2
HEADS_PER_GROUP = N_HEADS // N_GROUPS_KV
HEAD_DIM = 64
Q_DIM = N_HEADS * HEAD_DIM
KV_DIM = N_GROUPS_KV * HEAD_DIM
N_BRANCH = 3
CMP_BLOCK = 32
CMP_STRIDE = 16
SEL_BLOCK = 64
SEL_TOPK = 16
WINDOW = 512
CONV_DIM = 512
N_EXPERT_GROUPS = 4
EXPERTS_PER_GROUP = 8
N_EXPERTS = N_EXPERT_GROUPS * EXPERTS_PER_GROUP
ATTN_SCALE = HEAD_DIM ** -0.5
LOG2_E = 1.4426950408889634
FORCE_BONUS = 1e4
NEG_INF = -1e30
LN_EPS = 1e-5

LANES = 128
ATT_TILE = 256
GATE_ROWS = 16
SUM_ROWS = 16
VMEM_LIMIT = 48 * 1024 * 1024

MOE_TILE = 256
EXPERT_TILE = 1024
CHUNK = 16
LOCAL_ROWS = -(-(2 * MOE_TILE + N_EXPERTS * (CHUNK - 1)) // ATT_TILE) * ATT_TILE
MAX_CHUNKS = LOCAL_ROWS // CHUNK
TAB_COUNT = LANES - 1


def _cparams(n_axes):
    return pltpu.CompilerParams(dimension_semantics=("arbitrary",) * n_axes,
                                vmem_limit_bytes=VMEM_LIMIT)


def _dot(a, b):
    return jnp.dot(a, b, preferred_element_type=F32)


def _dot_nt(a, b):
    return lax.dot_general(a, b, (((1,), (1,)), ((), ())), preferred_element_type=F32)


def _dot_tn(a, b):
    return lax.dot_general(a, b, (((0,), (0,)), ((), ())), preferred_element_type=F32)


def _sigmoid(v):
    return 1.0 / (1.0 + jnp.exp(-v))


def _layer_norm(v, g, b):
    mu = jnp.mean(v, axis=-1, keepdims=True)
    d = v - mu
    var = jnp.mean(d * d, axis=-1, keepdims=True)
    return d * lax.rsqrt(var + LN_EPS) * g + b


def _inproj_kernel(seq, x_ref, xprev_ref, wtok_ref, wt_ref, convw_ref,
                   kcmp_ref, vcmp_ref, kslc_ref, kwin_ref, cu_ref, qt_ref, gt_ref, vslct_ref, vwint_ref, cmp_s):
    i = pl.program_id(0)
    tm = x_ref.shape[0]
    xb = x_ref[...].astype(BF16)

    c0 = 4 * KV_DIM
    kvf = _dot(xb, wtok_ref[:, 0:c0])
    kv = kvf.astype(BF16)
    cmp_s[0] = kvf[:, 0:KV_DIM]
    cmp_s[1] = kvf[:, KV_DIM:2 * KV_DIM]
    for l in range(CMP_STRIDE):
        rows = pl.ds(l, tm // CMP_STRIDE, stride=CMP_STRIDE)
        kcmp_ref[:, l * KV_DIM:(l + 1) * KV_DIM] = cmp_s[0, rows, :].astype(BF16)
        vcmp_ref[:, l * KV_DIM:(l + 1) * KV_DIM] = cmp_s[1, rows, :].astype(BF16)
    kwin_ref[...] = kv[:, 3 * KV_DIM:4 * KV_DIM]
    n_sel = seq // SEL_BLOCK
    blk = ((i * tm + lax.broadcasted_iota(jnp.int32, (tm, KV_DIM), 0)) // SEL_BLOCK) % n_sel
    onehot = jnp.where(lax.broadcasted_iota(jnp.int32, (tm, KV_DIM), 1) == blk, 1.0, 0.0).astype(BF16)
    kslc_ref[...] = jnp.concatenate([kv[:, 2 * KV_DIM:3 * KV_DIM], onehot], axis=1)
    conv = _dot(xb, wtok_ref[:, c0:c0 + 3 * CONV_DIM])
    cb = conv[:, 0:CONV_DIM]
    u = conv[:, CONV_DIM:2 * CONV_DIM] * conv[:, 2 * CONV_DIM:3 * CONV_DIM]
    xpb = xprev_ref[...].astype(BF16)
    up = (_dot(xpb, wtok_ref[:, c0 + CONV_DIM:c0 + 2 * CONV_DIM])
          * _dot(xpb, wtok_ref[:, c0 + 2 * CONV_DIM:c0 + 3 * CONV_DIM]))
    up = jnp.where(i % (seq // tm) == 0, 0.0, up)
    row = lax.broadcasted_iota(jnp.int32, (tm, CONV_DIM), 0)
    u1 = jnp.where(row == 0, up[7:8, :], pltpu.roll(u, 1, 0))
    u2 = jnp.where(row == 0, up[6:7, :], jnp.where(row == 1, up[7:8, :], pltpu.roll(u, 2, 0)))
    w = convw_ref[...]
    uc = w[0:1, :] * u2 + w[1:2, :] * u1 + w[2:3, :] * u
    cu_ref[...] = (cb * uc).astype(BF16)

    zt = _dot_nt(wt_ref[...], xb)
    qt_ref[...] = (zt[0:Q_DIM, :] * (ATTN_SCALE * LOG2_E)).astype(BF16)
    r0 = Q_DIM
    gt_ref[...] = _sigmoid(zt[r0:r0 + 2 * GATE_ROWS, :])
    r0 += 2 * GATE_ROWS
    vs = zt[r0:r0 + KV_DIM, :].astype(BF16)
    vw = zt[r0 + KV_DIM:r0 + 2 * KV_DIM, :].astype(BF16)
    for c in range(tm // ATT_TILE):
        vslct_ref[c] = vs[:, c * ATT_TILE:(c + 1) * ATT_TILE]
        vwint_ref[c] = vw[:, c * ATT_TILE:(c + 1) * ATT_TILE]


def _in_proj(x2, wtok, wt, conv_w, seq):
    T, D = x2.shape
    tm = 1024
    nt = T // tm
    n_tok = wtok.shape[1]
    n_t = wt.shape[0]
    row_blk = lambda n: pl.BlockSpec((tm, n), lambda i: (i, 0))
    out_shape = (
        jax.ShapeDtypeStruct((T // CMP_STRIDE, CMP_STRIDE * KV_DIM), BF16),
        jax.ShapeDtypeStruct((T // CMP_STRIDE, CMP_STRIDE * KV_DIM), BF16),
        jax.ShapeDtypeStruct((T, 2 * KV_DIM), BF16),
        jax.ShapeDtypeStruct((T, KV_DIM), BF16),
        jax.ShapeDtypeStruct((T, CONV_DIM), BF16),
        jax.ShapeDtypeStruct((Q_DIM, T), BF16),
        jax.ShapeDtypeStruct((2 * GATE_ROWS, T), F32),
        jax.ShapeDtypeStruct((T // ATT_TILE, KV_DIM, ATT_TILE), BF16),
        jax.ShapeDtypeStruct((T // ATT_TILE, KV_DIM, ATT_TILE), BF16),
    )
    vt_blk = pl.BlockSpec((tm // ATT_TILE, KV_DIM, ATT_TILE), lambda i: (i, 0, 0))
    chunk_blk = pl.BlockSpec((tm // CMP_STRIDE, CMP_STRIDE * KV_DIM), lambda i: (i, 0))
    return pl.pallas_call(
        functools.partial(_inproj_kernel, seq),
        grid=(nt,),
        in_specs=[
            row_blk(D),
            pl.BlockSpec((8, D), lambda i: (jnp.maximum(i * (tm // 8) - 1, 0), 0)),
            pl.BlockSpec((D, n_tok), lambda i: (0, 0)),
            pl.BlockSpec((n_t, D), lambda i: (0, 0)),
            pl.BlockSpec((3, CONV_DIM), lambda i: (0, 0)),
        ],
        out_specs=(
            chunk_blk, chunk_blk, row_blk(2 * KV_DIM), row_blk(KV_DIM), row_blk(CONV_DIM),
            pl.BlockSpec((Q_DIM, tm), lambda i: (0, i)),
            pl.BlockSpec((2 * GATE_ROWS, tm), lambda i: (0, i)),
            vt_blk, vt_blk,
        ),
        out_shape=out_shape,
        scratch_shapes=[pltpu.VMEM((2, tm, KV_DIM), F32)],
        compiler_params=_cparams(1),
        name="in_proj",
    )(x2, x2, wtok, wt, conv_w)


def _compress_kernel(kin_ref, vin_ref, wa_ref, wb_ref, pe_ref, w1_ref, b1_ref, w2_ref, b2_ref,
                     kc_ref, vct_ref):
    def one(idx, in_ref):
        c = in_ref[0]
        a = _dot(c, wa_ref[idx])
        b = _dot(c, wb_ref[idx])
        n = b.shape[0]
        peb = _dot(pe_ref[idx], w1_ref[idx])[0:1, :]
        bias = jnp.concatenate([peb, peb], axis=1) + b1_ref[idx]
        h = a + pltpu.roll(b, n - 1, 0) + bias
        return _dot(jax.nn.gelu(h).astype(BF16), w2_ref[idx]) + b2_ref[idx]

    kc_ref[0] = one(0, kin_ref).astype(BF16)
    vct_ref[0] = one(1, vin_ref).T.astype(BF16)


def _compress(kcmp3, vcmp3, wa, wb, pe, w1, b1, w2, b2):
    B, nch, width = kcmp3.shape
    hid2 = wa.shape[2]
    full = lambda a: pl.BlockSpec(a.shape, lambda b: (0,) * a.ndim)
    in_blk = pl.BlockSpec((1, nch, width), lambda b: (b, 0, 0))
    out_blk = pl.BlockSpec((1, nch, KV_DIM), lambda b: (b, 0, 0))
    return pl.pallas_call(
        _compress_kernel,
        grid=(B,),
        in_specs=[in_blk, in_blk, full(wa), full(wb), full(pe), full(w1), full(b1), full(w2), full(b2)],
        out_specs=(out_blk, pl.BlockSpec((1, KV_DIM, nch), lambda b: (b, 0, 0))),
        out_shape=(jax.ShapeDtypeStruct((B, nch, KV_DIM), BF16),
                   jax.ShapeDtypeStruct((B, KV_DIM, nch), BF16)),
        compiler_params=_cparams(1),
        name="compress",
    )(kcmp3, vcmp3, wa, wb, pe, w1, b1, w2, b2)


def _group_q(qt_ref, hh, g_is0):
    qh = qt_ref[hh * HEAD_DIM:(hh + 1) * HEAD_DIM, :]
    z = jnp.zeros_like(qh)
    return jnp.concatenate([qh, z], axis=0) if g_is0 else jnp.concatenate([z, qh], axis=0)


def _cmpattn_kernel(n_cmp, qt_ref, kc_ref, vct_ref, gt_ref, ovt_ref, ocmpt_ref, selb_ref, score_s):
    i = pl.program_id(1)
    tq = qt_ref.shape[1]
    nc = kc_ref.shape[1]
    n_sel = ovt_ref.shape[0]
    kc = kc_ref[0]
    t_n = i * tq + lax.broadcasted_iota(jnp.int32, (nc, tq), 1)
    n_io = lax.broadcasted_iota(jnp.int32, (nc, tq), 0)
    blk_end = jnp.where(n_io < n_cmp, n_io * CMP_STRIDE + CMP_BLOCK - 1, jnp.int32(2 ** 30))
    vis = blk_end <= t_n
    j_io = lax.broadcasted_iota(jnp.int32, (n_sel, tq), 0)
    t_j = i * tq + lax.broadcasted_iota(jnp.int32, (n_sel, tq), 1)
    cur = t_j // SEL_BLOCK
    bonus = jnp.where(j_io == 0, FORCE_BONUS,
                      jnp.where(j_io == cur, FORCE_BONUS, jnp.where(j_io == cur - 1, FORCE_BONUS, 0.0)))
    valid = j_io * SEL_BLOCK <= t_j
    for g in range(N_GROUPS_KV):
        imp = jnp.zeros((n_sel, tq), F32)
        for hh in range(HEADS_PER_GROUP):
            h = g * HEADS_PER_GROUP + hh
            s = _dot(kc, _group_q(qt_ref, h, g == 0))
            s = jnp.where(vis, s, NEG_INF)
            m = jnp.max(s, axis=0, keepdims=True)
            p = jnp.where(vis, jnp.exp2(s - m), 0.0)
            l = jnp.sum(p, axis=0, keepdims=True)
            pr = (p * jnp.where(l > 0.0, 1.0 / l, 0.0)).astype(BF16)
            o = _dot(vct_ref[0, g * HEAD_DIM:(g + 1) * HEAD_DIM, :], pr)
            gate = gt_ref[g * GATE_ROWS + hh * N_BRANCH:g * GATE_ROWS + hh * N_BRANCH + 1, :]
            ocmpt_ref[h * HEAD_DIM:(h + 1) * HEAD_DIM, :] = (o * gate).astype(BF16)
            imp = imp + _dot(ovt_ref[...], pr)
        score = jnp.where(valid, imp + bonus, -FORCE_BONUS)
        score_s[...] = score

        def outrank(jp, cnt):
            r = score_s[pl.ds(jp, 1), :]
            return cnt + jnp.where(r > score, 1.0, jnp.where(r == score, jnp.where(j_io > jp, 1.0, 0.0), 0.0))

        n_live = jnp.minimum((i + 1) * (tq // SEL_BLOCK), n_sel)
        cnt = lax.fori_loop(0, n_live, outrank, jnp.zeros((n_sel, tq), F32))
        selb_ref[0, g] = jnp.where(cnt < float(min(SEL_TOPK, n_sel)), 0.0, NEG_INF).astype(BF16)


def _cmp_attn(qt, kc, vct, gt, ovt, B, seq, n_cmp):
    T = qt.shape[1]
    tq = ATT_TILE
    nq = seq // tq
    n_sel = ovt.shape[0]
    nc = kc.shape[1]
    return pl.pallas_call(
        functools.partial(_cmpattn_kernel, n_cmp),
        grid=(B, nq),
        in_specs=[
            pl.BlockSpec((Q_DIM, tq), lambda b, i: (0, b * nq + i)),
            pl.BlockSpec((1, nc, KV_DIM), lambda b, i: (b, 0, 0)),
            pl.BlockSpec((1, KV_DIM, nc), lambda b, i: (b, 0, 0)),
            pl.BlockSpec((2 * GATE_ROWS, tq), lambda b, i: (0, b * nq + i)),
            pl.BlockSpec(ovt.shape, lambda b, i: (0, 0)),
        ],
        out_specs=(
            pl.BlockSpec((Q_DIM, tq), lambda b, i: (0, b * nq + i)),
            pl.BlockSpec((1, N_GROUPS_KV, n_sel, tq), lambda b, i: (b * nq + i, 0, 0, 0)),
        ),
        out_shape=(jax.ShapeDtypeStruct((Q_DIM, T), BF16),
                   jax.ShapeDtypeStruct((B * nq, N_GROUPS_KV, n_sel, tq), BF16)),
        scratch_shapes=[pltpu.VMEM((n_sel, tq), F32)],
        compiler_params=_cparams(2),
        name="cmp_attn",
    )(qt, kc, vct, gt, ovt)


def _nsa_kernel(qt_ref, kslc_ref, kwin_ref, vslct_ref, vwint_ref, selb_ref, gt_ref, ocmpt_ref, causal_ref,
                window_ref, ot_ref, q_s, s_s, m_s, acc_s):
    i = pl.program_id(1)
    tq = qt_ref.shape[1]
    tile = ATT_TILE
    pair = 2 * tile
    n_sel = selb_ref.shape[2]
    groups = range(N_GROUPS_KV)
    gq = HEADS_PER_GROUP * HEAD_DIM

    for g in groups:
        for hh in range(HEADS_PER_GROUP):
            lanes = slice(hh * tq, (hh + 1) * tq)
            qh = qt_ref[g * gq + hh * HEAD_DIM:g * gq + (hh + 1) * HEAD_DIM, :]
            zero = jnp.zeros_like(qh)
            q_s[g, 0:KV_DIM, lanes] = jnp.concatenate([qh, zero] if g == 0 else [zero, qh], axis=0)
            q_s[g, KV_DIM:KV_DIM + n_sel, lanes] = selb_ref[0, g]
            q_s[g, KV_DIM + n_sel:2 * KV_DIM, lanes] = jnp.zeros((KV_DIM - n_sel, tq), BF16)

    def v_rows(ref, g, tiles):
        vt = jnp.concatenate([ref[j, g * HEAD_DIM:(g + 1) * HEAD_DIM, :] for j in tiles], axis=1)
        return jnp.concatenate([vt, jnp.ones((SUM_ROWS, vt.shape[1]), BF16)], axis=0)

    def sel_scores(g, jj):
        keys = kslc_ref[pl.ds(pl.multiple_of(jj * pair, pair), pair), :]
        return _dot(keys, q_s[g])

    def sel_update(g, s, jj):
        m_old = m_s[g]
        m_new = jnp.maximum(m_old, jnp.max(s, axis=0, keepdims=True))
        p = jnp.exp2(s - m_new).astype(BF16)
        acc_s[g] = jnp.exp2(m_old - m_new) * acc_s[g] + _dot(v_rows(vslct_ref, g, (2 * jj, 2 * jj + 1)), p)
        m_s[g] = m_new

    m_s[...] = jnp.full(m_s.shape, NEG_INF, F32)
    acc_s[...] = jnp.zeros(acc_s.shape, F32)
    n_full = i // 2
    for g in groups:
        s_s[g] = sel_scores(g, 0)

    def sel_body(jj, carry):
        for g in groups:
            s = s_s[g]
            s_s[g] = sel_scores(g, jj + 1)
            sel_update(g, s, jj)
        return carry

    lax.fori_loop(0, n_full, sel_body, 0)

    j0 = jnp.maximum(i - WINDOW // tile, 0)
    n_win = WINDOW + tile
    for g in groups:
        sel_update(g, s_s[g] + causal_ref[0], n_full)
        o_slc = acc_s[g, 0:HEAD_DIM, :] * (1.0 / acc_s[g, HEAD_DIM:HEAD_DIM + 1, :])
        s = _dot(kwin_ref[pl.ds(pl.multiple_of(j0 * tile, tile), n_win), :], q_s[g, 0:KV_DIM, :])
        s = s + window_ref[0]
        p = jnp.exp2(s - jnp.max(s, axis=0, keepdims=True)).astype(BF16)
        ow = _dot(v_rows(vwint_ref, g, (j0, j0 + 1, j0 + 2)), p)
        o_win = ow[0:HEAD_DIM, :] * (1.0 / ow[HEAD_DIM:HEAD_DIM + 1, :])
        for hh in range(HEADS_PER_GROUP):
            lanes = slice(hh * tq, (hh + 1) * tq)
            rows = slice(g * gq + hh * HEAD_DIM, g * gq + (hh + 1) * HEAD_DIM)
            gate = g * GATE_ROWS + hh * N_BRANCH
            o = (ocmpt_ref[rows, :].astype(F32) + gt_ref[gate + 1:gate + 2, :] * o_slc[:, lanes]
                 + gt_ref[gate + 2:gate + 3, :] * o_win[:, lanes])
            ot_ref[rows, :] = o.astype(BF16)


def _nsa_attn(qt, kslc, kwin, vslct, vwint, selb, gt, ocmpt, B, seq):
    T = qt.shape[1]
    tq = ATT_TILE
    nq = seq // tq
    ktiles = seq // ATT_TILE
    n_sel = selb.shape[2]
    wide = HEADS_PER_GROUP * tq
    n_win = WINDOW + ATT_TILE
    kmq = (lax.broadcasted_iota(jnp.int32, (1, n_win, wide), 1)
           - lax.broadcasted_iota(jnp.int32, (1, n_win, wide), 2) % tq)
    par = lax.broadcasted_iota(jnp.int32, (2, 1, 1), 0)
    causal = jnp.where(kmq[:, :2 * ATT_TILE] <= par * ATT_TILE, 0.0, NEG_INF).astype(F32)
    behind = lax.broadcasted_iota(jnp.int32, (n_win // ATT_TILE, 1, 1), 0) * ATT_TILE - kmq
    window = jnp.where((behind >= 0) & (behind < WINDOW), 0.0, NEG_INF).astype(F32)
    qblk = pl.BlockSpec((Q_DIM, tq), lambda b, i: (0, b * nq + i))
    vblk = pl.BlockSpec((ktiles, KV_DIM, ATT_TILE), lambda b, i: (b, 0, 0))
    return pl.pallas_call(
        _nsa_kernel,
        grid=(B, nq),
        in_specs=[
            qblk,
            pl.BlockSpec((seq, 2 * KV_DIM), lambda b, i: (b, 0)),
            pl.BlockSpec((seq, KV_DIM), lambda b, i: (b, 0)),
            vblk, vblk,
            pl.BlockSpec((1, N_GROUPS_KV, n_sel, tq), lambda b, i: (b * nq + i, 0, 0, 0)),
            pl.BlockSpec((N_GROUPS_KV * GATE_ROWS, tq), lambda b, i: (0, b * nq + i)),
            qblk,
            pl.BlockSpec((1, 2 * ATT_TILE, wide), lambda b, i: (i % 2, 0, 0)),
            pl.BlockSpec((1, n_win, wide), lambda b, i: (jnp.minimum(i, WINDOW // ATT_TILE), 0, 0)),
        ],
        out_specs=qblk,
        out_shape=jax.ShapeDtypeStruct((Q_DIM, T), BF16),
        scratch_shapes=[
            pltpu.VMEM((N_GROUPS_KV, 2 * KV_DIM, wide), BF16),
            pltpu.VMEM((N_GROUPS_KV, 2 * ATT_TILE, wide), F32),
            pltpu.VMEM((N_GROUPS_KV, 1, wide), F32),
            pltpu.VMEM((N_GROUPS_KV, HEAD_DIM + SUM_ROWS, wide), F32),
        ],
        compiler_params=_cparams(2),
        name="nsa_attn",
    )(qt, kslc, kwin, vslct, vwint, selb, gt, ocmpt, causal, window)


def _route(lt):
    row = lax.broadcasted_iota(jnp.int32, lt.shape, 0)
    big = 2 ** 20
    col = lambda f, v: f(v, axis=0, keepdims=True)
    grp = jnp.where((row >= N_EXPERTS) & (row < N_EXPERTS + N_EXPERT_GROUPS), lt, NEG_INF)
    gmax = col(jnp.max, grp)
    g_idx = col(jnp.min, jnp.where(grp == gmax, row, big)) - N_EXPERTS
    p_group = 1.0 / col(jnp.sum, jnp.exp(grp - gmax))
    own = jnp.where((row < N_EXPERTS) & (row // EXPERTS_PER_GROUP == g_idx), lt, NEG_INF)
    l1 = col(jnp.max, own)
    e1 = col(jnp.min, jnp.where(own == l1, row, big))
    rest = jnp.where(row == e1, NEG_INF, own)
    l2 = col(jnp.max, rest)
    e2 = col(jnp.min, jnp.where(rest == l2, row, big))
    r = jnp.exp(l2 - l1)
    c1 = p_group / (1.0 + r)
    return e1, e2, c1, c1 * r


R_SLOT1, R_SLOT2, R_C1, R_C2 = range(4)
ROUTE_ROWS = 40


def _merge_kernel(alpha, ot_ref, cu_ref, x_ref, wn_ref, wc_ref, wgm_ref, wo_ref, g1_ref, b1_ref,
                  wrt_ref, brt_ref, h1_ref, routet_ref, cnt_ref):
    tm, d = x_ref.shape
    iota = lambda shape, dim: lax.broadcasted_iota(jnp.int32, shape, dim)
    sq = (MOE_TILE, MOE_TILE)
    earlier_token = jnp.where(iota(sq, 0) < iota(sq, 1), 1.0, 0.0).astype(BF16)
    lower_expert = jnp.where(iota((LANES, LANES), 0) < iota((LANES, LANES), 1), 1.0, 0.0).astype(BF16)
    x = x_ref[...]
    xb = x.astype(BF16)
    y_nsa = _dot_tn(ot_ref[...], wn_ref[...])
    y_conv = _dot(cu_ref[...], wc_ref[...])
    merged = (_sigmoid(_dot(xb, wgm_ref[:, 0:d])) * y_nsa
              + _sigmoid(_dot(xb, wgm_ref[:, d:2 * d])) * y_conv)
    mix = _dot(merged.astype(BF16), wo_ref[...])
    h1 = _layer_norm(alpha * x + mix, g1_ref[...], b1_ref[...])
    h1_ref[...] = h1
    lt = (_dot_nt(wrt_ref[...], h1.astype(BF16)) + brt_ref[...])[0:ROUTE_ROWS, :]
    e1, e2, c1, c2 = _route(lt)
    row = iota((LANES, tm), 0)
    pick1 = row == e1
    pick2 = row == e2
    one = lambda m: jnp.where(m, 1.0, 0.0).astype(BF16)
    ones8 = jnp.ones((8, MOE_TILE), BF16)
    recs = []
    for sub in range(tm // MOE_TILE):
        cols = slice(sub * MOE_TILE, (sub + 1) * MOE_TILE)
        oh1, oh2 = one(pick1[:, cols]), one(pick2[:, cols])
        oh = oh1 + oh2
        before = _dot(oh, earlier_token)
        cnt = _dot_nt(ones8, oh)
        chunks = jnp.floor((cnt + (CHUNK - 1.0)) * (1.0 / CHUNK))
        run_start = (_dot(chunks.astype(BF16), lower_expert) * CHUNK).astype(BF16)
        slot1 = (_dot(run_start, oh1)[0:1, :]
                 + jnp.sum(jnp.where(pick1[:, cols], before, 0.0), axis=0, keepdims=True))
        slot2 = (_dot(run_start, oh2)[0:1, :]
                 + jnp.sum(jnp.where(pick2[:, cols], before, 0.0), axis=0, keepdims=True))
        recs.append(jnp.concatenate([slot1, slot2, c1[:, cols], c2[:, cols],
                                     jnp.zeros((4, MOE_TILE), F32)], axis=0))
        cnt_ref[sub] = cnt
    routet_ref[...] = jnp.concatenate(recs, axis=1)


def _merge(ot, cu, x2, wn, wc, wgm, wo, g1, b1, wrt, brt, alpha):
    T, D = x2.shape
    tm = brt.shape[1]
    sub = tm // MOE_TILE
    full = lambda a: pl.BlockSpec(a.shape, lambda i: (0,) * a.ndim)
    return pl.pallas_call(
        functools.partial(_merge_kernel, alpha),
        grid=(T // tm,),
        in_specs=[
            pl.BlockSpec((Q_DIM, tm), lambda i: (0, i)),
            pl.BlockSpec((tm, CONV_DIM), lambda i: (i, 0)),
            pl.BlockSpec((tm, D), lambda i: (i, 0)),
            full(wn), full(wc), full(wgm), full(wo), full(g1), full(b1), full(wrt), full(brt),
        ],
        out_specs=(pl.BlockSpec((tm, D), lambda i: (i, 0)),
                   pl.BlockSpec((8, tm), lambda i: (0, i)),
                   pl.BlockSpec((sub, 8, LANES), lambda i: (i, 0, 0))),
        out_shape=(jax.ShapeDtypeStruct((T, D), F32),
                   jax.ShapeDtypeStruct((8, T), F32),
                   jax.ShapeDtypeStruct((T // MOE_TILE, 8, LANES), F32)),
        compiler_params=_cparams(1),
        name="merge",
    )(ot, cu, x2, wn, wc, wgm, wo, g1, b1, wrt, brt)


def _chunk_rows(q):
    return pl.ds(pl.multiple_of(q * CHUNK, CHUNK), CHUNK)


def _wait_chunks(n, copy_of_rows):
    size = MAX_CHUNKS
    while size >= 1:
        @pl.when((n & size) != 0)
        def _(size=size):
            copy_of_rows(size * CHUNK).wait()
        size //= 2


def _dispatch_kernel(tab_ref, tab_m1_ref, tab_m2_ref, tails_ref, nused_ref, h1_ref, routet_ref, xs_ref,
                     buf, zero_s, sem):
    i = pl.program_id(0)
    last = pl.num_programs(0) - 1
    tm = h1_ref.shape[0]
    slot = i % 2

    def drain(tab, slot_):
        _wait_chunks(tab[0, 0, TAB_COUNT],
                     lambda rows: pltpu.make_async_copy(buf.at[slot_, pl.ds(0, rows)], xs_ref.at[pl.ds(0, rows)],
                                                        sem.at[slot_]))

    s1 = routet_ref[R_SLOT1:R_SLOT1 + 1, :].astype(jnp.int32)
    s2 = routet_ref[R_SLOT2:R_SLOT2 + 1, :].astype(jnp.int32)
    r_io = lax.broadcasted_iota(jnp.int32, (LOCAL_ROWS, tm), 0)
    perm = jnp.where(r_io == s1, 1.0, jnp.where(r_io == s2, 1.0, 0.0)).astype(BF16)
    srt = _dot(perm, h1_ref[...].astype(BF16)).astype(BF16)

    @pl.when(i >= 2)
    def _():
        drain(tab_m2_ref, slot)

    buf[slot] = srt

    def start(q, c):
        dst = pl.multiple_of(tab_ref[0, 0, q], CHUNK)
        pltpu.make_async_copy(buf.at[slot, _chunk_rows(q)], xs_ref.at[pl.ds(dst, CHUNK)], sem.at[slot]).start()
        return c

    lax.fori_loop(0, tab_ref[0, 0, TAB_COUNT], start, 0)

    @pl.when(i == last)
    def _():
        drain(tab_ref, slot)

    @pl.when(jnp.logical_and(i == last, i >= 1))
    def _():
        drain(tab_m1_ref, 1 - slot)

    def tail_copy(q):
        dst = pl.multiple_of(jnp.maximum(tails_ref[q], 0), CHUNK)
        return pltpu.make_async_copy(zero_s.at[pl.ds(0, CHUNK)], xs_ref.at[pl.ds(dst, CHUNK)], sem.at[2])

    def tile_copy(j):
        dst = pl.multiple_of(j * EXPERT_TILE, EXPERT_TILE)
        return pltpu.make_async_copy(zero_s, xs_ref.at[pl.ds(dst, EXPERT_TILE)], sem.at[2])

    def for_each_fill(on_copy):
        def tail(q, c):
            @pl.when(tails_ref[q] >= 0)
            def _():
                on_copy(tail_copy(q))
            return c

        def tile(j, c):
            on_copy(tile_copy(j))
            return c

        lax.fori_loop(0, tails_ref.shape[0], tail, 0)
        lax.fori_loop(nused_ref[0], xs_ref.shape[0] // EXPERT_TILE, tile, 0)

    @pl.when(i == 0)
    def _():
        zero_s[...] = jnp.zeros(zero_s.shape, zero_s.dtype)
        for_each_fill(lambda cp: cp.start())

    @pl.when(i == last)
    def _():
        for_each_fill(lambda cp: cp.wait())


def _dispatch(tab, tails, n_used, h1, route_t, n_rows):
    T, D = h1.shape
    tm = MOE_TILE
    tab_blk = lambda back: pl.BlockSpec((1, 1, LANES), lambda i: (jnp.maximum(i - back, 0), 0, 0),
                                        memory_space=pltpu.SMEM)
    return pl.pallas_call(
        _dispatch_kernel,
        grid=(T // tm,),
        in_specs=[
            tab_blk(0), tab_blk(1), tab_blk(2),
            pl.BlockSpec(memory_space=pltpu.SMEM),
            pl.BlockSpec(memory_space=pltpu.SMEM),
            pl.BlockSpec((tm, D), lambda i: (i, 0)),
            pl.BlockSpec((8, tm), lambda i: (0, i)),
        ],
        out_specs=pl.BlockSpec(memory_space=pl.ANY),
        out_shape=jax.ShapeDtypeStruct((n_rows, D), BF16),
        scratch_shapes=[pltpu.VMEM((2, LOCAL_ROWS, D), BF16), pltpu.VMEM((EXPERT_TILE, D), BF16),
                        pltpu.SemaphoreType.DMA((3,))],
        compiler_params=_cparams(1),
        name="moe_dispatch",
    )(tab, tab, tab, tails, n_used, h1, route_t)


def _experts_kernel(te_ref, nu_ref, xs_ref, wg_ref, wu_ref, wd_ref, ys_ref):
    del te_ref
    j = pl.program_id(0)

    @pl.when(j < nu_ref[0])
    def _():
        xb = xs_ref[...]
        hg = _dot(xb, wg_ref[0].astype(BF16))
        h = hg * _sigmoid(hg) * _dot(xb, wu_ref[0].astype(BF16))
        y = _dot(h.astype(BF16), wd_ref[0].astype(BF16))
        ys_ref[...] = y.astype(BF16)

    @pl.when(j >= nu_ref[0])
    def _():
        ys_ref[...] = jnp.zeros(ys_ref.shape, ys_ref.dtype)


def _experts(tile_expert, n_used, xs, wg, wu, wd):
    n_rows, D = xs.shape
    hid = wg.shape[2]
    tm = EXPERT_TILE
    used = lambda j, te, nu: (jnp.minimum(j, nu[0] - 1), 0)
    grid_spec = pltpu.PrefetchScalarGridSpec(
        num_scalar_prefetch=2,
        grid=(n_rows // tm,),
        in_specs=[
            pl.BlockSpec((tm, D), used),
            pl.BlockSpec((1, D, hid), lambda j, te, nu: (te[j], 0, 0)),
            pl.BlockSpec((1, D, hid), lambda j, te, nu: (te[j], 0, 0)),
            pl.BlockSpec((1, hid, D), lambda j, te, nu: (te[j], 0, 0)),
        ],
        out_specs=pl.BlockSpec((tm, D), lambda j, te, nu: (j, 0)),
    )
    return pl.pallas_call(
        _experts_kernel,
        grid_spec=grid_spec,
        out_shape=jax.ShapeDtypeStruct((n_rows, D), BF16),
        compiler_params=_cparams(1),
        name="moe_experts",
    )(tile_expert, n_used, xs, wg, wu, wd)


def _combine_kernel(alpha, tab_ref, tab_next_ref, routet_ref, h1_ref, p_ref, pproj_ref, pgw_ref,
                    pgb_ref, g2_ref, b2_ref, ys_ref, out_ref, buf, sem):
    i = pl.program_id(0)
    tm = h1_ref.shape[0]
    slot = i % 2

    def gather(tab, slot_):
        def start(q, c):
            src = pl.multiple_of(tab[0, 0, q], CHUNK)
            pltpu.make_async_copy(ys_ref.at[pl.ds(src, CHUNK)], buf.at[slot_, _chunk_rows(q)],
                                  sem.at[slot_]).start()
            return c
        lax.fori_loop(0, tab[0, 0, TAB_COUNT], start, 0)

    @pl.when(i == 0)
    def _():
        buf[...] = jnp.zeros(buf.shape, buf.dtype)
        gather(tab_ref, 0)

    @pl.when(i + 1 < pl.num_programs(0))
    def _():
        gather(tab_next_ref, 1 - slot)

    h1 = h1_ref[...]
    ple = (_sigmoid(_dot(h1.astype(BF16), pgw_ref[...]) + pgb_ref[...])
           * _dot(p_ref[...].astype(BF16), pproj_ref[...]))

    _wait_chunks(tab_ref[0, 0, TAB_COUNT],
                 lambda rows: pltpu.make_async_copy(ys_ref.at[pl.ds(0, rows)], buf.at[slot, pl.ds(0, rows)],
                                                    sem.at[slot]))

    s1 = routet_ref[R_SLOT1:R_SLOT1 + 1, :].astype(jnp.int32)
    s2 = routet_ref[R_SLOT2:R_SLOT2 + 1, :].astype(jnp.int32)
    c1 = routet_ref[R_C1:R_C1 + 1, :]
    c2 = routet_ref[R_C2:R_C2 + 1, :]
    r_io = lax.broadcasted_iota(jnp.int32, (LOCAL_ROWS, tm), 0)
    weights = jnp.where(r_io == s1, c1, jnp.where(r_io == s2, c2, 0.0)).astype(BF16)
    ffn = _dot_tn(weights, buf[slot])
    out_ref[...] = _layer_norm(alpha * h1 + ffn + ple, g2_ref[...], b2_ref[...])


def _combine(tab, route_t, h1, p2, pproj, pgw, pgb, g2, b2, ys, alpha):
    T, D = h1.shape
    tm = MOE_TILE
    nt = T // tm
    full = lambda a: pl.BlockSpec(a.shape, lambda i: (0,) * a.ndim)
    return pl.pallas_call(
        functools.partial(_combine_kernel, alpha),
        grid=(nt,),
        in_specs=[
            pl.BlockSpec((1, 1, LANES), lambda i: (i, 0, 0), memory_space=pltpu.SMEM),
            pl.BlockSpec((1, 1, LANES), lambda i: (jnp.minimum(i + 1, nt - 1), 0, 0), memory_space=pltpu.SMEM),
            pl.BlockSpec((8, tm), lambda i: (0, i)),
            pl.BlockSpec((tm, D), lambda i: (i, 0)),
            pl.BlockSpec((tm, p2.shape[1]), lambda i: (i, 0)),
            full(pproj), full(pgw), full(pgb), full(g2), full(b2),
            pl.BlockSpec(memory_space=pl.ANY),
        ],
        out_specs=pl.BlockSpec((tm, D), lambda i: (i, 0)),
        out_shape=jax.ShapeDtypeStruct((T, D), F32),
        scratch_shapes=[pltpu.VMEM((2, LOCAL_ROWS, D), BF16), pltpu.SemaphoreType.DMA((2,))],
        compiler_params=_cparams(1),
        name="moe_combine",
    )(tab, tab, route_t, h1, p2, pproj, pgw, pgb, g2, b2, ys)


def _moe_plan(cnt_tiles, n_expert_tiles):
    cnt = cnt_tiles[:, 0, :N_EXPERTS].astype(jnp.int32)
    cnt8 = (cnt + CHUNK - 1) // CHUNK * CHUNK
    lend = jnp.cumsum(cnt8, axis=1)
    lstart = lend - cnt8
    gend = jnp.cumsum(cnt8, axis=0)
    region = (gend[-1] + EXPERT_TILE - 1) // EXPERT_TILE * EXPERT_TILE
    oend = jnp.cumsum(region)
    gstart = (oend - region)[None, :] + gend - cnt8
    q8 = jnp.arange(MAX_CHUNKS, dtype=jnp.int32) * CHUNK
    eq = jnp.minimum(jnp.sum((lend[:, None, :] <= q8[None, :, None]).astype(jnp.int32), axis=-1), N_EXPERTS - 1)
    shift = jnp.sum(jnp.where(eq[:, :, None] == jnp.arange(N_EXPERTS)[None, None, :],
                              (gstart - lstart)[:, None, :], 0), axis=-1)
    dstq = shift + q8[None, :]
    nt = cnt.shape[0]
    tab = jnp.concatenate([dstq, jnp.zeros((nt, TAB_COUNT - MAX_CHUNKS), jnp.int32), lend[:, -1:] // CHUNK], axis=1)
    tile_row = jnp.arange(n_expert_tiles, dtype=jnp.int32) * EXPERT_TILE
    tile_expert = jnp.minimum(jnp.sum((oend[None, :] <= tile_row[:, None]).astype(jnp.int32), axis=1),
                              N_EXPERTS - 1)
    n_used = (oend[-1] // EXPERT_TILE).reshape(1)
    c = jnp.arange(EXPERT_TILE // CHUNK, dtype=jnp.int32)[None, :] * CHUNK
    tail_start = (oend - region + gend[-1])[:, None] + c
    tails = jnp.where(tail_start < oend[:, None], tail_start, -1).reshape(-1)
    return tab.reshape(nt, 1, LANES), tails, tile_expert, n_used


def _split_w_in(w_in, D):
    sizes = [Q_DIM] + [KV_DIM] * 6 + [N_HEADS * N_BRANCH] + [CONV_DIM] * 3 + [D] * 2
    offs = np.concatenate([[0], np.cumsum(sizes)])
    names = ["q", "k_cmp", "v_cmp", "k_slc", "v_slc", "k_win", "v_win", "g_nsa",
             "conv_b", "conv_c", "conv_h", "g_m_nsa", "g_m_conv"]
    return {n: w_in[:, int(offs[k]):int(offs[k + 1])] for k, n in enumerate(names)}


def _layer(x2, p2, B, seq, depth, w_in, cmp_pe, cmp_w1, cmp_b1, cmp_w2, cmp_b2, conv_w, w_nsa_out,
           w_conv_out, w_o, ln1_g, ln1_b, rg_w, rg_b, re_w, re_b, e_wg, e_wu, e_wd, ple_proj,
           ple_gate_w, ple_gate_b, ln2_g, ln2_b):
    T, D = x2.shape
    alpha = (2.0 * depth) ** 0.25
    w = _split_w_in(w_in, D)
    wtok = jnp.concatenate([w["k_cmp"], w["v_cmp"], w["k_slc"], w["k_win"],
                            w["conv_b"], w["conv_c"], w["conv_h"]], axis=1).astype(BF16)
    gcols = w["g_nsa"].reshape(D, N_GROUPS_KV, HEADS_PER_GROUP * N_BRANCH)
    gcols = jnp.pad(gcols, ((0, 0), (0, 0), (0, GATE_ROWS - HEADS_PER_GROUP * N_BRANCH)))
    wt = jnp.concatenate([w["q"], gcols.reshape(D, N_GROUPS_KV * GATE_ROWS), w["v_slc"], w["v_win"]],
                         axis=1).astype(BF16).T
    kcmp, vcmp, kslc, kwin, cu, qt, gt, vslct, vwint = _in_proj(x2, wtok, wt, conv_w, seq)

    half = CMP_BLOCK // 2
    n_chunks = seq // CMP_STRIDE
    n_cmp = (seq - CMP_BLOCK) // CMP_STRIDE + 1
    hidden = cmp_w1.shape[-1]
    eye = jnp.eye(N_GROUPS_KV, dtype=F32)
    w1r = cmp_w1.reshape(2, CMP_BLOCK, HEAD_DIM, hidden)
    expand = lambda m: jnp.einsum("ildh,gk->ilgdkh", m, eye).reshape(
        2, half * KV_DIM, N_GROUPS_KV * hidden).astype(BF16)
    wa, wb = expand(w1r[:, :half]), expand(w1r[:, half:])
    pe = jnp.broadcast_to(cmp_pe.reshape(2, 1, CMP_BLOCK * HEAD_DIM), (2, 8, CMP_BLOCK * HEAD_DIM)).astype(BF16)
    b1t = jnp.tile(cmp_b1.reshape(2, 1, hidden), (1, 1, N_GROUPS_KV))
    w2b = jnp.einsum("ihd,gk->ighkd", cmp_w2, eye).reshape(2, N_GROUPS_KV * hidden, KV_DIM).astype(BF16)
    b2t = jnp.tile(cmp_b2.reshape(2, 1, HEAD_DIM), (1, 1, N_GROUPS_KV))
    kc, vct = _compress(kcmp.reshape(B, n_chunks, CMP_STRIDE * KV_DIM),
                        vcmp.reshape(B, n_chunks, CMP_STRIDE * KV_DIM),
                        wa, wb, pe, cmp_w1.astype(BF16), b1t, w2b, b2t)

    n_sel = seq // SEL_BLOCK
    c_start = np.arange(n_chunks) * CMP_STRIDE
    s_start = np.arange(n_sel) * SEL_BLOCK
    overlap = ((c_start[None, :] <= s_start[:, None] + SEL_BLOCK - 1)
               & (c_start[None, :] + CMP_BLOCK - 1 >= s_start[:, None])).astype(np.float32)
    ocmpt, selb = _cmp_attn(qt, kc, vct, gt, jnp.asarray(overlap, BF16), B, seq, n_cmp)
    ot = _nsa_attn(qt, kslc, kwin, vslct, vwint, selb, gt, ocmpt, B, seq)

    wgm = jnp.concatenate([w["g_m_nsa"], w["g_m_conv"]], axis=1).astype(BF16)
    merge_tile = 512
    wrt = jnp.pad(jnp.concatenate([re_w, rg_w], axis=1).T, ((0, LANES - N_EXPERTS - N_EXPERT_GROUPS), (0, 0)))
    brt = jnp.pad(jnp.concatenate([re_b, rg_b]), (0, LANES - N_EXPERTS - N_EXPERT_GROUPS))
    brt = jnp.broadcast_to(brt[:, None], (LANES, merge_tile))
    h1, route_t, cnt_tiles = _merge(ot, cu, x2, w_nsa_out.astype(BF16), w_conv_out.astype(BF16), wgm,
                                    w_o.astype(BF16), ln1_g.reshape(1, D), ln1_b.reshape(1, D),
                                    wrt.astype(BF16), brt, alpha)
    n_tok_tiles = T // MOE_TILE
    max_rows = 2 * T + n_tok_tiles * N_EXPERTS * (CHUNK - 1) + N_EXPERTS * (EXPERT_TILE - 1)
    n_expert_tiles = -(-max_rows // EXPERT_TILE)
    tab, tails, tile_expert, n_used = _moe_plan(cnt_tiles, n_expert_tiles)
    xs = _dispatch(tab, tails, n_used, h1, route_t, n_expert_tiles * EXPERT_TILE)
    ys = _experts(tile_expert, n_used, xs, e_wg, e_wu, e_wd)
    return _combine(tab, route_t, h1, p2, ple_proj.astype(BF16), ple_gate_w.astype(BF16),
                    ple_gate_b.reshape(1, D), ln2_g.reshape(1, D), ln2_b.reshape(1, D), ys, alpha)


def kernel(x, p, w_in, cmp_pe, cmp_w1, cmp_b1, cmp_w2, cmp_b2, conv_w, w_nsa_out, w_conv_out, w_o, ln1_g, ln1_b, router_group_w, router_group_b, router_expert_w, router_expert_b, expert_w_gate, expert_w_up, expert_w_down, ple_proj, ple_gate_w, ple_gate_b, ln2_g, ln2_b):
    B, seq, D = x.shape
    depth = w_in.shape[0]
    x2 = x.reshape(B * seq, D)
    for i in range(depth):
        x2 = _layer(x2, p[i].reshape(B * seq, -1), B, seq, depth, w_in[i], cmp_pe[i], cmp_w1[i], cmp_b1[i],
                    cmp_w2[i], cmp_b2[i], conv_w[i], w_nsa_out[i], w_conv_out[i], w_o[i], ln1_g[i], ln1_b[i],
                    router_group_w[i], router_group_b[i], router_expert_w[i], router_expert_b[i],
                    expert_w_gate[i], expert_w_up[i], expert_w_down[i], ple_proj[i], ple_gate_w[i],
                    ple_gate_b[i], ln2_g[i], ln2_b[i])
    return x2.reshape(B, seq, D)
```

```python
import functools

import jax
import jax.numpy as jnp
import numpy as np
from jax import lax
from jax.experimental import pallas as pl
from jax.experimental.pallas import tpu as pltpu

F32 = jnp.float32
BF16 = jnp.bfloat16

N_HEADS = 8
N_GROUPS_KV = 2
HEADS_PER_GROUP = N_HEADS // N_GROUPS_KV
HEAD_DIM = 64
Q_DIM = N_HEADS * HEAD_DIM
KV_DIM = N_GROUPS_KV * HEAD_DIM
N_BRANCH = 3
CMP_BLOCK = 32
CMP_STRIDE = 16
SEL_BLOCK = 64
SEL_TOPK = 16
WINDOW = 512
CONV_DIM = 512
N_EXPERT_GROUPS = 4
EXPERTS_PER_GROUP = 8
N_EXPERTS = N_EXPERT_GROUPS * EXPERTS_PER_GROUP
ATTN_SCALE = HEAD_DIM ** -0.5
LOG2_E = 1.4426950408889634
FORCE_BONUS = 1e4
NEG_INF = -1e30
LN_EPS = 1e-5

LANES = 128
ATT_TILE = 256
GATE_ROWS = 16
SUM_ROWS = 16
VMEM_LIMIT = 48 * 1024 * 1024

MOE_TILE = 256
EXPERT_TILE = 1024
CHUNK = 16
LOCAL_ROWS = -(-(2 * MOE_TILE + N_EXPERTS * (CHUNK - 1)) // ATT_TILE) * ATT_TILE
MAX_CHUNKS = LOCAL_ROWS // CHUNK
TAB_COUNT = LANES - 1


def _cparams(n_axes):
    return pltpu.CompilerParams(dimension_semantics=("arbitrary",) * n_axes,
                                vmem_limit_bytes=VMEM_LIMIT)


def _dot(a, b):
    return jnp.dot(a, b, preferred_element_type=F32)


def _dot_nt(a, b):
    return lax.dot_general(a, b, (((1,), (1,)), ((), ())), preferred_element_type=F32)


def _dot_tn(a, b):
    return lax.dot_general(a, b, (((0,), (0,)), ((), ())), preferred_element_type=F32)


def _sigmoid(v):
    return 1.0 / (1.0 + jnp.exp(-v))


def _layer_norm(v, g, b):
    mu = jnp.mean(v, axis=-1, keepdims=True)
    d = v - mu
    var = jnp.mean(d * d, axis=-1, keepdims=True)
    return d * lax.rsqrt(var + LN_EPS) * g + b


def _inproj_kernel(seq, x_ref, xprev_ref, wtok_ref, wt_ref, convw_ref,
                   kcmp_ref, vcmp_ref, kslc_ref, kwin_ref, cu_ref, qt_ref, gt_ref, vslct_ref, vwint_ref, cmp_s):
    i = pl.program_id(0)
    tm = x_ref.shape[0]
    xb = x_ref[...].astype(BF16)

    c0 = 4 * KV_DIM
    kvf = _dot(xb, wtok_ref[:, 0:c0])
    kv = kvf.astype(BF16)
    cmp_s[0] = kvf[:, 0:KV_DIM]
    cmp_s[1] = kvf[:, KV_DIM:2 * KV_DIM]
    for l in range(CMP_STRIDE):
        rows = pl.ds(l, tm // CMP_STRIDE, stride=CMP_STRIDE)
        kcmp_ref[:, l * KV_DIM:(l + 1) * KV_DIM] = cmp_s[0, rows, :].astype(BF16)
        vcmp_ref[:, l * KV_DIM:(l + 1) * KV_DIM] = cmp_s[1, rows, :].astype(BF16)
    kwin_ref[...] = kv[:, 3 * KV_DIM:4 * KV_DIM]
    n_sel = seq // SEL_BLOCK
    blk = ((i * tm + lax.broadcasted_iota(jnp.int32, (tm, KV_DIM), 0)) // SEL_BLOCK) % n_sel
    onehot = jnp.where(lax.broadcasted_iota(jnp.int32, (tm, KV_DIM), 1) == blk, 1.0, 0.0).astype(BF16)
    kslc_ref[...] = jnp.concatenate([kv[:, 2 * KV_DIM:3 * KV_DIM], onehot], axis=1)
    conv = _dot(xb, wtok_ref[:, c0:c0 + 3 * CONV_DIM])
    cb = conv[:, 0:CONV_DIM]
    u = conv[:, CONV_DIM:2 * CONV_DIM] * conv[:, 2 * CONV_DIM:3 * CONV_DIM]
    xpb = xprev_ref[...].astype(BF16)
    up = (_dot(xpb, wtok_ref[:, c0 + CONV_DIM:c0 + 2 * CONV_DIM])
          * _dot(xpb, wtok_ref[:, c0 + 2 * CONV_DIM:c0 + 3 * CONV_DIM]))
    up = jnp.where(i % (seq // tm) == 0, 0.0, up)
    row = lax.broadcasted_iota(jnp.int32, (tm, CONV_DIM), 0)
    u1 = jnp.where(row == 0, up[7:8, :], pltpu.roll(u, 1, 0))
    u2 = jnp.where(row == 0, up[6:7, :], jnp.where(row == 1, up[7:8, :], pltpu.roll(u, 2, 0)))
    w = convw_ref[...]
    uc = w[0:1, :] * u2 + w[1:2, :] * u1 + w[2:3, :] * u
    cu_ref[...] = (cb * uc).astype(BF16)

    zt = _dot_nt(wt_ref[...], xb)
    qt_ref[...] = (zt[0:Q_DIM, :] * (ATTN_SCALE * LOG2_E)).astype(BF16)
    r0 = Q_DIM
    gt_ref[...] = _sigmoid(zt[r0:r0 + 2 * GATE_ROWS, :])
    r0 += 2 * GATE_ROWS
    vs = zt[r0:r0 + KV_DIM, :].astype(BF16)
    vw = zt[r0 + KV_DIM:r0 + 2 * KV_DIM, :].astype(BF16)
    for c in range(tm // ATT_TILE):
        vslct_ref[c] = vs[:, c * ATT_TILE:(c + 1) * ATT_TILE]
        vwint_ref[c] = vw[:, c * ATT_TILE:(c + 1) * ATT_TILE]


def _in_proj(x2, wtok, wt, conv_w, seq):
    T, D = x2.shape
    tm = 1024
    nt = T // tm
    n_tok = wtok.shape[1]
    n_t = wt.shape[0]
    row_blk = lambda n: pl.BlockSpec((tm, n), lambda i: (i, 0))
    out_shape = (
        jax.ShapeDtypeStruct((T // CMP_STRIDE, CMP_STRIDE * KV_DIM), BF16),
        jax.ShapeDtypeStruct((T // CMP_STRIDE, CMP_STRIDE * KV_DIM), BF16),
        jax.ShapeDtypeStruct((T, 2 * KV_DIM), BF16),
        jax.ShapeDtypeStruct((T, KV_DIM), BF16),
        jax.ShapeDtypeStruct((T, CONV_DIM), BF16),
        jax.ShapeDtypeStruct((Q_DIM, T), BF16),
        jax.ShapeDtypeStruct((2 * GATE_ROWS, T), F32),
        jax.ShapeDtypeStruct((T // ATT_TILE, KV_DIM, ATT_TILE), BF16),
        jax.ShapeDtypeStruct((T // ATT_TILE, KV_DIM, ATT_TILE), BF16),
    )
    vt_blk = pl.BlockSpec((tm // ATT_TILE, KV_DIM, ATT_TILE), lambda i: (i, 0, 0))
    chunk_blk = pl.BlockSpec((tm // CMP_STRIDE, CMP_STRIDE * KV_DIM), lambda i: (i, 0))
    return pl.pallas_call(
        functools.partial(_inproj_kernel, seq),
        grid=(nt,),
        in_specs=[
            row_blk(D),
            pl.BlockSpec((8, D), lambda i: (jnp.maximum(i * (tm // 8) - 1, 0), 0)),
            pl.BlockSpec((D, n_tok), lambda i: (0, 0)),
            pl.BlockSpec((n_t, D), lambda i: (0, 0)),
            pl.BlockSpec((3, CONV_DIM), lambda i: (0, 0)),
        ],
        out_specs=(
            chunk_blk, chunk_blk, row_blk(2 * KV_DIM), row_blk(KV_DIM), row_blk(CONV_DIM),
            pl.BlockSpec((Q_DIM, tm), lambda i: (0, i)),
            pl.BlockSpec((2 * GATE_ROWS, tm), lambda i: (0, i)),
            vt_blk, vt_blk,
        ),
        out_shape=out_shape,
        scratch_shapes=[pltpu.VMEM((2, tm, KV_DIM), F32)],
        compiler_params=_cparams(1),
        name="in_proj",
    )(x2, x2, wtok, wt, conv_w)


def _compress_kernel(kin_ref, vin_ref, wa_ref, wb_ref, pe_ref, w1_ref, b1_ref, w2_ref, b2_ref,
                     kc_ref, vct_ref):
    def one(idx, in_ref):
        c = in_ref[0]
        a = _dot(c, wa_ref[idx])
        b = _dot(c, wb_ref[idx])
        n = b.shape[0]
        peb = _dot(pe_ref[idx], w1_ref[idx])[0:1, :]
        bias = jnp.concatenate([peb, peb], axis=1) + b1_ref[idx]
        h = a + pltpu.roll(b, n - 1, 0) + bias
        return _dot(jax.nn.gelu(h).astype(BF16), w2_ref[idx]) + b2_ref[idx]

    kc_ref[0] = one(0, kin_ref).astype(BF16)
    vct_ref[0] = one(1, vin_ref).T.astype(BF16)


def _compress(kcmp3, vcmp3, wa, wb, pe, w1, b1, w2, b2):
    B, nch, width = kcmp3.shape
    hid2 = wa.shape[2]
    full = lambda a: pl.BlockSpec(a.shape, lambda b: (0,) * a.ndim)
    in_blk = pl.BlockSpec((1, nch, width), lambda b: (b, 0, 0))
    out_blk = pl.BlockSpec((1, nch, KV_DIM), lambda b: (b, 0, 0))
    return pl.pallas_call(
        _compress_kernel,
        grid=(B,),
        in_specs=[in_blk, in_blk, full(wa), full(wb), full(pe), full(w1), full(b1), full(w2), full(b2)],
        out_specs=(out_blk, pl.BlockSpec((1, KV_DIM, nch), lambda b: (b, 0, 0))),
        out_shape=(jax.ShapeDtypeStruct((B, nch, KV_DIM), BF16),
                   jax.ShapeDtypeStruct((B, KV_DIM, nch), BF16)),
        compiler_params=_cparams(1),
        name="compress",
    )(kcmp3, vcmp3, wa, wb, pe, w1, b1, w2, b2)


def _group_q(qt_ref, hh, g_is0):
    qh = qt_ref[hh * HEAD_DIM:(hh + 1) * HEAD_DIM, :]
    z = jnp.zeros_like(qh)
    return jnp.concatenate([qh, z], axis=0) if g_is0 else jnp.concatenate([z, qh], axis=0)


def _cmpattn_kernel(n_cmp, qt_ref, kc_ref, vct_ref, gt_ref, ovt_ref, ocmpt_ref, selb_ref):
    i = pl.program_id(1)
    tq = qt_ref.shape[1]
    nc = kc_ref.shape[1]
    n_sel = ovt_ref.shape[0]
    kc = kc_ref[0]
    t_n = i * tq + lax.broadcasted_iota(jnp.int32, (nc, tq), 1)
    n_io = lax.broadcasted_iota(jnp.int32, (nc, tq), 0)
    blk_end = jnp.where(n_io < n_cmp, n_io * CMP_STRIDE + CMP_BLOCK - 1, jnp.int32(2 ** 30))
    vis = blk_end <= t_n
    j_io = lax.broadcasted_iota(jnp.int32, (n_sel, tq), 0)
    t_j = i * tq + lax.broadcasted_iota(jnp.int32, (n_sel, tq), 1)
    cur = t_j // SEL_BLOCK
    bonus = jnp.where(j_io == 0, FORCE_BONUS,
                      jnp.where(j_io == cur, FORCE_BONUS, jnp.where(j_io == cur - 1, FORCE_BONUS, 0.0)))
    valid = j_io * SEL_BLOCK <= t_j
    for g in range(N_GROUPS_KV):
        imp = jnp.zeros((n_sel, tq), F32)
        for hh in range(HEADS_PER_GROUP):
            h = g * HEADS_PER_GROUP + hh
            s = _dot(kc, _group_q(qt_ref, h, g == 0))
            s = jnp.where(vis, s, NEG_INF)
            m = jnp.max(s, axis=0, keepdims=True)
            p = jnp.where(vis, jnp.exp2(s - m), 0.0)
            l = jnp.sum(p, axis=0, keepdims=True)
            pr = (p * jnp.where(l > 0.0, 1.0 / l, 0.0)).astype(BF16)
            o = _dot(vct_ref[0, g * HEAD_DIM:(g + 1) * HEAD_DIM, :], pr)
            gate = gt_ref[g * GATE_ROWS + hh * N_BRANCH:g * GATE_ROWS + hh * N_BRANCH + 1, :]
            ocmpt_ref[h * HEAD_DIM:(h + 1) * HEAD_DIM, :] = (o * gate).astype(BF16)
            imp = imp + _dot(ovt_ref[...], pr)
        score = jnp.where(valid, imp + bonus, -FORCE_BONUS)
        cnt = jnp.zeros((n_sel, tq), F32)
        for jp in range(n_sel):
            r = score[jp:jp + 1, :]
            cnt = cnt + jnp.where(r > score, 1.0, jnp.where(r == score, jnp.where(j_io > jp, 1.0, 0.0), 0.0))
        selb_ref[0, g] = jnp.where(cnt < float(min(SEL_TOPK, n_sel)), 0.0, NEG_INF).astype(BF16)


def _cmp_attn(qt, kc, vct, gt, ovt, B, seq, n_cmp):
    T = qt.shape[1]
    tq = ATT_TILE
    nq = seq // tq
    n_sel = ovt.shape[0]
    nc = kc.shape[1]
    return pl.pallas_call(
        functools.partial(_cmpattn_kernel, n_cmp),
        grid=(B, nq),
        in_specs=[
            pl.BlockSpec((Q_DIM, tq), lambda b, i: (0, b * nq + i)),
            pl.BlockSpec((1, nc, KV_DIM), lambda b, i: (b, 0, 0)),
            pl.BlockSpec((1, KV_DIM, nc), lambda b, i: (b, 0, 0)),
            pl.BlockSpec((2 * GATE_ROWS, tq), lambda b, i: (0, b * nq + i)),
            pl.BlockSpec(ovt.shape, lambda b, i: (0, 0)),
        ],
        out_specs=(
            pl.BlockSpec((Q_DIM, tq), lambda b, i: (0, b * nq + i)),
            pl.BlockSpec((1, N_GROUPS_KV, n_sel, tq), lambda b, i: (b * nq + i, 0, 0, 0)),
        ),
        out_shape=(jax.ShapeDtypeStruct((Q_DIM, T), BF16),
                   jax.ShapeDtypeStruct((B * nq, N_GROUPS_KV, n_sel, tq), BF16)),
        compiler_params=_cparams(2),
        name="cmp_attn",
    )(qt, kc, vct, gt, ovt)


def _nsa_kernel(qt_ref, kslc_ref, kwin_ref, vslct_ref, vwint_ref, selb_ref, gt_ref, ocmpt_ref, causal_ref,
                window_ref, ot_ref, q_s, s_s, m_s, acc_s):
    i = pl.program_id(1)
    tq = qt_ref.shape[1]
    tile = ATT_TILE
    pair = 2 * tile
    n_sel = selb_ref.shape[2]
    groups = range(N_GROUPS_KV)
    gq = HEADS_PER_GROUP * HEAD_DIM

    for g in groups:
        for hh in range(HEADS_PER_GROUP):
            lanes = slice(hh * tq, (hh + 1) * tq)
            qh = qt_ref[g * gq + hh * HEAD_DIM:g * gq + (hh + 1) * HEAD_DIM, :]
            zero = jnp.zeros_like(qh)
            q_s[g, 0:KV_DIM, lanes] = jnp.concatenate([qh, zero] if g == 0 else [zero, qh], axis=0)
            q_s[g, KV_DIM:KV_DIM + n_sel, lanes] = selb_ref[0, g]
            q_s[g, KV_DIM + n_sel:2 * KV_DIM, lanes] = jnp.zeros((KV_DIM - n_sel, tq), BF16)

    def v_rows(ref, g, tiles):
        vt = jnp.concatenate([ref[j, g * HEAD_DIM:(g + 1) * HEAD_DIM, :] for j in tiles], axis=1)
        return jnp.concatenate([vt, jnp.ones((SUM_ROWS, vt.shape[1]), BF16)], axis=0)

    def sel_scores(g, jj):
        keys = kslc_ref[pl.ds(pl.multiple_of(jj * pair, pair), pair), :]
        return _dot(keys, q_s[g])

    def sel_update(g, s, jj):
        m_old = m_s[g]
        m_new = jnp.maximum(m_old, jnp.max(s, axis=0, keepdims=True))
        p = jnp.exp2(s - m_new).astype(BF16)
        acc_s[g] = jnp.exp2(m_old - m_new) * acc_s[g] + _dot(v_rows(vslct_ref, g, (2 * jj, 2 * jj + 1)), p)
        m_s[g] = m_new

    m_s[...] = jnp.full(m_s.shape, NEG_INF, F32)
    acc_s[...] = jnp.zeros(acc_s.shape, F32)
    n_full = i // 2
    for g in groups:
        s_s[g] = sel_scores(g, 0)

    def sel_body(jj, carry):
        for g in groups:
            s = s_s[g]
            s_s[g] = sel_scores(g, jj + 1)
            sel_update(g, s, jj)
        return carry

    lax.fori_loop(0, n_full, sel_body, 0)

    j0 = jnp.maximum(i - WINDOW // tile, 0)
    n_win = WINDOW + tile
    for g in groups:
        sel_update(g, s_s[g] + causal_ref[0], n_full)
        o_slc = acc_s[g, 0:HEAD_DIM, :] * (1.0 / acc_s[g, HEAD_DIM:HEAD_DIM + 1, :])
        s = _dot(kwin_ref[pl.ds(pl.multiple_of(j0 * tile, tile), n_win), :], q_s[g, 0:KV_DIM, :])
        s = s + window_ref[0]
        p = jnp.exp2(s - jnp.max(s, axis=0, keepdims=True)).astype(BF16)
        ow = _dot(v_rows(vwint_ref, g, (j0, j0 + 1, j0 + 2)), p)
        o_win = ow[0:HEAD_DIM, :] * (1.0 / ow[HEAD_DIM:HEAD_DIM + 1, :])
        for hh in range(HEADS_PER_GROUP):
            lanes = slice(hh * tq, (hh + 1) * tq)
            rows = slice(g * gq + hh * HEAD_DIM, g * gq + (hh + 1) * HEAD_DIM)
            gate = g * GATE_ROWS + hh * N_BRANCH
            o = (ocmpt_ref[rows, :].astype(F32) + gt_ref[gate + 1:gate + 2, :] * o_slc[:, lanes]
                 + gt_ref[gate + 2:gate + 3, :] * o_win[:, lanes])
            ot_ref[rows, :] = o.astype(BF16)


def _nsa_attn(qt, kslc, kwin, vslct, vwint, selb, gt, ocmpt, B, seq):
    T = qt.shape[1]
    tq = ATT_TILE
    nq = seq // tq
    ktiles = seq // ATT_TILE
    n_sel = selb.shape[2]
    wide = HEADS_PER_GROUP * tq
    n_win = WINDOW + ATT_TILE
    kmq = (lax.broadcasted_iota(jnp.int32, (1, n_win, wide), 1)
           - lax.broadcasted_iota(jnp.int32, (1, n_win, wide), 2) % tq)
    par = lax.broadcasted_iota(jnp.int32, (2, 1, 1), 0)
    causal = jnp.where(kmq[:, :2 * ATT_TILE] <= par * ATT_TILE, 0.0, NEG_INF).astype(F32)
    behind = lax.broadcasted_iota(jnp.int32, (n_win // ATT_TILE, 1, 1), 0) * ATT_TILE - kmq
    window = jnp.where((behind >= 0) & (behind < WINDOW), 0.0, NEG_INF).astype(F32)
    qblk = pl.BlockSpec((Q_DIM, tq), lambda b, i: (0, b * nq + i))
    vblk = pl.BlockSpec((ktiles, KV_DIM, ATT_TILE), lambda b, i: (b, 0, 0))
    return pl.pallas_call(
        _nsa_kernel,
        grid=(B, nq),
        in_specs=[
            qblk,
            pl.BlockSpec((seq, 2 * KV_DIM), lambda b, i: (b, 0)),
            pl.BlockSpec((seq, KV_DIM), lambda b, i: (b, 0)),
            vblk, vblk,
            pl.BlockSpec((1, N_GROUPS_KV, n_sel, tq), lambda b, i: (b * nq + i, 0, 0, 0)),
            pl.BlockSpec((N_GROUPS_KV * GATE_ROWS, tq), lambda b, i: (0, b * nq + i)),
            qblk,
            pl.BlockSpec((1, 2 * ATT_TILE, wide), lambda b, i: (i % 2, 0, 0)),
            pl.BlockSpec((1, n_win, wide), lambda b, i: (jnp.minimum(i, WINDOW // ATT_TILE), 0, 0)),
        ],
        out_specs=qblk,
        out_shape=jax.ShapeDtypeStruct((Q_DIM, T), BF16),
        scratch_shapes=[
            pltpu.VMEM((N_GROUPS_KV, 2 * KV_DIM, wide), BF16),
            pltpu.VMEM((N_GROUPS_KV, 2 * ATT_TILE, wide), F32),
            pltpu.VMEM((N_GROUPS_KV, 1, wide), F32),
            pltpu.VMEM((N_GROUPS_KV, HEAD_DIM + SUM_ROWS, wide), F32),
        ],
        compiler_params=_cparams(2),
        name="nsa_attn",
    )(qt, kslc, kwin, vslct, vwint, selb, gt, ocmpt, causal, window)


def _route(lt):
    row = lax.broadcasted_iota(jnp.int32, lt.shape, 0)
    big = 2 ** 20
    col = lambda f, v: f(v, axis=0, keepdims=True)
    grp = jnp.where((row >= N_EXPERTS) & (row < N_EXPERTS + N_EXPERT_GROUPS), lt, NEG_INF)
    gmax = col(jnp.max, grp)
    g_idx = col(jnp.min, jnp.where(grp == gmax, row, big)) - N_EXPERTS
    p_group = 1.0 / col(jnp.sum, jnp.exp(grp - gmax))
    own = jnp.where((row < N_EXPERTS) & (row // EXPERTS_PER_GROUP == g_idx), lt, NEG_INF)
    l1 = col(jnp.max, own)
    e1 = col(jnp.min, jnp.where(own == l1, row, big))
    rest = jnp.where(row == e1, NEG_INF, own)
    l2 = col(jnp.max, rest)
    e2 = col(jnp.min, jnp.where(rest == l2, row, big))
    r = jnp.exp(l2 - l1)
    c1 = p_group / (1.0 + r)
    return e1, e2, c1, c1 * r


R_SLOT1, R_SLOT2, R_C1, R_C2 = range(4)
ROUTE_ROWS = 40


def _merge_kernel(alpha, ot_ref, cu_ref, x_ref, wn_ref, wc_ref, wgm_ref, wo_ref, g1_ref, b1_ref,
                  wrt_ref, brt_ref, h1_ref, routet_ref, cnt_ref, lt_s):
    i = pl.program_id(0)
    tm, d = x_ref.shape
    iota = lambda shape, dim: lax.broadcasted_iota(jnp.int32, shape, dim)
    sq = (MOE_TILE, MOE_TILE)
    earlier_token = jnp.where(iota(sq, 0) < iota(sq, 1), 1.0, 0.0).astype(BF16)
    lower_expert = jnp.where(iota((LANES, LANES), 0) < iota((LANES, LANES), 1), 1.0, 0.0).astype(BF16)

    @pl.when(i == 0)
    def _():
        lt_s[...] = jnp.zeros(lt_s.shape, F32)

    e1, e2, c1, c2 = _route(lt_s[...])
    row = iota((LANES, tm), 0)
    pick1 = row == e1
    pick2 = row == e2
    one = lambda m: jnp.where(m, 1.0, 0.0).astype(BF16)
    ones8 = jnp.ones((8, MOE_TILE), BF16)
    subs = [slice(k * MOE_TILE, (k + 1) * MOE_TILE) for k in range(tm // MOE_TILE)]
    oh1 = [one(pick1[:, c]) for c in subs]
    oh2 = [one(pick2[:, c]) for c in subs]
    oh = [a + b for a, b in zip(oh1, oh2)]

    x = x_ref[...]
    xb = x.astype(BF16)
    y_nsa = _dot_tn(ot_ref[...], wn_ref[...])
    y_conv = _dot(cu_ref[...], wc_ref[...])
    merged = (_sigmoid(_dot(xb, wgm_ref[:, 0:d])) * y_nsa
              + _sigmoid(_dot(xb, wgm_ref[:, d:2 * d])) * y_conv)
    mix = _dot(merged.astype(BF16), wo_ref[...])

    before = [_dot(o, earlier_token) for o in oh]
    cnt = [_dot_nt(ones8, o) for o in oh]
    chunks = [jnp.floor((c + (CHUNK - 1.0)) * (1.0 / CHUNK)).astype(BF16) for c in cnt]
    run_start = [(_dot(c, lower_expert) * CHUNK).astype(BF16) for c in chunks]
    start1 = [_dot(r, o)[0:1, :] for r, o in zip(run_start, oh1)]
    start2 = [_dot(r, o)[0:1, :] for r, o in zip(run_start, oh2)]
    recs = []
    for k, c in enumerate(subs):
        slot1 = start1[k] + jnp.sum(jnp.where(pick1[:, c], before[k], 0.0), axis=0, keepdims=True)
        slot2 = start2[k] + jnp.sum(jnp.where(pick2[:, c], before[k], 0.0), axis=0, keepdims=True)
        recs.append(jnp.concatenate([slot1, slot2, c1[:, c], c2[:, c], jnp.zeros((4, MOE_TILE), F32)], axis=0))
        cnt_ref[k] = cnt[k]
    routet_ref[...] = jnp.concatenate(recs, axis=1)

    h1 = _layer_norm(alpha * x + mix, g1_ref[...], b1_ref[...])
    h1_ref[...] = h1
    lt_s[...] = (_dot_nt(wrt_ref[...], h1.astype(BF16)) + brt_ref[...])[0:ROUTE_ROWS, :]


def _merge(ot, cu, x2, wn, wc, wgm, wo, g1, b1, wrt, brt, alpha):
    T, D = x2.shape
    tm = brt.shape[1]
    sub = tm // MOE_TILE
    nt = T // tm
    full = lambda a: pl.BlockSpec(a.shape, lambda i: (0,) * a.ndim)
    cur = lambda i: jnp.minimum(i, nt - 1)
    prev = lambda i: jnp.maximum(i - 1, 0)
    return pl.pallas_call(
        functools.partial(_merge_kernel, alpha),
        grid=(nt + 1,),
        in_specs=[
            pl.BlockSpec((Q_DIM, tm), lambda i: (0, cur(i))),
            pl.BlockSpec((tm, CONV_DIM), lambda i: (cur(i), 0)),
            pl.BlockSpec((tm, D), lambda i: (cur(i), 0)),
            full(wn), full(wc), full(wgm), full(wo), full(g1), full(b1), full(wrt), full(brt),
        ],
        out_specs=(pl.BlockSpec((tm, D), lambda i: (cur(i), 0)),
                   pl.BlockSpec((8, tm), lambda i: (0, prev(i))),
                   pl.BlockSpec((sub, 8, LANES), lambda i: (prev(i), 0, 0))),
        out_shape=(jax.ShapeDtypeStruct((T, D), F32),
                   jax.ShapeDtypeStruct((8, T), F32),
                   jax.ShapeDtypeStruct((T // MOE_TILE, 8, LANES), F32)),
        scratch_shapes=[pltpu.VMEM((ROUTE_ROWS, tm), F32)],
        compiler_params=_cparams(1),
        name="merge",
    )(ot, cu, x2, wn, wc, wgm, wo, g1, b1, wrt, brt)


def _chunk_rows(q):
    return pl.ds(pl.multiple_of(q * CHUNK, CHUNK), CHUNK)


def _wait_chunks(n, copy_of_rows):
    size = MAX_CHUNKS
    while size >= 1:
        @pl.when((n & size) != 0)
        def _(size=size):
            copy_of_rows(size * CHUNK).wait()
        size //= 2


def _dispatch_kernel(tab_ref, tab_m1_ref, tab_m2_ref, tails_ref, nused_ref, h1_ref, routet_ref, xs_ref,
                     buf, zero_s, sem):
    i = pl.program_id(0)
    last = pl.num_programs(0) - 1
    tm = h1_ref.shape[0]
    slot = i % 2

    def drain(tab, slot_):
        _wait_chunks(tab[0, 0, TAB_COUNT],
                     lambda rows: pltpu.make_async_copy(buf.at[slot_, pl.ds(0, rows)], xs_ref.at[pl.ds(0, rows)],
                                                        sem.at[slot_]))

    s1 = routet_ref[R_SLOT1:R_SLOT1 + 1, :].astype(jnp.int32)
    s2 = routet_ref[R_SLOT2:R_SLOT2 + 1, :].astype(jnp.int32)
    r_io = lax.broadcasted_iota(jnp.int32, (LOCAL_ROWS, tm), 0)
    perm = jnp.where(r_io == s1, 1.0, jnp.where(r_io == s2, 1.0, 0.0)).astype(BF16)
    srt = _dot(perm, h1_ref[...].astype(BF16)).astype(BF16)

    @pl.when(i >= 2)
    def _():
        drain(tab_m2_ref, slot)

    buf[slot] = srt

    def start(q, c):
        dst = pl.multiple_of(tab_ref[0, 0, q], CHUNK)
        pltpu.make_async_copy(buf.at[slot, _chunk_rows(q)], xs_ref.at[pl.ds(dst, CHUNK)], sem.at[slot]).start()
        return c

    lax.fori_loop(0, tab_ref[0, 0, TAB_COUNT], start, 0)

    @pl.when(i == last)
    def _():
        drain(tab_ref, slot)

    @pl.when(jnp.logical_and(i == last, i >= 1))
    def _():
        drain(tab_m1_ref, 1 - slot)

    @pl.when(i == last)
    def _():
        zero_s[...] = jnp.zeros(zero_s.shape, zero_s.dtype)

        def tail_copy(q):
            dst = pl.multiple_of(jnp.maximum(tails_ref[q], 0), CHUNK)
            return pltpu.make_async_copy(zero_s.at[pl.ds(0, CHUNK)], xs_ref.at[pl.ds(dst, CHUNK)], sem.at[2])

        def tile_copy(j):
            dst = pl.multiple_of(j * EXPERT_TILE, EXPERT_TILE)
            return pltpu.make_async_copy(zero_s, xs_ref.at[pl.ds(dst, EXPERT_TILE)], sem.at[2])

        def start_tail(q, c):
            @pl.when(tails_ref[q] >= 0)
            def _():
                tail_copy(q).start()
            return c

        def wait_tail(q, c):
            @pl.when(tails_ref[q] >= 0)
            def _():
                tail_copy(q).wait()
            return c

        def start_tile(j, c):
            tile_copy(j).start()
            return c

        def wait_tile(j, c):
            tile_copy(j).wait()
            return c

        n_tiles = xs_ref.shape[0] // EXPERT_TILE
        lax.fori_loop(0, tails_ref.shape[0], start_tail, 0)
        lax.fori_loop(nused_ref[0], n_tiles, start_tile, 0)
        lax.fori_loop(0, tails_ref.shape[0], wait_tail, 0)
        lax.fori_loop(nused_ref[0], n_tiles, wait_tile, 0)


def _dispatch(tab, tails, n_used, h1, route_t, n_rows):
    T, D = h1.shape
    tm = MOE_TILE
    tab_blk = lambda back: pl.BlockSpec((1, 1, LANES), lambda i: (jnp.maximum(i - back, 0), 0, 0),
                                        memory_space=pltpu.SMEM)
    return pl.pallas_call(
        _dispatch_kernel,
        grid=(T // tm,),
        in_specs=[
            tab_blk(0), tab_blk(1), tab_blk(2),
            pl.BlockSpec(memory_space=pltpu.SMEM),
            pl.BlockSpec(memory_space=pltpu.SMEM),
            pl.BlockSpec((tm, D), lambda i: (i, 0)),
            pl.BlockSpec((8, tm), lambda i: (0, i)),
        ],
        out_specs=pl.BlockSpec(memory_space=pl.ANY),
        out_shape=jax.ShapeDtypeStruct((n_rows, D), BF16),
        scratch_shapes=[pltpu.VMEM((2, LOCAL_ROWS, D), BF16), pltpu.VMEM((EXPERT_TILE, D), BF16),
                        pltpu.SemaphoreType.DMA((3,))],
        compiler_params=_cparams(1),
        name="moe_dispatch",
    )(tab, tab, tab, tails, n_used, h1, route_t)


def _experts_kernel(te_ref, nu_ref, xs_ref, wg_ref, wu_ref, wd_ref, ys_ref):
    del te_ref
    j = pl.program_id(0)

    @pl.when(j < nu_ref[0])
    def _():
        xb = xs_ref[...]
        hg = _dot(xb, wg_ref[0].astype(BF16))
        h = hg * _sigmoid(hg) * _dot(xb, wu_ref[0].astype(BF16))
        y = _dot(h.astype(BF16), wd_ref[0].astype(BF16))
        ys_ref[...] = y.astype(BF16)

    @pl.when(j >= nu_ref[0])
    def _():
        ys_ref[...] = jnp.zeros(ys_ref.shape, ys_ref.dtype)


def _experts(tile_expert, n_used, xs, wg, wu, wd):
    n_rows, D = xs.shape
    hid = wg.shape[2]
    tm = EXPERT_TILE
    used = lambda j, te, nu: (jnp.minimum(j, nu[0] - 1), 0)
    grid_spec = pltpu.PrefetchScalarGridSpec(
        num_scalar_prefetch=2,
        grid=(n_rows // tm,),
        in_specs=[
            pl.BlockSpec((tm, D), used),
            pl.BlockSpec((1, D, hid), lambda j, te, nu: (te[j], 0, 0)),
            pl.BlockSpec((1, D, hid), lambda j, te, nu: (te[j], 0, 0)),
            pl.BlockSpec((1, hid, D), lambda j, te, nu: (te[j], 0, 0)),
        ],
        out_specs=pl.BlockSpec((tm, D), lambda j, te, nu: (j, 0)),
    )
    return pl.pallas_call(
        _experts_kernel,
        grid_spec=grid_spec,
        out_shape=jax.ShapeDtypeStruct((n_rows, D), BF16),
        compiler_params=_cparams(1),
        name="moe_experts",
    )(tile_expert, n_used, xs, wg, wu, wd)


def _combine_kernel(alpha, tab_ref, tab_next_ref, routet_ref, h1_ref, p_ref, pproj_ref, pgw_ref,
                    pgb_ref, g2_ref, b2_ref, ys_ref, out_ref, buf, sem):
    i = pl.program_id(0)
    tm = h1_ref.shape[0]
    slot = i % 2

    def gather(tab, slot_):
        def start(q, c):
            src = pl.multiple_of(tab[0, 0, q], CHUNK)
            pltpu.make_async_copy(ys_ref.at[pl.ds(src, CHUNK)], buf.at[slot_, _chunk_rows(q)],
                                  sem.at[slot_]).start()
            return c
        lax.fori_loop(0, tab[0, 0, TAB_COUNT], start, 0)

    @pl.when(i == 0)
    def _():
        buf[...] = jnp.zeros(buf.shape, buf.dtype)
        gather(tab_ref, 0)

    @pl.when(i + 1 < pl.num_programs(0))
    def _():
        gather(tab_next_ref, 1 - slot)

    h1 = h1_ref[...]
    ple = (_sigmoid(_dot(h1.astype(BF16), pgw_ref[...]) + pgb_ref[...])
           * _dot(p_ref[...].astype(BF16), pproj_ref[...]))

    _wait_chunks(tab_ref[0, 0, TAB_COUNT],
                 lambda rows: pltpu.make_async_copy(ys_ref.at[pl.ds(0, rows)], buf.at[slot, pl.ds(0, rows)],
                                                    sem.at[slot]))

    s1 = routet_ref[R_SLOT1:R_SLOT1 + 1, :].astype(jnp.int32)
    s2 = routet_ref[R_SLOT2:R_SLOT2 + 1, :].astype(jnp.int32)
    c1 = routet_ref[R_C1:R_C1 + 1, :]
    c2 = routet_ref[R_C2:R_C2 + 1, :]
    r_io = lax.broadcasted_iota(jnp.int32, (LOCAL_ROWS, tm), 0)
    weights = jnp.where(r_io == s1, c1, jnp.where(r_io == s2, c2, 0.0)).astype(BF16)
    ffn = _dot_tn(weights, buf[slot])
    out_ref[...] = _layer_norm(alpha * h1 + ffn + ple, g2_ref[...], b2_ref[...])


def _combine(tab, route_t, h1, p2, pproj, pgw, pgb, g2, b2, ys, alpha):
    T, D = h1.shape
    tm = MOE_TILE
    nt = T // tm
    full = lambda a: pl.BlockSpec(a.shape, lambda i: (0,) * a.ndim)
    return pl.pallas_call(
        functools.partial(_combine_kernel, alpha),
        grid=(nt,),
        in_specs=[
            pl.BlockSpec((1, 1, LANES), lambda i: (i, 0, 0), memory_space=pltpu.SMEM),
            pl.BlockSpec((1, 1, LANES), lambda i: (jnp.minimum(i + 1, nt - 1), 0, 0), memory_space=pltpu.SMEM),
            pl.BlockSpec((8, tm), lambda i: (0, i)),
            pl.BlockSpec((tm, D), lambda i: (i, 0)),
            pl.BlockSpec((tm, p2.shape[1]), lambda i: (i, 0)),
            full(pproj), full(pgw), full(pgb), full(g2), full(b2),
            pl.BlockSpec(memory_space=pl.ANY),
        ],
        out_specs=pl.BlockSpec((tm, D), lambda i: (i, 0)),
        out_shape=jax.ShapeDtypeStruct((T, D), F32),
        scratch_shapes=[pltpu.VMEM((2, LOCAL_ROWS, D), BF16), pltpu.SemaphoreType.DMA((2,))],
        compiler_params=_cparams(1),
        name="moe_combine",
    )(tab, tab, route_t, h1, p2, pproj, pgw, pgb, g2, b2, ys)


def _moe_plan(cnt_tiles, n_expert_tiles):
    cnt = cnt_tiles[:, 0, :N_EXPERTS].astype(jnp.int32)
    cnt8 = (cnt + CHUNK - 1) // CHUNK * CHUNK
    lend = jnp.cumsum(cnt8, axis=1)
    lstart = lend - cnt8
    gend = jnp.cumsum(cnt8, axis=0)
    region = (gend[-1] + EXPERT_TILE - 1) // EXPERT_TILE * EXPERT_TILE
    oend = jnp.cumsum(region)
    gstart = (oend - region)[None, :] + gend - cnt8
    q8 = jnp.arange(MAX_CHUNKS, dtype=jnp.int32) * CHUNK
    eq = jnp.minimum(jnp.sum((lend[:, None, :] <= q8[None, :, None]).astype(jnp.int32), axis=-1), N_EXPERTS - 1)
    shift = jnp.sum(jnp.where(eq[:, :, None] == jnp.arange(N_EXPERTS)[None, None, :],
                              (gstart - lstart)[:, None, :], 0), axis=-1)
    dstq = shift + q8[None, :]
    nt = cnt.shape[0]
    tab = jnp.concatenate([dstq, jnp.zeros((nt, TAB_COUNT - MAX_CHUNKS), jnp.int32), lend[:, -1:] // CHUNK], axis=1)
    tile_row = jnp.arange(n_expert_tiles, dtype=jnp.int32) * EXPERT_TILE
    tile_expert = jnp.minimum(jnp.sum((oend[None, :] <= tile_row[:, None]).astype(jnp.int32), axis=1),
                              N_EXPERTS - 1)
    n_used = (oend[-1] // EXPERT_TILE).reshape(1)
    c = jnp.arange(EXPERT_TILE // CHUNK, dtype=jnp.int32)[None, :] * CHUNK
    tail_start = (oend - region + gend[-1])[:, None] + c
    tails = jnp.where(tail_start < oend[:, None], tail_start, -1).reshape(-1)
    return tab.reshape(nt, 1, LANES), tails, tile_expert, n_used


def _split_w_in(w_in, D):
    sizes = [Q_DIM] + [KV_DIM] * 6 + [N_HEADS * N_BRANCH] + [CONV_DIM] * 3 + [D] * 2
    offs = np.concatenate([[0], np.cumsum(sizes)])
    names = ["q", "k_cmp", "v_cmp", "k_slc", "v_slc", "k_win", "v_win", "g_nsa",
             "conv_b", "conv_c", "conv_h", "g_m_nsa", "g_m_conv"]
    return {n: w_in[:, int(offs[k]):int(offs[k + 1])] for k, n in enumerate(names)}


def _layer(x2, p2, B, seq, depth, w_in, cmp_pe, cmp_w1, cmp_b1, cmp_w2, cmp_b2, conv_w, w_nsa_out,
           w_conv_out, w_o, ln1_g, ln1_b, rg_w, rg_b, re_w, re_b, e_wg, e_wu, e_wd, ple_proj,
           ple_gate_w, ple_gate_b, ln2_g, ln2_b):
    T, D = x2.shape
    alpha = (2.0 * depth) ** 0.25
    w = _split_w_in(w_in, D)
    wtok = jnp.concatenate([w["k_cmp"], w["v_cmp"], w["k_slc"], w["k_win"],
                            w["conv_b"], w["conv_c"], w["conv_h"]], axis=1).astype(BF16)
    gcols = w["g_nsa"].reshape(D, N_GROUPS_KV, HEADS_PER_GROUP * N_BRANCH)
    gcols = jnp.pad(gcols, ((0, 0), (0, 0), (0, GATE_ROWS - HEADS_PER_GROUP * N_BRANCH)))
    wt = jnp.concatenate([w["q"], gcols.reshape(D, N_GROUPS_KV * GATE_ROWS), w["v_slc"], w["v_win"]],
                         axis=1).astype(BF16).T
    kcmp, vcmp, kslc, kwin, cu, qt, gt, vslct, vwint = _in_proj(x2, wtok, wt, conv_w, seq)

    half = CMP_BLOCK // 2
    n_chunks = seq // CMP_STRIDE
    n_cmp = (seq - CMP_BLOCK) // CMP_STRIDE + 1
    hidden = cmp_w1.shape[-1]
    eye = jnp.eye(N_GROUPS_KV, dtype=F32)
    w1r = cmp_w1.reshape(2, CMP_BLOCK, HEAD_DIM, hidden)
    expand = lambda m: jnp.einsum("ildh,gk->ilgdkh", m, eye).reshape(
        2, half * KV_DIM, N_GROUPS_KV * hidden).astype(BF16)
    wa, wb = expand(w1r[:, :half]), expand(w1r[:, half:])
    pe = jnp.broadcast_to(cmp_pe.reshape(2, 1, CMP_BLOCK * HEAD_DIM), (2, 8, CMP_BLOCK * HEAD_DIM)).astype(BF16)
    b1t = jnp.tile(cmp_b1.reshape(2, 1, hidden), (1, 1, N_GROUPS_KV))
    w2b = jnp.einsum("ihd,gk->ighkd", cmp_w2, eye).reshape(2, N_GROUPS_KV * hidden, KV_DIM).astype(BF16)
    b2t = jnp.tile(cmp_b2.reshape(2, 1, HEAD_DIM), (1, 1, N_GROUPS_KV))
    kc, vct = _compress(kcmp.reshape(B, n_chunks, CMP_STRIDE * KV_DIM),
                        vcmp.reshape(B, n_chunks, CMP_STRIDE * KV_DIM),
                        wa, wb, pe, cmp_w1.astype(BF16), b1t, w2b, b2t)

    n_sel = seq // SEL_BLOCK
    c_start = np.arange(n_chunks) * CMP_STRIDE
    s_start = np.arange(n_sel) * SEL_BLOCK
    overlap = ((c_start[None, :] <= s_start[:, None] + SEL_BLOCK - 1)
               & (c_start[None, :] + CMP_BLOCK - 1 >= s_start[:, None])).astype(np.float32)
    ocmpt, selb = _cmp_attn(qt, kc, vct, gt, jnp.asarray(overlap, BF16), B, seq, n_cmp)
    ot = _nsa_attn(qt, kslc, kwin, vslct, vwint, selb, gt, ocmpt, B, seq)

    wgm = jnp.concatenate([w["g_m_nsa"], w["g_m_conv"]], axis=1).astype(BF16)
    merge_tile = 512
    wrt = jnp.pad(jnp.concatenate([re_w, rg_w], axis=1).T, ((0, LANES - N_EXPERTS - N_EXPERT_GROUPS), (0, 0)))
    brt = jnp.pad(jnp.concatenate([re_b, rg_b]), (0, LANES - N_EXPERTS - N_EXPERT_GROUPS))
    brt = jnp.broadcast_to(brt[:, None], (LANES, merge_tile))
    h1, route_t, cnt_tiles = _merge(ot, cu, x2, w_nsa_out.astype(BF16), w_conv_out.astype(BF16), wgm,
                                    w_o.astype(BF16), ln1_g.reshape(1, D), ln1_b.reshape(1, D),
                                    wrt.astype(BF16), brt, alpha)
    n_tok_tiles = T // MOE_TILE
    max_rows = 2 * T + n_tok_tiles * N_EXPERTS * (CHUNK - 1) + N_EXPERTS * (EXPERT_TILE - 1)
    n_expert_tiles = -(-max_rows // EXPERT_TILE)
    tab, tails, tile_expert, n_used = _moe_plan(cnt_tiles, n_expert_tiles)
    xs = _dispatch(tab, tails, n_used, h1, route_t, n_expert_tiles * EXPERT_TILE)
    ys = _experts(tile_expert, n_used, xs, e_wg, e_wu, e_wd)
    return _combine(tab, route_t, h1, p2, ple_proj.astype(BF16), ple_gate_w.astype(BF16),
                    ple_gate_b.reshape(1, D), ln2_g.reshape(1, D), ln2_b.reshape(1, D), ys, alpha)


def kernel(x, p, w_in, cmp_pe, cmp_w1, cmp_b1, cmp_w2, cmp_b2, conv_w, w_nsa_out, w_conv_out, w_o, ln1_g, ln1_b, router_group_w, router_group_b, router_expert_w, router_expert_b, expert_w_gate, expert_w_up, expert_w_down, ple_proj, ple_gate_w, ple_gate_b, ln2_g, ln2_b):
    B, seq, D = x.shape
    depth = w_in.shape[0]
    x2 = x.reshape(B * seq, D)
    for i in range(depth):
        x2 = _layer(x2, p[i].reshape(B * seq, -1), B, seq, depth, w_in[i], cmp_pe[i], cmp_w1[i], cmp_b1[i],
                    cmp_w2[i], cmp_b2[i], conv_w[i], w_nsa_out[i], w_conv_out[i], w_o[i], ln1_g[i], ln1_b[i],
                    router_group_w[i], router_group_b[i], router_expert_w[i], router_expert_b[i],
                    expert_w_gate[i], expert_w_up[i], expert_w_down[i], ple_proj[i], ple_gate_w[i],
                    ple_gate_b[i], ln2_g[i], ln2_b[i])
    return x2.reshape(B, seq, D)
```

```python
import functools

import jax
import jax.numpy as jnp
import numpy as np
from jax import lax
from jax.experimental import pallas as pl
from jax.experimental.pallas import tpu as pltpu

F32 = jnp.float32
BF16 = jnp.bfloat16

N_HEADS = 8
N_GROUPS_KV = 2
HEADS_PER_GROUP = N_HEADS // N_GROUPS_KV
HEAD_DIM = 64
Q_DIM = N_HEADS * HEAD_DIM
KV_DIM = N_GROUPS_KV * HEAD_DIM
N_BRANCH = 3
CMP_BLOCK = 32
CMP_STRIDE = 16
SEL_BLOCK = 64
SEL_TOPK = 16
WINDOW = 512
CONV_DIM = 512
N_EXPERT_GROUPS = 4
EXPERTS_PER_GROUP = 8
N_EXPERTS = N_EXPERT_GROUPS * EXPERTS_PER_GROUP
ATTN_SCALE = HEAD_DIM ** -0.5
LOG2_E = 1.4426950408889634
FORCE_BONUS = 1e4
NEG_INF = -1e30
LN_EPS = 1e-5

LANES = 128
ATT_TILE = 256
GATE_ROWS = 16
SUM_ROWS = 16
VMEM_LIMIT = 48 * 1024 * 1024

MOE_TILE = 256
EXPERT_TILE = 1024
CHUNK = 16
LOCAL_ROWS = -(-(2 * MOE_TILE + N_EXPERTS * (CHUNK - 1)) // ATT_TILE) * ATT_TILE
MAX_CHUNKS = LOCAL_ROWS // CHUNK
TAB_COUNT = LANES - 1


def _cparams(n_axes):
    return pltpu.CompilerParams(dimension_semantics=("arbitrary",) * n_axes,
                                vmem_limit_bytes=VMEM_LIMIT)


def _dot(a, b):
    return jnp.dot(a, b, preferred_element_type=F32)


def _dot_nt(a, b):
    return lax.dot_general(a, b, (((1,), (1,)), ((), ())), preferred_element_type=F32)


def _dot_tn(a, b):
    return lax.dot_general(a, b, (((0,), (0,)), ((), ())), preferred_element_type=F32)


def _sigmoid(v):
    return 1.0 / (1.0 + jnp.exp(-v))


def _layer_norm(v, g, b):
    mu = jnp.mean(v, axis=-1, keepdims=True)
    d = v - mu
    var = jnp.mean(d * d, axis=-1, keepdims=True)
    return d * lax.rsqrt(var + LN_EPS) * g + b


def _inproj_kernel(seq, x_ref, xprev_ref, wtok_ref, wt_ref, convw_ref,
                   kcmp_ref, vcmp_ref, kslc_ref, kwin_ref, cu_ref, qt_ref, gt_ref, vslct_ref, vwint_ref, cmp_s):
    i = pl.program_id(0)
    tm = x_ref.shape[0]
    xb = x_ref[...].astype(BF16)

    c0 = 4 * KV_DIM
    kvf = _dot(xb, wtok_ref[:, 0:c0])
    kv = kvf.astype(BF16)
    cmp_s[0] = kvf[:, 0:KV_DIM]
    cmp_s[1] = kvf[:, KV_DIM:2 * KV_DIM]
    for l in range(CMP_STRIDE):
        rows = pl.ds(l, tm // CMP_STRIDE, stride=CMP_STRIDE)
        kcmp_ref[:, l * KV_DIM:(l + 1) * KV_DIM] = cmp_s[0, rows, :].astype(BF16)
        vcmp_ref[:, l * KV_DIM:(l + 1) * KV_DIM] = cmp_s[1, rows, :].astype(BF16)
    kwin_ref[...] = kv[:, 3 * KV_DIM:4 * KV_DIM]
    n_sel = seq // SEL_BLOCK
    blk = ((i * tm + lax.broadcasted_iota(jnp.int32, (tm, KV_DIM), 0)) // SEL_BLOCK) % n_sel
    onehot = jnp.where(lax.broadcasted_iota(jnp.int32, (tm, KV_DIM), 1) == blk, 1.0, 0.0).astype(BF16)
    kslc_ref[...] = jnp.concatenate([kv[:, 2 * KV_DIM:3 * KV_DIM], onehot], axis=1)
    conv = _dot(xb, wtok_ref[:, c0:c0 + 3 * CONV_DIM])
    cb = conv[:, 0:CONV_DIM]
    u = conv[:, CONV_DIM:2 * CONV_DIM] * conv[:, 2 * CONV_DIM:3 * CONV_DIM]
    xpb = xprev_ref[...].astype(BF16)
    up = (_dot(xpb, wtok_ref[:, c0 + CONV_DIM:c0 + 2 * CONV_DIM])
          * _dot(xpb, wtok_ref[:, c0 + 2 * CONV_DIM:c0 + 3 * CONV_DIM]))
    up = jnp.where(i % (seq // tm) == 0, 0.0, up)
    row = lax.broadcasted_iota(jnp.int32, (tm, CONV_DIM), 0)
    u1 = jnp.where(row == 0, up[7:8, :], pltpu.roll(u, 1, 0))
    u2 = jnp.where(row == 0, up[6:7, :], jnp.where(row == 1, up[7:8, :], pltpu.roll(u, 2, 0)))
    w = convw_ref[...]
    uc = w[0:1, :] * u2 + w[1:2, :] * u1 + w[2:3, :] * u
    cu_ref[...] = (cb * uc).astype(BF16)

    zt = _dot_nt(wt_ref[...], xb)
    qt_ref[...] = (zt[0:Q_DIM, :] * (ATTN_SCALE * LOG2_E)).astype(BF16)
    r0 = Q_DIM
    gt_ref[...] = _sigmoid(zt[r0:r0 + 2 * GATE_ROWS, :])
    r0 += 2 * GATE_ROWS
    vs = zt[r0:r0 + KV_DIM, :].astype(BF16)
    vw = zt[r0 + KV_DIM:r0 + 2 * KV_DIM, :].astype(BF16)
    for c in range(tm // ATT_TILE):
        vslct_ref[c] = vs[:, c * ATT_TILE:(c + 1) * ATT_TILE]
        vwint_ref[c] = vw[:, c * ATT_TILE:(c + 1) * ATT_TILE]


def _in_proj(x2, wtok, wt, conv_w, seq):
    T, D = x2.shape
    tm = 1024
    nt = T // tm
    n_tok = wtok.shape[1]
    n_t = wt.shape[0]
    row_blk = lambda n: pl.BlockSpec((tm, n), lambda i: (i, 0))
    out_shape = (
        jax.ShapeDtypeStruct((T // CMP_STRIDE, CMP_STRIDE * KV_DIM), BF16),
        jax.ShapeDtypeStruct((T // CMP_STRIDE, CMP_STRIDE * KV_DIM), BF16),
        jax.ShapeDtypeStruct((T, 2 * KV_DIM), BF16),
        jax.ShapeDtypeStruct((T, KV_DIM), BF16),
        jax.ShapeDtypeStruct((T, CONV_DIM), BF16),
        jax.ShapeDtypeStruct((Q_DIM, T), BF16),
        jax.ShapeDtypeStruct((2 * GATE_ROWS, T), F32),
        jax.ShapeDtypeStruct((T // ATT_TILE, KV_DIM, ATT_TILE), BF16),
        jax.ShapeDtypeStruct((T // ATT_TILE, KV_DIM, ATT_TILE), BF16),
    )
    vt_blk = pl.BlockSpec((tm // ATT_TILE, KV_DIM, ATT_TILE), lambda i: (i, 0, 0))
    chunk_blk = pl.BlockSpec((tm // CMP_STRIDE, CMP_STRIDE * KV_DIM), lambda i: (i, 0))
    return pl.pallas_call(
        functools.partial(_inproj_kernel, seq),
        grid=(nt,),
        in_specs=[
            row_blk(D),
            pl.BlockSpec((8, D), lambda i: (jnp.maximum(i * (tm // 8) - 1, 0), 0)),
            pl.BlockSpec((D, n_tok), lambda i: (0, 0)),
            pl.BlockSpec((n_t, D), lambda i: (0, 0)),
            pl.BlockSpec((3, CONV_DIM), lambda i: (0, 0)),
        ],
        out_specs=(
            chunk_blk, chunk_blk, row_blk(2 * KV_DIM), row_blk(KV_DIM), row_blk(CONV_DIM),
            pl.BlockSpec((Q_DIM, tm), lambda i: (0, i)),
            pl.BlockSpec((2 * GATE_ROWS, tm), lambda i: (0, i)),
            vt_blk, vt_blk,
        ),
        out_shape=out_shape,
        scratch_shapes=[pltpu.VMEM((2, tm, KV_DIM), F32)],
        compiler_params=_cparams(1),
        name="in_proj",
    )(x2, x2, wtok, wt, conv_w)


def _compress_kernel(kin_ref, vin_ref, wa_ref, wb_ref, pe_ref, w1_ref, b1_ref, w2_ref, b2_ref,
                     kc_ref, vct_ref):
    def one(idx, in_ref):
        c = in_ref[0]
        a = _dot(c, wa_ref[idx])
        b = _dot(c, wb_ref[idx])
        n = b.shape[0]
        peb = _dot(pe_ref[idx], w1_ref[idx])[0:1, :]
        bias = jnp.concatenate([peb, peb], axis=1) + b1_ref[idx]
        h = a + pltpu.roll(b, n - 1, 0) + bias
        return _dot(jax.nn.gelu(h).astype(BF16), w2_ref[idx]) + b2_ref[idx]

    kc_ref[0] = one(0, kin_ref).astype(BF16)
    vct_ref[0] = one(1, vin_ref).T.astype(BF16)


def _compress(kcmp3, vcmp3, wa, wb, pe, w1, b1, w2, b2):
    B, nch, width = kcmp3.shape
    hid2 = wa.shape[2]
    full = lambda a: pl.BlockSpec(a.shape, lambda b: (0,) * a.ndim)
    in_blk = pl.BlockSpec((1, nch, width), lambda b: (b, 0, 0))
    out_blk = pl.BlockSpec((1, nch, KV_DIM), lambda b: (b, 0, 0))
    return pl.pallas_call(
        _compress_kernel,
        grid=(B,),
        in_specs=[in_blk, in_blk, full(wa), full(wb), full(pe), full(w1), full(b1), full(w2), full(b2)],
        out_specs=(out_blk, pl.BlockSpec((1, KV_DIM, nch), lambda b: (b, 0, 0))),
        out_shape=(jax.ShapeDtypeStruct((B, nch, KV_DIM), BF16),
                   jax.ShapeDtypeStruct((B, KV_DIM, nch), BF16)),
        compiler_params=_cparams(1),
        name="compress",
    )(kcmp3, vcmp3, wa, wb, pe, w1, b1, w2, b2)


def _group_q(qt_ref, hh, g_is0):
    qh = qt_ref[hh * HEAD_DIM:(hh + 1) * HEAD_DIM, :]
    z = jnp.zeros_like(qh)
    return jnp.concatenate([qh, z], axis=0) if g_is0 else jnp.concatenate([z, qh], axis=0)


def _cmpattn_kernel(n_cmp, qt_ref, kc_ref, vct_ref, gt_ref, ovt_ref, ocmpt_ref, selb_ref):
    i = pl.program_id(1)
    tq = qt_ref.shape[1]
    nc = kc_ref.shape[1]
    n_sel = ovt_ref.shape[0]
    kc = kc_ref[0]
    t_n = i * tq + lax.broadcasted_iota(jnp.int32, (nc, tq), 1)
    n_io = lax.broadcasted_iota(jnp.int32, (nc, tq), 0)
    blk_end = jnp.where(n_io < n_cmp, n_io * CMP_STRIDE + CMP_BLOCK - 1, jnp.int32(2 ** 30))
    vis = blk_end <= t_n
    j_io = lax.broadcasted_iota(jnp.int32, (n_sel, tq), 0)
    t_j = i * tq + lax.broadcasted_iota(jnp.int32, (n_sel, tq), 1)
    cur = t_j // SEL_BLOCK
    bonus = jnp.where(j_io == 0, FORCE_BONUS,
                      jnp.where(j_io == cur, FORCE_BONUS, jnp.where(j_io == cur - 1, FORCE_BONUS, 0.0)))
    valid = j_io * SEL_BLOCK <= t_j
    for g in range(N_GROUPS_KV):
        imp = jnp.zeros((n_sel, tq), F32)
        for hh in range(HEADS_PER_GROUP):
            h = g * HEADS_PER_GROUP + hh
            s = _dot(kc, _group_q(qt_ref, h, g == 0))
            s = jnp.where(vis, s, NEG_INF)
            m = jnp.max(s, axis=0, keepdims=True)
            p = jnp.where(vis, jnp.exp2(s - m), 0.0)
            l = jnp.sum(p, axis=0, keepdims=True)
            pr = (p * jnp.where(l > 0.0, 1.0 / l, 0.0)).astype(BF16)
            o = _dot(vct_ref[0, g * HEAD_DIM:(g + 1) * HEAD_DIM, :], pr)
            gate = gt_ref[g * GATE_ROWS + hh * N_BRANCH:g * GATE_ROWS + hh * N_BRANCH + 1, :]
            ocmpt_ref[h * HEAD_DIM:(h + 1) * HEAD_DIM, :] = (o * gate).astype(BF16)
            imp = imp + _dot(ovt_ref[...], pr)
        score = jnp.where(valid, imp + bonus, -FORCE_BONUS)
        cnt = jnp.zeros((n_sel, tq), F32)
        for jp in range(n_sel):
            r = score[jp:jp + 1, :]
            cnt = cnt + jnp.where(r > score, 1.0, jnp.where(r == score, jnp.where(j_io > jp, 1.0, 0.0), 0.0))
        selb_ref[0, g] = jnp.where(cnt < float(min(SEL_TOPK, n_sel)), 0.0, NEG_INF).astype(BF16)


def _cmp_attn(qt, kc, vct, gt, ovt, B, seq, n_cmp):
    T = qt.shape[1]
    tq = ATT_TILE
    nq = seq // tq
    n_sel = ovt.shape[0]
    nc = kc.shape[1]
    return pl.pallas_call(
        functools.partial(_cmpattn_kernel, n_cmp),
        grid=(B, nq),
        in_specs=[
            pl.BlockSpec((Q_DIM, tq), lambda b, i: (0, b * nq + i)),
            pl.BlockSpec((1, nc, KV_DIM), lambda b, i: (b, 0, 0)),
            pl.BlockSpec((1, KV_DIM, nc), lambda b, i: (b, 0, 0)),
            pl.BlockSpec((2 * GATE_ROWS, tq), lambda b, i: (0, b * nq + i)),
            pl.BlockSpec(ovt.shape, lambda b, i: (0, 0)),
        ],
        out_specs=(
            pl.BlockSpec((Q_DIM, tq), lambda b, i: (0, b * nq + i)),
            pl.BlockSpec((1, N_GROUPS_KV, n_sel, tq), lambda b, i: (b * nq + i, 0, 0, 0)),
        ),
        out_shape=(jax.ShapeDtypeStruct((Q_DIM, T), BF16),
                   jax.ShapeDtypeStruct((B * nq, N_GROUPS_KV, n_sel, tq), BF16)),
        compiler_params=_cparams(2),
        name="cmp_attn",
    )(qt, kc, vct, gt, ovt)


def _nsa_kernel(qt_ref, kslc_ref, kwin_ref, vslct_ref, vwint_ref, selb_ref, gt_ref, ocmpt_ref, causal_ref,
                window_ref, ot_ref, q_s, s_s, m_s, acc_s):
    i = pl.program_id(1)
    tq = qt_ref.shape[1]
    tile = ATT_TILE
    pair = 2 * tile
    n_sel = selb_ref.shape[2]
    groups = range(N_GROUPS_KV)
    gq = HEADS_PER_GROUP * HEAD_DIM

    for g in groups:
        for hh in range(HEADS_PER_GROUP):
            lanes = slice(hh * tq, (hh + 1) * tq)
            qh = qt_ref[g * gq + hh * HEAD_DIM:g * gq + (hh + 1) * HEAD_DIM, :]
            zero = jnp.zeros_like(qh)
            q_s[g, 0:KV_DIM, lanes] = jnp.concatenate([qh, zero] if g == 0 else [zero, qh], axis=0)
            q_s[g, KV_DIM:KV_DIM + n_sel, lanes] = selb_ref[0, g]
            q_s[g, KV_DIM + n_sel:2 * KV_DIM, lanes] = jnp.zeros((KV_DIM - n_sel, tq), BF16)

    def v_rows(ref, g, tiles):
        vt = jnp.concatenate([ref[j, g * HEAD_DIM:(g + 1) * HEAD_DIM, :] for j in tiles], axis=1)
        return jnp.concatenate([vt, jnp.ones((SUM_ROWS, vt.shape[1]), BF16)], axis=0)

    def sel_scores(g, jj):
        keys = kslc_ref[pl.ds(pl.multiple_of(jj * pair, pair), pair), :]
        return _dot(keys, q_s[g])

    def sel_update(g, s, jj):
        m_old = m_s[g]
        m_new = jnp.maximum(m_old, jnp.max(s, axis=0, keepdims=True))
        p = jnp.exp2(s - m_new).astype(BF16)
        acc_s[g] = jnp.exp2(m_old - m_new) * acc_s[g] + _dot(v_rows(vslct_ref, g, (2 * jj, 2 * jj + 1)), p)
        m_s[g] = m_new

    m_s[...] = jnp.full(m_s.shape, NEG_INF, F32)
    acc_s[...] = jnp.zeros(acc_s.shape, F32)
    n_full = i // 2
    for g in groups:
        s_s[g] = sel_scores(g, 0)

    def sel_body(jj, carry):
        for g in groups:
            s = s_s[g]
            s_s[g] = sel_scores(g, jj + 1)
            sel_update(g, s, jj)
        return carry

    lax.fori_loop(0, n_full, sel_body, 0)

    j0 = jnp.maximum(i - WINDOW // tile, 0)
    n_win = WINDOW + tile
    for g in groups:
        sel_update(g, s_s[g] + causal_ref[0], n_full)
        o_slc = acc_s[g, 0:HEAD_DIM, :] * (1.0 / acc_s[g, HEAD_DIM:HEAD_DIM + 1, :])
        s = _dot(kwin_ref[pl.ds(pl.multiple_of(j0 * tile, tile), n_win), :], q_s[g, 0:KV_DIM, :])
        s = s + window_ref[0]
        p = jnp.exp2(s - jnp.max(s, axis=0, keepdims=True)).astype(BF16)
        ow = _dot(v_rows(vwint_ref, g, (j0, j0 + 1, j0 + 2)), p)
        o_win = ow[0:HEAD_DIM, :] * (1.0 / ow[HEAD_DIM:HEAD_DIM + 1, :])
        for hh in range(HEADS_PER_GROUP):
            lanes = slice(hh * tq, (hh + 1) * tq)
            rows = slice(g * gq + hh * HEAD_DIM, g * gq + (hh + 1) * HEAD_DIM)
            gate = g * GATE_ROWS + hh * N_BRANCH
            o = (ocmpt_ref[rows, :].astype(F32) + gt_ref[gate + 1:gate + 2, :] * o_slc[:, lanes]
                 + gt_ref[gate + 2:gate + 3, :] * o_win[:, lanes])
            ot_ref[rows, :] = o.astype(BF16)


def _nsa_attn(qt, kslc, kwin, vslct, vwint, selb, gt, ocmpt, B, seq):
    T = qt.shape[1]
    tq = ATT_TILE
    nq = seq // tq
    ktiles = seq // ATT_TILE
    n_sel = selb.shape[2]
    wide = HEADS_PER_GROUP * tq
    n_win = WINDOW + ATT_TILE
    kmq = (lax.broadcasted_iota(jnp.int32, (1, n_win, wide), 1)
           - lax.broadcasted_iota(jnp.int32, (1, n_win, wide), 2) % tq)
    par = lax.broadcasted_iota(jnp.int32, (2, 1, 1), 0)
    causal = jnp.where(kmq[:, :2 * ATT_TILE] <= par * ATT_TILE, 0.0, NEG_INF).astype(F32)
    behind = lax.broadcasted_iota(jnp.int32, (n_win // ATT_TILE, 1, 1), 0) * ATT_TILE - kmq
    window = jnp.where((behind >= 0) & (behind < WINDOW), 0.0, NEG_INF).astype(F32)
    qblk = pl.BlockSpec((Q_DIM, tq), lambda b, i: (0, b * nq + i))
    vblk = pl.BlockSpec((ktiles, KV_DIM, ATT_TILE), lambda b, i: (b, 0, 0))
    return pl.pallas_call(
        _nsa_kernel,
        grid=(B, nq),
        in_specs=[
            qblk,
            pl.BlockSpec((seq, 2 * KV_DIM), lambda b, i: (b, 0)),
            pl.BlockSpec((seq, KV_DIM), lambda b, i: (b, 0)),
            vblk, vblk,
            pl.BlockSpec((1, N_GROUPS_KV, n_sel, tq), lambda b, i: (b * nq + i, 0, 0, 0)),
            pl.BlockSpec((N_GROUPS_KV * GATE_ROWS, tq), lambda b, i: (0, b * nq + i)),
            qblk,
            pl.BlockSpec((1, 2 * ATT_TILE, wide), lambda b, i: (i % 2, 0, 0)),
            pl.BlockSpec((1, n_win, wide), lambda b, i: (jnp.minimum(i, WINDOW // ATT_TILE), 0, 0)),
        ],
        out_specs=qblk,
        out_shape=jax.ShapeDtypeStruct((Q_DIM, T), BF16),
        scratch_shapes=[
            pltpu.VMEM((N_GROUPS_KV, 2 * KV_DIM, wide), BF16),
            pltpu.VMEM((N_GROUPS_KV, 2 * ATT_TILE, wide), F32),
            pltpu.VMEM((N_GROUPS_KV, 1, wide), F32),
            pltpu.VMEM((N_GROUPS_KV, HEAD_DIM + SUM_ROWS, wide), F32),
        ],
        compiler_params=_cparams(2),
        name="nsa_attn",
    )(qt, kslc, kwin, vslct, vwint, selb, gt, ocmpt, causal, window)


def _route(lt):
    row = lax.broadcasted_iota(jnp.int32, lt.shape, 0)
    big = 2 ** 20
    col = lambda f, v: f(v, axis=0, keepdims=True)
    grp = jnp.where((row >= N_EXPERTS) & (row < N_EXPERTS + N_EXPERT_GROUPS), lt, NEG_INF)
    gmax = col(jnp.max, grp)
    g_idx = col(jnp.min, jnp.where(grp == gmax, row, big)) - N_EXPERTS
    p_group = 1.0 / col(jnp.sum, jnp.exp(grp - gmax))
    own = jnp.where((row < N_EXPERTS) & (row // EXPERTS_PER_GROUP == g_idx), lt, NEG_INF)
    l1 = col(jnp.max, own)
    e1 = col(jnp.min, jnp.where(own == l1, row, big))
    rest = jnp.where(row == e1, NEG_INF, own)
    l2 = col(jnp.max, rest)
    e2 = col(jnp.min, jnp.where(rest == l2, row, big))
    r = jnp.exp(l2 - l1)
    c1 = p_group / (1.0 + r)
    return e1, e2, c1, c1 * r


R_SLOT1, R_SLOT2, R_C1, R_C2 = range(4)
ROUTE_ROWS = 40


def _merge_kernel(alpha, ot_ref, cu_ref, x_ref, wn_ref, wc_ref, wgm_ref, wo_ref, g1_ref, b1_ref,
                  wrt_ref, brt_ref, h1_ref, routet_ref, cnt_ref, lt_s):
    i = pl.program_id(0)
    tm, d = x_ref.shape
    iota = lambda shape, dim: lax.broadcasted_iota(jnp.int32, shape, dim)
    sq = (MOE_TILE, MOE_TILE)
    earlier_token = jnp.where(iota(sq, 0) < iota(sq, 1), 1.0, 0.0).astype(BF16)
    lower_expert = jnp.where(iota((LANES, LANES), 0) < iota((LANES, LANES), 1), 1.0, 0.0).astype(BF16)

    @pl.when(i == 0)
    def _():
        lt_s[...] = jnp.zeros(lt_s.shape, F32)

    e1, e2, c1, c2 = _route(lt_s[...])
    row = iota((LANES, tm), 0)
    pick1 = row == e1
    pick2 = row == e2
    one = lambda m: jnp.where(m, 1.0, 0.0).astype(BF16)
    ones8 = jnp.ones((8, MOE_TILE), BF16)
    subs = [slice(k * MOE_TILE, (k + 1) * MOE_TILE) for k in range(tm // MOE_TILE)]
    oh1 = [one(pick1[:, c]) for c in subs]
    oh2 = [one(pick2[:, c]) for c in subs]
    oh = [a + b for a, b in zip(oh1, oh2)]

    x = x_ref[...]
    xb = x.astype(BF16)
    y_nsa = _dot_tn(ot_ref[...], wn_ref[...])
    y_conv = _dot(cu_ref[...], wc_ref[...])
    merged = (_sigmoid(_dot(xb, wgm_ref[:, 0:d])) * y_nsa
              + _sigmoid(_dot(xb, wgm_ref[:, d:2 * d])) * y_conv)
    mix = _dot(merged.astype(BF16), wo_ref[...])

    before = [_dot(o, earlier_token) for o in oh]
    cnt = [_dot_nt(ones8, o) for o in oh]
    chunks = [jnp.floor((c + (CHUNK - 1.0)) * (1.0 / CHUNK)).astype(BF16) for c in cnt]
    run_start = [(_dot(c, lower_expert) * CHUNK).astype(BF16) for c in chunks]
    start1 = [_dot(r, o)[0:1, :] for r, o in zip(run_start, oh1)]
    start2 = [_dot(r, o)[0:1, :] for r, o in zip(run_start, oh2)]
    recs = []
    for k, c in enumerate(subs):
        slot1 = start1[k] + jnp.sum(jnp.where(pick1[:, c], before[k], 0.0), axis=0, keepdims=True)
        slot2 = start2[k] + jnp.sum(jnp.where(pick2[:, c], before[k], 0.0), axis=0, keepdims=True)
        recs.append(jnp.concatenate([slot1, slot2, c1[:, c], c2[:, c], jnp.zeros((4, MOE_TILE), F32)], axis=0))
        cnt_ref[k] = cnt[k]
    routet_ref[...] = jnp.concatenate(recs, axis=1)

    h1 = _layer_norm(alpha * x + mix, g1_ref[...], b1_ref[...])
    h1_ref[...] = h1
    lt_s[...] = (_dot_nt(wrt_ref[...], h1.astype(BF16)) + brt_ref[...])[0:ROUTE_ROWS, :]


def _merge(ot, cu, x2, wn, wc, wgm, wo, g1, b1, wrt, brt, alpha):
    T, D = x2.shape
    tm = brt.shape[1]
    sub = tm // MOE_TILE
    nt = T // tm
    full = lambda a: pl.BlockSpec(a.shape, lambda i: (0,) * a.ndim)
    cur = lambda i: jnp.minimum(i, nt - 1)
    prev = lambda i: jnp.maximum(i - 1, 0)
    return pl.pallas_call(
        functools.partial(_merge_kernel, alpha),
        grid=(nt + 1,),
        in_specs=[
            pl.BlockSpec((Q_DIM, tm), lambda i: (0, cur(i))),
            pl.BlockSpec((tm, CONV_DIM), lambda i: (cur(i), 0)),
            pl.BlockSpec((tm, D), lambda i: (cur(i), 0)),
            full(wn), full(wc), full(wgm), full(wo), full(g1), full(b1), full(wrt), full(brt),
        ],
        out_specs=(pl.BlockSpec((tm, D), lambda i: (cur(i), 0)),
                   pl.BlockSpec((8, tm), lambda i: (0, prev(i))),
                   pl.BlockSpec((sub, 8, LANES), lambda i: (prev(i), 0, 0))),
        out_shape=(jax.ShapeDtypeStruct((T, D), F32),
                   jax.ShapeDtypeStruct((8, T), F32),
                   jax.ShapeDtypeStruct((T // MOE_TILE, 8, LANES), F32)),
        scratch_shapes=[pltpu.VMEM((ROUTE_ROWS, tm), F32)],
        compiler_params=_cparams(1),
        name="merge",
    )(ot, cu, x2, wn, wc, wgm, wo, g1, b1, wrt, brt)


def _chunk_rows(q):
    return pl.ds(pl.multiple_of(q * CHUNK, CHUNK), CHUNK)


def _wait_chunks(n, copy_of_rows):
    size = MAX_CHUNKS
    while size >= 1:
        @pl.when((n & size) != 0)
        def _(size=size):
            copy_of_rows(size * CHUNK).wait()
        size //= 2


def _dispatch_kernel(tab_ref, tab_m1_ref, tab_m2_ref, tails_ref, nused_ref, h1_ref, routet_ref, xs_ref,
                     buf, zero_s, sem):
    i = pl.program_id(0)
    last = pl.num_programs(0) - 1
    tm = h1_ref.shape[0]
    slot = i % 2

    def drain(tab, slot_):
        _wait_chunks(tab[0, 0, TAB_COUNT],
                     lambda rows: pltpu.make_async_copy(buf.at[slot_, pl.ds(0, rows)], xs_ref.at[pl.ds(0, rows)],
                                                        sem.at[slot_]))

    s1 = routet_ref[R_SLOT1:R_SLOT1 + 1, :].astype(jnp.int32)
    s2 = routet_ref[R_SLOT2:R_SLOT2 + 1, :].astype(jnp.int32)
    r_io = lax.broadcasted_iota(jnp.int32, (LOCAL_ROWS, tm), 0)
    perm = jnp.where(r_io == s1, 1.0, jnp.where(r_io == s2, 1.0, 0.0)).astype(BF16)
    srt = _dot(perm, h1_ref[...].astype(BF16)).astype(BF16)

    @pl.when(i >= 2)
    def _():
        drain(tab_m2_ref, slot)

    buf[slot] = srt

    def start(q, c):
        dst = pl.multiple_of(tab_ref[0, 0, q], CHUNK)
        pltpu.make_async_copy(buf.at[slot, _chunk_rows(q)], xs_ref.at[pl.ds(dst, CHUNK)], sem.at[slot]).start()
        return c

    lax.fori_loop(0, tab_ref[0, 0, TAB_COUNT], start, 0)

    @pl.when(i == last)
    def _():
        drain(tab_ref, slot)

    @pl.when(jnp.logical_and(i == last, i >= 1))
    def _():
        drain(tab_m1_ref, 1 - slot)

    @pl.when(i == last)
    def _():
        zero_s[...] = jnp.zeros(zero_s.shape, zero_s.dtype)

        def tail_copy(q):
            dst = pl.multiple_of(jnp.maximum(tails_ref[q], 0), CHUNK)
            return pltpu.make_async_copy(zero_s.at[pl.ds(0, CHUNK)], xs_ref.at[pl.ds(dst, CHUNK)], sem.at[2])

        def tile_copy(j):
            dst = pl.multiple_of(j * EXPERT_TILE, EXPERT_TILE)
            return pltpu.make_async_copy(zero_s, xs_ref.at[pl.ds(dst, EXPERT_TILE)], sem.at[2])

        def start_tail(q, c):
            @pl.when(tails_ref[q] >= 0)
            def _():
                tail_copy(q).start()
            return c

        def wait_tail(q, c):
            @pl.when(tails_ref[q] >= 0)
            def _():
                tail_copy(q).wait()
            return c

        def start_tile(j, c):
            tile_copy(j).start()
            return c

        def wait_tile(j, c):
            tile_copy(j).wait()
            return c

        n_tiles = xs_ref.shape[0] // EXPERT_TILE
        lax.fori_loop(0, tails_ref.shape[0], start_tail, 0)
        lax.fori_loop(nused_ref[0], n_tiles, start_tile, 0)
        lax.fori_loop(0, tails_ref.shape[0], wait_tail, 0)
        lax.fori_loop(nused_ref[0], n_tiles, wait_tile, 0)


def _dispatch(tab, tails, n_used, h1, route_t, n_rows):
    T, D = h1.shape
    tm = MOE_TILE
    tab_blk = lambda back: pl.BlockSpec((1, 1, LANES), lambda i: (jnp.maximum(i - back, 0), 0, 0),
                                        memory_space=pltpu.SMEM)
    return pl.pallas_call(
        _dispatch_kernel,
        grid=(T // tm,),
        in_specs=[
            tab_blk(0), tab_blk(1), tab_blk(2),
            pl.BlockSpec(memory_space=pltpu.SMEM),
            pl.BlockSpec(memory_space=pltpu.SMEM),
            pl.BlockSpec((tm, D), lambda i: (i, 0)),
            pl.BlockSpec((8, tm), lambda i: (0, i)),
        ],
        out_specs=pl.BlockSpec(memory_space=pl.ANY),
        out_shape=jax.ShapeDtypeStruct((n_rows, D), BF16),
        scratch_shapes=[pltpu.VMEM((2, LOCAL_ROWS, D), BF16), pltpu.VMEM((EXPERT_TILE, D), BF16),
                        pltpu.SemaphoreType.DMA((3,))],
        compiler_params=_cparams(1),
        name="moe_dispatch",
    )(tab, tab, tab, tails, n_used, h1, route_t)


def _experts_kernel(te_ref, nu_ref, xs_ref, wg_ref, wu_ref, wd_ref, ys_ref):
    del te_ref
    j = pl.program_id(0)

    @pl.when(j < nu_ref[0])
    def _():
        xb = xs_ref[...]
        hg = _dot(xb, wg_ref[0].astype(BF16))
        h = hg * _sigmoid(hg) * _dot(xb, wu_ref[0].astype(BF16))
        y = _dot(h.astype(BF16), wd_ref[0].astype(BF16))
        ys_ref[...] = y.astype(BF16)

    @pl.when(j >= nu_ref[0])
    def _():
        ys_ref[...] = jnp.zeros(ys_ref.shape, ys_ref.dtype)


def _experts(tile_expert, n_used, xs, wg, wu, wd):
    n_rows, D = xs.shape
    hid = wg.shape[2]
    tm = EXPERT_TILE
    used = lambda j, te, nu: (jnp.minimum(j, nu[0] - 1), 0)
    grid_spec = pltpu.PrefetchScalarGridSpec(
        num_scalar_prefetch=2,
        grid=(n_rows // tm,),
        in_specs=[
            pl.BlockSpec((tm, D), used),
            pl.BlockSpec((1, D, hid), lambda j, te, nu: (te[j], 0, 0)),
            pl.BlockSpec((1, D, hid), lambda j, te, nu: (te[j], 0, 0)),
            pl.BlockSpec((1, hid, D), lambda j, te, nu: (te[j], 0, 0)),
        ],
        out_specs=pl.BlockSpec((tm, D), lambda j, te, nu: (j, 0)),
    )
    return pl.pallas_call(
        _experts_kernel,
        grid_spec=grid_spec,
        out_shape=jax.ShapeDtypeStruct((n_rows, D), BF16),
        compiler_params=_cparams(1),
        name="moe_experts",
    )(tile_expert, n_used, xs, wg, wu, wd)


def _combine_kernel(alpha, tab_ref, tab_next_ref, routet_ref, h1_ref, p_ref, pproj_ref, pgw_ref,
                    pgb_ref, g2_ref, b2_ref, ys_ref, out_ref, buf, sem):
    i = pl.program_id(0)
    tm = h1_ref.shape[0]
    slot = i % 2

    def gather(tab, slot_):
        def start(q, c):
            src = pl.multiple_of(tab[0, 0, q], CHUNK)
            pltpu.make_async_copy(ys_ref.at[pl.ds(src, CHUNK)], buf.at[slot_, _chunk_rows(q)],
                                  sem.at[slot_]).start()
            return c
        lax.fori_loop(0, tab[0, 0, TAB_COUNT], start, 0)

    @pl.when(i == 0)
    def _():
        buf[...] = jnp.zeros(buf.shape, buf.dtype)
        gather(tab_ref, 0)

    @pl.when(i + 1 < pl.num_programs(0))
    def _():
        gather(tab_next_ref, 1 - slot)

    h1 = h1_ref[...]
    ple = (_sigmoid(_dot(h1.astype(BF16), pgw_ref[...]) + pgb_ref[...])
           * _dot(p_ref[...].astype(BF16), pproj_ref[...]))

    _wait_chunks(tab_ref[0, 0, TAB_COUNT],
                 lambda rows: pltpu.make_async_copy(ys_ref.at[pl.ds(0, rows)], buf.at[slot, pl.ds(0, rows)],
                                                    sem.at[slot]))

    s1 = routet_ref[R_SLOT1:R_SLOT1 + 1, :].astype(jnp.int32)
    s2 = routet_ref[R_SLOT2:R_SLOT2 + 1, :].astype(jnp.int32)
    c1 = routet_ref[R_C1:R_C1 + 1, :]
    c2 = routet_ref[R_C2:R_C2 + 1, :]
    r_io = lax.broadcasted_iota(jnp.int32, (LOCAL_ROWS, tm), 0)
    weights = jnp.where(r_io == s1, c1, jnp.where(r_io == s2, c2, 0.0)).astype(BF16)
    ffn = _dot_tn(weights, buf[slot])
    out_ref[...] = _layer_norm(alpha * h1 + ffn + ple, g2_ref[...], b2_ref[...])


def _combine(tab, route_t, h1, p2, pproj, pgw, pgb, g2, b2, ys, alpha):
    T, D = h1.shape
    tm = MOE_TILE
    nt = T // tm
    full = lambda a: pl.BlockSpec(a.shape, lambda i: (0,) * a.ndim)
    return pl.pallas_call(
        functools.partial(_combine_kernel, alpha),
        grid=(nt,),
        in_specs=[
            pl.BlockSpec((1, 1, LANES), lambda i: (i, 0, 0), memory_space=pltpu.SMEM),
            pl.BlockSpec((1, 1, LANES), lambda i: (jnp.minimum(i + 1, nt - 1), 0, 0), memory_space=pltpu.SMEM),
            pl.BlockSpec((8, tm), lambda i: (0, i)),
            pl.BlockSpec((tm, D), lambda i: (i, 0)),
            pl.BlockSpec((tm, p2.shape[1]), lambda i: (i, 0)),
            full(pproj), full(pgw), full(pgb), full(g2), full(b2),
            pl.BlockSpec(memory_space=pl.ANY),
        ],
        out_specs=pl.BlockSpec((tm, D), lambda i: (i, 0)),
        out_shape=jax.ShapeDtypeStruct((T, D), F32),
        scratch_shapes=[pltpu.VMEM((2, LOCAL_ROWS, D), BF16), pltpu.SemaphoreType.DMA((2,))],
        compiler_params=_cparams(1),
        name="moe_combine",
    )(tab, tab, route_t, h1, p2, pproj, pgw, pgb, g2, b2, ys)


def _moe_plan(cnt_tiles, n_expert_tiles):
    cnt = cnt_tiles[:, 0, :N_EXPERTS].astype(jnp.int32)
    cnt8 = (cnt + CHUNK - 1) // CHUNK * CHUNK
    lend = jnp.cumsum(cnt8, axis=1)
    lstart = lend - cnt8
    gend = jnp.cumsum(cnt8, axis=0)
    region = (gend[-1] + EXPERT_TILE - 1) // EXPERT_TILE * EXPERT_TILE
    oend = jnp.cumsum(region)
    gstart = (oend - region)[None, :] + gend - cnt8
    q8 = jnp.arange(MAX_CHUNKS, dtype=jnp.int32) * CHUNK
    eq = jnp.minimum(jnp.sum((lend[:, None, :] <= q8[None, :, None]).astype(jnp.int32), axis=-1), N_EXPERTS - 1)
    shift = jnp.sum(jnp.where(eq[:, :, None] == jnp.arange(N_EXPERTS)[None, None, :],
                              (gstart - lstart)[:, None, :], 0), axis=-1)
    dstq = shift + q8[None, :]
    nt = cnt.shape[0]
    tab = jnp.concatenate([dstq, jnp.zeros((nt, TAB_COUNT - MAX_CHUNKS), jnp.int32), lend[:, -1:] // CHUNK], axis=1)
    tile_row = jnp.arange(n_expert_tiles, dtype=jnp.int32) * EXPERT_TILE
    tile_expert = jnp.minimum(jnp.sum((oend[None, :] <= tile_row[:, None]).astype(jnp.int32), axis=1),
                              N_EXPERTS - 1)
    n_used = (oend[-1] // EXPERT_TILE).reshape(1)
    c = jnp.arange(EXPERT_TILE // CHUNK, dtype=jnp.int32)[None, :] * CHUNK
    tail_start = (oend - region + gend[-1])[:, None] + c
    tails = jnp.where(tail_start < oend[:, None], tail_start, -1).reshape(-1)
    return tab.reshape(nt, 1, LANES), tails, tile_expert, n_used


def _split_w_in(w_in, D):
    sizes = [Q_DIM] + [KV_DIM] * 6 + [N_HEADS * N_BRANCH] + [CONV_DIM] * 3 + [D] * 2
    offs = np.concatenate([[0], np.cumsum(sizes)])
    names = ["q", "k_cmp", "v_cmp", "k_slc", "v_slc", "k_win", "v_win", "g_nsa",
             "conv_b", "conv_c", "conv_h", "g_m_nsa", "g_m_conv"]
    return {n: w_in[:, int(offs[k]):int(offs[k + 1])] for k, n in enumerate(names)}


def _layer(x2, p2, B, seq, depth, w_in, cmp_pe, cmp_w1, cmp_b1, cmp_w2, cmp_b2, conv_w, w_nsa_out,
           w_conv_out, w_o, ln1_g, ln1_b, rg_w, rg_b, re_w, re_b, e_wg, e_wu, e_wd, ple_proj,
           ple_gate_w, ple_gate_b, ln2_g, ln2_b):
    T, D = x2.shape
    alpha = (2.0 * depth) ** 0.25
    w = _split_w_in(w_in, D)
    wtok = jnp.concatenate([w["k_cmp"], w["v_cmp"], w["k_slc"], w["k_win"],
                            w["conv_b"], w["conv_c"], w["conv_h"]], axis=1).astype(BF16)
    gcols = w["g_nsa"].reshape(D, N_GROUPS_KV, HEADS_PER_GROUP * N_BRANCH)
    gcols = jnp.pad(gcols, ((0, 0), (0, 0), (0, GATE_ROWS - HEADS_PER_GROUP * N_BRANCH)))
    wt = jnp.concatenate([w["q"], gcols.reshape(D, N_GROUPS_KV * GATE_ROWS), w["v_slc"], w["v_win"]],
                         axis=1).astype(BF16).T
    kcmp, vcmp, kslc, kwin, cu, qt, gt, vslct, vwint = _in_proj(x2, wtok, wt, conv_w, seq)

    half = CMP_BLOCK // 2
    n_chunks = seq // CMP_STRIDE
    n_cmp = (seq - CMP_BLOCK) // CMP_STRIDE + 1
    hidden = cmp_w1.shape[-1]
    eye = jnp.eye(N_GROUPS_KV, dtype=F32)
    w1r = cmp_w1.reshape(2, CMP_BLOCK, HEAD_DIM, hidden)
    expand = lambda m: jnp.einsum("ildh,gk->ilgdkh", m, eye).reshape(
        2, half * KV_DIM, N_GROUPS_KV * hidden).astype(BF16)
    wa, wb = expand(w1r[:, :half]), expand(w1r[:, half:])
    pe = jnp.broadcast_to(cmp_pe.reshape(2, 1, CMP_BLOCK * HEAD_DIM), (2, 8, CMP_BLOCK * HEAD_DIM)).astype(BF16)
    b1t = jnp.tile(cmp_b1.reshape(2, 1, hidden), (1, 1, N_GROUPS_KV))
    w2b = jnp.einsum("ihd,gk->ighkd", cmp_w2, eye).reshape(2, N_GROUPS_KV * hidden, KV_DIM).astype(BF16)
    b2t = jnp.tile(cmp_b2.reshape(2, 1, HEAD_DIM), (1, 1, N_GROUPS_KV))
    kc, vct = _compress(kcmp.reshape(B, n_chunks, CMP_STRIDE * KV_DIM),
                        vcmp.reshape(B, n_chunks, CMP_STRIDE * KV_DIM),
                        wa, wb, pe, cmp_w1.astype(BF16), b1t, w2b, b2t)

    n_sel = seq // SEL_BLOCK
    c_start = np.arange(n_chunks) * CMP_STRIDE
    s_start = np.arange(n_sel) * SEL_BLOCK
    overlap = ((c_start[None, :] <= s_start[:, None] + SEL_BLOCK - 1)
               & (c_start[None, :] + CMP_BLOCK - 1 >= s_start[:, None])).astype(np.float32)
    ocmpt, selb = _cmp_attn(qt, kc, vct, gt, jnp.asarray(overlap, BF16), B, seq, n_cmp)
    ot = _nsa_attn(qt, kslc, kwin, vslct, vwint, selb, gt, ocmpt, B, seq)

    wgm = jnp.concatenate([w["g_m_nsa"], w["g_m_conv"]], axis=1).astype(BF16)
    merge_tile = 1024
    wrt = jnp.pad(jnp.concatenate([re_w, rg_w], axis=1).T, ((0, LANES - N_EXPERTS - N_EXPERT_GROUPS), (0, 0)))
    brt = jnp.pad(jnp.concatenate([re_b, rg_b]), (0, LANES - N_EXPERTS - N_EXPERT_GROUPS))
    brt = jnp.broadcast_to(brt[:, None], (LANES, merge_tile))
    h1, route_t, cnt_tiles = _merge(ot, cu, x2, w_nsa_out.astype(BF16), w_conv_out.astype(BF16), wgm,
                                    w_o.astype(BF16), ln1_g.reshape(1, D), ln1_b.reshape(1, D),
                                    wrt.astype(BF16), brt, alpha)
    n_tok_tiles = T // MOE_TILE
    max_rows = 2 * T + n_tok_tiles * N_EXPERTS * (CHUNK - 1) + N_EXPERTS * (EXPERT_TILE - 1)
    n_expert_tiles = -(-max_rows // EXPERT_TILE)
    tab, tails, tile_expert, n_used = _moe_plan(cnt_tiles, n_expert_tiles)
    xs = _dispatch(tab, tails, n_used, h1, route_t, n_expert_tiles * EXPERT_TILE)
    ys = _experts(tile_expert, n_used, xs, e_wg, e_wu, e_wd)
    return _combine(tab, route_t, h1, p2, ple_proj.astype(BF16), ple_gate_w.astype(BF16),
                    ple_gate_b.reshape(1, D), ln2_g.reshape(1, D), ln2_b.reshape(1, D), ys, alpha)


def kernel(x, p, w_in, cmp_pe, cmp_w1, cmp_b1, cmp_w2, cmp_b2, conv_w, w_nsa_out, w_conv_out, w_o, ln1_g, ln1_b, router_group_w, router_group_b, router_expert_w, router_expert_b, expert_w_gate, expert_w_up, expert_w_down, ple_proj, ple_gate_w, ple_gate_b, ln2_g, ln2_b):
    B, seq, D = x.shape
    depth = w_in.shape[0]
    x2 = x.reshape(B * seq, D)
    for i in range(depth):
        x2 = _layer(x2, p[i].reshape(B * seq, -1), B, seq, depth, w_in[i], cmp_pe[i], cmp_w1[i], cmp_b1[i],
                    cmp_w2[i], cmp_b2[i], conv_w[i], w_nsa_out[i], w_conv_out[i], w_o[i], ln1_g[i], ln1_b[i],
                    router_group_w[i], router_group_b[i], router_expert_w[i], router_expert_b[i],
                    expert_w_gate[i], expert_w_up[i], expert_w_down[i], ple_proj[i], ple_gate_w[i],
                    ple_gate_b[i], ln2_g[i], ln2_b[i])
    return x2.reshape(B, seq, D)
```

```python
import functools

import jax
import jax.numpy as jnp
import numpy as np
from jax import lax
from jax.experimental import pallas as pl
from jax.experimental.pallas import tpu as pltpu

F32 = jnp.float32
BF16 = jnp.bfloat16

N_HEADS = 8
N_GROUPS_KV = 2
HEADS_PER_GROUP = N_HEADS // N_GROUPS_KV
HEAD_DIM = 64
Q_DIM = N_HEADS * HEAD_DIM
KV_DIM = N_GROUPS_KV * HEAD_DIM
N_BRANCH = 3
CMP_BLOCK = 32
CMP_STRIDE = 16
SEL_BLOCK = 64
SEL_TOPK = 16
WINDOW = 512
CONV_DIM = 512
N_EXPERT_GROUPS = 4
EXPERTS_PER_GROUP = 8
N_EXPERTS = N_EXPERT_GROUPS * EXPERTS_PER_GROUP
ATTN_SCALE = HEAD_DIM ** -0.5
LOG2_E = 1.4426950408889634
FORCE_BONUS = 1e4
NEG_INF = -1e30
LN_EPS = 1e-5

LANES = 128
ATT_TILE = 256
GATE_ROWS = 16
SUM_ROWS = 16
VMEM_LIMIT = 48 * 1024 * 1024

MOE_TILE = 256
EXPERT_TILE = 1024
CHUNK = 16
LOCAL_ROWS = -(-(2 * MOE_TILE + N_EXPERTS * (CHUNK - 1)) // ATT_TILE) * ATT_TILE
MAX_CHUNKS = LOCAL_ROWS // CHUNK
TAB_COUNT = LANES - 1


def _cparams(n_axes):
    return pltpu.CompilerParams(dimension_semantics=("arbitrary",) * n_axes,
                                vmem_limit_bytes=VMEM_LIMIT)


def _dot(a, b):
    return jnp.dot(a, b, preferred_element_type=F32)


def _dot_nt(a, b):
    return lax.dot_general(a, b, (((1,), (1,)), ((), ())), preferred_element_type=F32)


def _dot_tn(a, b):
    return lax.dot_general(a, b, (((0,), (0,)), ((), ())), preferred_element_type=F32)


def _sigmoid(v):
    return 1.0 / (1.0 + jnp.exp(-v))


def _layer_norm(v, g, b):
    mu = jnp.mean(v, axis=-1, keepdims=True)
    d = v - mu
    var = jnp.mean(d * d, axis=-1, keepdims=True)
    return d * lax.rsqrt(var + LN_EPS) * g + b


def _inproj_kernel(seq, x_ref, xprev_ref, wtok_ref, wt_ref, convw_ref,
                   kcmp_ref, vcmp_ref, kslc_ref, kwin_ref, cu_ref, qt_ref, gt_ref, vslct_ref, vwint_ref, cmp_s):
    i = pl.program_id(0)
    tm = x_ref.shape[0]
    xb = x_ref[...].astype(BF16)

    c0 = 4 * KV_DIM
    kvf = _dot(xb, wtok_ref[:, 0:c0])
    kv = kvf.astype(BF16)
    cmp_s[0] = kvf[:, 0:KV_DIM]
    cmp_s[1] = kvf[:, KV_DIM:2 * KV_DIM]
    for l in range(CMP_STRIDE):
        rows = pl.ds(l, tm // CMP_STRIDE, stride=CMP_STRIDE)
        kcmp_ref[:, l * KV_DIM:(l + 1) * KV_DIM] = cmp_s[0, rows, :].astype(BF16)
        vcmp_ref[:, l * KV_DIM:(l + 1) * KV_DIM] = cmp_s[1, rows, :].astype(BF16)
    kwin_ref[...] = kv[:, 3 * KV_DIM:4 * KV_DIM]
    n_sel = seq // SEL_BLOCK
    blk = ((i * tm + lax.broadcasted_iota(jnp.int32, (tm, KV_DIM), 0)) // SEL_BLOCK) % n_sel
    onehot = jnp.where(lax.broadcasted_iota(jnp.int32, (tm, KV_DIM), 1) == blk, 1.0, 0.0).astype(BF16)
    kslc_ref[...] = jnp.concatenate([kv[:, 2 * KV_DIM:3 * KV_DIM], onehot], axis=1)
    conv = _dot(xb, wtok_ref[:, c0:c0 + 3 * CONV_DIM])
    cb = conv[:, 0:CONV_DIM]
    u = conv[:, CONV_DIM:2 * CONV_DIM] * conv[:, 2 * CONV_DIM:3 * CONV_DIM]
    xpb = xprev_ref[...].astype(BF16)
    up = (_dot(xpb, wtok_ref[:, c0 + CONV_DIM:c0 + 2 * CONV_DIM])
          * _dot(xpb, wtok_ref[:, c0 + 2 * CONV_DIM:c0 + 3 * CONV_DIM]))
    up = jnp.where(i % (seq // tm) == 0, 0.0, up)
    row = lax.broadcasted_iota(jnp.int32, (tm, CONV_DIM), 0)
    u1 = jnp.where(row == 0, up[7:8, :], pltpu.roll(u, 1, 0))
    u2 = jnp.where(row == 0, up[6:7, :], jnp.where(row == 1, up[7:8, :], pltpu.roll(u, 2, 0)))
    w = convw_ref[...]
    uc = w[0:1, :] * u2 + w[1:2, :] * u1 + w[2:3, :] * u
    cu_ref[...] = (cb * uc).astype(BF16)

    zt = _dot_nt(wt_ref[...], xb)
    qt_ref[...] = (zt[0:Q_DIM, :] * (ATTN_SCALE * LOG2_E)).astype(BF16)
    r0 = Q_DIM
    gt_ref[...] = _sigmoid(zt[r0:r0 + 2 * GATE_ROWS, :])
    r0 += 2 * GATE_ROWS
    vs = zt[r0:r0 + KV_DIM, :].astype(BF16)
    vw = zt[r0 + KV_DIM:r0 + 2 * KV_DIM, :].astype(BF16)
    for c in range(tm // ATT_TILE):
        vslct_ref[c] = vs[:, c * ATT_TILE:(c + 1) * ATT_TILE]
        vwint_ref[c] = vw[:, c * ATT_TILE:(c + 1) * ATT_TILE]


def _in_proj(x2, wtok, wt, conv_w, seq):
    T, D = x2.shape
    tm = 1024
    nt = T // tm
    n_tok = wtok.shape[1]
    n_t = wt.shape[0]
    row_blk = lambda n: pl.BlockSpec((tm, n), lambda i: (i, 0))
    out_shape = (
        jax.ShapeDtypeStruct((T // CMP_STRIDE, CMP_STRIDE * KV_DIM), BF16),
        jax.ShapeDtypeStruct((T // CMP_STRIDE, CMP_STRIDE * KV_DIM), BF16),
        jax.ShapeDtypeStruct((T, 2 * KV_DIM), BF16),
        jax.ShapeDtypeStruct((T, KV_DIM), BF16),
        jax.ShapeDtypeStruct((T, CONV_DIM), BF16),
        jax.ShapeDtypeStruct((Q_DIM, T), BF16),
        jax.ShapeDtypeStruct((2 * GATE_ROWS, T), F32),
        jax.ShapeDtypeStruct((T // ATT_TILE, KV_DIM, ATT_TILE), BF16),
        jax.ShapeDtypeStruct((T // ATT_TILE, KV_DIM, ATT_TILE), BF16),
    )
    vt_blk = pl.BlockSpec((tm // ATT_TILE, KV_DIM, ATT_TILE), lambda i: (i, 0, 0))
    chunk_blk = pl.BlockSpec((tm // CMP_STRIDE, CMP_STRIDE * KV_DIM), lambda i: (i, 0))
    return pl.pallas_call(
        functools.partial(_inproj_kernel, seq),
        grid=(nt,),
        in_specs=[
            row_blk(D),
            pl.BlockSpec((8, D), lambda i: (jnp.maximum(i * (tm // 8) - 1, 0), 0)),
            pl.BlockSpec((D, n_tok), lambda i: (0, 0)),
            pl.BlockSpec((n_t, D), lambda i: (0, 0)),
            pl.BlockSpec((3, CONV_DIM), lambda i: (0, 0)),
        ],
        out_specs=(
            chunk_blk, chunk_blk, row_blk(2 * KV_DIM), row_blk(KV_DIM), row_blk(CONV_DIM),
            pl.BlockSpec((Q_DIM, tm), lambda i: (0, i)),
            pl.BlockSpec((2 * GATE_ROWS, tm), lambda i: (0, i)),
            vt_blk, vt_blk,
        ),
        out_shape=out_shape,
        scratch_shapes=[pltpu.VMEM((2, tm, KV_DIM), F32)],
        compiler_params=_cparams(1),
        name="in_proj",
    )(x2, x2, wtok, wt, conv_w)


def _compress_kernel(kin_ref, vin_ref, wa_ref, wb_ref, pe_ref, w1_ref, b1_ref, w2_ref, b2_ref,
                     kc_ref, vct_ref):
    def one(idx, in_ref):
        c = in_ref[0]
        a = _dot(c, wa_ref[idx])
        b = _dot(c, wb_ref[idx])
        n = b.shape[0]
        peb = _dot(pe_ref[idx], w1_ref[idx])[0:1, :]
        bias = jnp.concatenate([peb, peb], axis=1) + b1_ref[idx]
        h = a + pltpu.roll(b, n - 1, 0) + bias
        return _dot(jax.nn.gelu(h).astype(BF16), w2_ref[idx]) + b2_ref[idx]

    kc_ref[0] = one(0, kin_ref).astype(BF16)
    vct_ref[0] = one(1, vin_ref).T.astype(BF16)


def _compress(kcmp3, vcmp3, wa, wb, pe, w1, b1, w2, b2):
    B, nch, width = kcmp3.shape
    hid2 = wa.shape[2]
    full = lambda a: pl.BlockSpec(a.shape, lambda b: (0,) * a.ndim)
    in_blk = pl.BlockSpec((1, nch, width), lambda b: (b, 0, 0))
    out_blk = pl.BlockSpec((1, nch, KV_DIM), lambda b: (b, 0, 0))
    return pl.pallas_call(
        _compress_kernel,
        grid=(B,),
        in_specs=[in_blk, in_blk, full(wa), full(wb), full(pe), full(w1), full(b1), full(w2), full(b2)],
        out_specs=(out_blk, pl.BlockSpec((1, KV_DIM, nch), lambda b: (b, 0, 0))),
        out_shape=(jax.ShapeDtypeStruct((B, nch, KV_DIM), BF16),
                   jax.ShapeDtypeStruct((B, KV_DIM, nch), BF16)),
        compiler_params=_cparams(1),
        name="compress",
    )(kcmp3, vcmp3, wa, wb, pe, w1, b1, w2, b2)


def _group_q(qt_ref, hh, g_is0):
    qh = qt_ref[hh * HEAD_DIM:(hh + 1) * HEAD_DIM, :]
    z = jnp.zeros_like(qh)
    return jnp.concatenate([qh, z], axis=0) if g_is0 else jnp.concatenate([z, qh], axis=0)


def _cmpattn_kernel(n_cmp, qt_ref, kc_ref, vct_ref, gt_ref, ovt_ref, ocmpt_ref, selb_ref):
    i = pl.program_id(1)
    tq = qt_ref.shape[1]
    nc = kc_ref.shape[1]
    n_sel = ovt_ref.shape[0]
    kc = kc_ref[0]
    t_n = i * tq + lax.broadcasted_iota(jnp.int32, (nc, tq), 1)
    n_io = lax.broadcasted_iota(jnp.int32, (nc, tq), 0)
    blk_end = jnp.where(n_io < n_cmp, n_io * CMP_STRIDE + CMP_BLOCK - 1, jnp.int32(2 ** 30))
    vis = blk_end <= t_n
    j_io = lax.broadcasted_iota(jnp.int32, (n_sel, tq), 0)
    t_j = i * tq + lax.broadcasted_iota(jnp.int32, (n_sel, tq), 1)
    cur = t_j // SEL_BLOCK
    bonus = jnp.where(j_io == 0, FORCE_BONUS,
                      jnp.where(j_io == cur, FORCE_BONUS, jnp.where(j_io == cur - 1, FORCE_BONUS, 0.0)))
    valid = j_io * SEL_BLOCK <= t_j
    for g in range(N_GROUPS_KV):
        imp = jnp.zeros((n_sel, tq), F32)
        for hh in range(HEADS_PER_GROUP):
            h = g * HEADS_PER_GROUP + hh
            s = _dot(kc, _group_q(qt_ref, h, g == 0))
            s = jnp.where(vis, s, NEG_INF)
            m = jnp.max(s, axis=0, keepdims=True)
            p = jnp.where(vis, jnp.exp2(s - m), 0.0)
            l = jnp.sum(p, axis=0, keepdims=True)
            pr = (p * jnp.where(l > 0.0, 1.0 / l, 0.0)).astype(BF16)
            o = _dot(vct_ref[0, g * HEAD_DIM:(g + 1) * HEAD_DIM, :], pr)
            gate = gt_ref[g * GATE_ROWS + hh * N_BRANCH:g * GATE_ROWS + hh * N_BRANCH + 1, :]
            ocmpt_ref[h * HEAD_DIM:(h + 1) * HEAD_DIM, :] = (o * gate).astype(BF16)
            imp = imp + _dot(ovt_ref[...], pr)
        score = jnp.where(valid, imp + bonus, -FORCE_BONUS)
        cnt = jnp.zeros((n_sel, tq), F32)
        for jp in range(n_sel):
            r = score[jp:jp + 1, :]
            cnt = cnt + jnp.where(r > score, 1.0, jnp.where(r == score, jnp.where(j_io > jp, 1.0, 0.0), 0.0))
        selb_ref[0, g] = jnp.where(cnt < float(min(SEL_TOPK, n_sel)), 0.0, NEG_INF).astype(BF16)


def _cmp_attn(qt, kc, vct, gt, ovt, B, seq, n_cmp):
    T = qt.shape[1]
    tq = ATT_TILE
    nq = seq // tq
    n_sel = ovt.shape[0]
    nc = kc.shape[1]
    return pl.pallas_call(
        functools.partial(_cmpattn_kernel, n_cmp),
        grid=(B, nq),
        in_specs=[
            pl.BlockSpec((Q_DIM, tq), lambda b, i: (0, b * nq + i)),
            pl.BlockSpec((1, nc, KV_DIM), lambda b, i: (b, 0, 0)),
            pl.BlockSpec((1, KV_DIM, nc), lambda b, i: (b, 0, 0)),
            pl.BlockSpec((2 * GATE_ROWS, tq), lambda b, i: (0, b * nq + i)),
            pl.BlockSpec(ovt.shape, lambda b, i: (0, 0)),
        ],
        out_specs=(
            pl.BlockSpec((Q_DIM, tq), lambda b, i: (0, b * nq + i)),
            pl.BlockSpec((1, N_GROUPS_KV, n_sel, tq), lambda b, i: (b * nq + i, 0, 0, 0)),
        ),
        out_shape=(jax.ShapeDtypeStruct((Q_DIM, T), BF16),
                   jax.ShapeDtypeStruct((B * nq, N_GROUPS_KV, n_sel, tq), BF16)),
        compiler_params=_cparams(2),
        name="cmp_attn",
    )(qt, kc, vct, gt, ovt)


def _nsa_kernel(qt_ref, kslc_ref, kwin_ref, vslct_ref, vwint_ref, selb_ref, gt_ref, ocmpt_ref, causal_ref,
                window_ref, ot_ref, q_s, s_s, m_s, acc_s):
    i = pl.program_id(1)
    tq = qt_ref.shape[1]
    tile = ATT_TILE
    pair = 2 * tile
    n_sel = selb_ref.shape[2]
    groups = range(N_GROUPS_KV)
    gq = HEADS_PER_GROUP * HEAD_DIM

    for g in groups:
        for hh in range(HEADS_PER_GROUP):
            lanes = slice(hh * tq, (hh + 1) * tq)
            qh = qt_ref[g * gq + hh * HEAD_DIM:g * gq + (hh + 1) * HEAD_DIM, :]
            zero = jnp.zeros_like(qh)
            q_s[g, 0:KV_DIM, lanes] = jnp.concatenate([qh, zero] if g == 0 else [zero, qh], axis=0)
            q_s[g, KV_DIM:KV_DIM + n_sel, lanes] = selb_ref[0, g]
            q_s[g, KV_DIM + n_sel:2 * KV_DIM, lanes] = jnp.zeros((KV_DIM - n_sel, tq), BF16)

    def v_rows(ref, g, tiles):
        vt = jnp.concatenate([ref[j, g * HEAD_DIM:(g + 1) * HEAD_DIM, :] for j in tiles], axis=1)
        return jnp.concatenate([vt, jnp.ones((SUM_ROWS, vt.shape[1]), BF16)], axis=0)

    def sel_scores(g, jj):
        keys = kslc_ref[pl.ds(pl.multiple_of(jj * pair, pair), pair), :]
        return _dot(keys, q_s[g])

    def sel_update(g, s, jj):
        m_old = m_s[g]
        m_new = jnp.maximum(m_old, jnp.max(s, axis=0, keepdims=True))
        p = jnp.exp2(s - m_new).astype(BF16)
        acc_s[g] = jnp.exp2(m_old - m_new) * acc_s[g] + _dot(v_rows(vslct_ref, g, (2 * jj, 2 * jj + 1)), p)
        m_s[g] = m_new

    m_s[...] = jnp.full(m_s.shape, NEG_INF, F32)
    acc_s[...] = jnp.zeros(acc_s.shape, F32)
    n_full = i // 2
    for g in groups:
        s_s[g] = sel_scores(g, 0)

    def sel_body(jj, carry):
        for g in groups:
            s = s_s[g]
            s_s[g] = sel_scores(g, jj + 1)
            sel_update(g, s, jj)
        return carry

    lax.fori_loop(0, n_full, sel_body, 0)

    j0 = jnp.maximum(i - WINDOW // tile, 0)
    n_win = WINDOW + tile
    for g in groups:
        sel_update(g, s_s[g] + causal_ref[0], n_full)
        o_slc = acc_s[g, 0:HEAD_DIM, :] * (1.0 / acc_s[g, HEAD_DIM:HEAD_DIM + 1, :])
        s = _dot(kwin_ref[pl.ds(pl.multiple_of(j0 * tile, tile), n_win), :], q_s[g, 0:KV_DIM, :])
        s = s + window_ref[0]
        p = jnp.exp2(s - jnp.max(s, axis=0, keepdims=True)).astype(BF16)
        ow = _dot(v_rows(vwint_ref, g, (j0, j0 + 1, j0 + 2)), p)
        o_win = ow[0:HEAD_DIM, :] * (1.0 / ow[HEAD_DIM:HEAD_DIM + 1, :])
        for hh in range(HEADS_PER_GROUP):
            lanes = slice(hh * tq, (hh + 1) * tq)
            rows = slice(g * gq + hh * HEAD_DIM, g * gq + (hh + 1) * HEAD_DIM)
            gate = g * GATE_ROWS + hh * N_BRANCH
            o = (ocmpt_ref[rows, :].astype(F32) + gt_ref[gate + 1:gate + 2, :] * o_slc[:, lanes]
                 + gt_ref[gate + 2:gate + 3, :] * o_win[:, lanes])
            ot_ref[rows, :] = o.astype(BF16)


def _nsa_attn(qt, kslc, kwin, vslct, vwint, selb, gt, ocmpt, B, seq):
    T = qt.shape[1]
    tq = ATT_TILE
    nq = seq // tq
    ktiles = seq // ATT_TILE
    n_sel = selb.shape[2]
    wide = HEADS_PER_GROUP * tq
    n_win = WINDOW + ATT_TILE
    kmq = (lax.broadcasted_iota(jnp.int32, (1, n_win, wide), 1)
           - lax.broadcasted_iota(jnp.int32, (1, n_win, wide), 2) % tq)
    par = lax.broadcasted_iota(jnp.int32, (2, 1, 1), 0)
    causal = jnp.where(kmq[:, :2 * ATT_TILE] <= par * ATT_TILE, 0.0, NEG_INF).astype(F32)
    behind = lax.broadcasted_iota(jnp.int32, (n_win // ATT_TILE, 1, 1), 0) * ATT_TILE - kmq
    window = jnp.where((behind >= 0) & (behind < WINDOW), 0.0, NEG_INF).astype(F32)
    qblk = pl.BlockSpec((Q_DIM, tq), lambda b, i: (0, b * nq + i))
    vblk = pl.BlockSpec((ktiles, KV_DIM, ATT_TILE), lambda b, i: (b, 0, 0))
    return pl.pallas_call(
        _nsa_kernel,
        grid=(B, nq),
        in_specs=[
            qblk,
            pl.BlockSpec((seq, 2 * KV_DIM), lambda b, i: (b, 0)),
            pl.BlockSpec((seq, KV_DIM), lambda b, i: (b, 0)),
            vblk, vblk,
            pl.BlockSpec((1, N_GROUPS_KV, n_sel, tq), lambda b, i: (b * nq + i, 0, 0, 0)),
            pl.BlockSpec((N_GROUPS_KV * GATE_ROWS, tq), lambda b, i: (0, b * nq + i)),
            qblk,
            pl.BlockSpec((1, 2 * ATT_TILE, wide), lambda b, i: (i % 2, 0, 0)),
            pl.BlockSpec((1, n_win, wide), lambda b, i: (jnp.minimum(i, WINDOW // ATT_TILE), 0, 0)),
        ],
        out_specs=qblk,
        out_shape=jax.ShapeDtypeStruct((Q_DIM, T), BF16),
        scratch_shapes=[
            pltpu.VMEM((N_GROUPS_KV, 2 * KV_DIM, wide), BF16),
            pltpu.VMEM((N_GROUPS_KV, 2 * ATT_TILE, wide), F32),
            pltpu.VMEM((N_GROUPS_KV, 1, wide), F32),
            pltpu.VMEM((N_GROUPS_KV, HEAD_DIM + SUM_ROWS, wide), F32),
        ],
        compiler_params=_cparams(2),
        name="nsa_attn",
    )(qt, kslc, kwin, vslct, vwint, selb, gt, ocmpt, causal, window)


def _route(lt):
    row = lax.broadcasted_iota(jnp.int32, lt.shape, 0)
    big = 2 ** 20
    col = lambda f, v: f(v, axis=0, keepdims=True)
    grp = jnp.where((row >= N_EXPERTS) & (row < N_EXPERTS + N_EXPERT_GROUPS), lt, NEG_INF)
    gmax = col(jnp.max, grp)
    g_idx = col(jnp.min, jnp.where(grp == gmax, row, big)) - N_EXPERTS
    p_group = 1.0 / col(jnp.sum, jnp.exp(grp - gmax))
    own = jnp.where((row < N_EXPERTS) & (row // EXPERTS_PER_GROUP == g_idx), lt, NEG_INF)
    l1 = col(jnp.max, own)
    e1 = col(jnp.min, jnp.where(own == l1, row, big))
    rest = jnp.where(row == e1, NEG_INF, own)
    l2 = col(jnp.max, rest)
    e2 = col(jnp.min, jnp.where(rest == l2, row, big))
    r = jnp.exp(l2 - l1)
    c1 = p_group / (1.0 + r)
    return e1, e2, c1, c1 * r


R_SLOT1, R_SLOT2, R_C1, R_C2 = range(4)
ROUTE_ROWS = 40


def _merge_kernel(alpha, ot_ref, cu_ref, x_ref, wn_ref, wc_ref, wgm_ref, wo_ref, g1_ref, b1_ref,
                  wrt_ref, brt_ref, h1_ref, routet_ref, cnt_ref, lt_s):
    i = pl.program_id(0)
    tm, d = x_ref.shape
    iota = lambda shape, dim: lax.broadcasted_iota(jnp.int32, shape, dim)
    sq = (MOE_TILE, MOE_TILE)
    earlier_token = jnp.where(iota(sq, 0) < iota(sq, 1), 1.0, 0.0).astype(BF16)
    lower_expert = jnp.where(iota((LANES, LANES), 0) < iota((LANES, LANES), 1), 1.0, 0.0).astype(BF16)

    @pl.when(i == 0)
    def _():
        lt_s[...] = jnp.zeros(lt_s.shape, F32)

    e1, e2, c1, c2 = _route(lt_s[...])
    row = iota((LANES, tm), 0)
    pick1 = row == e1
    pick2 = row == e2
    one = lambda m: jnp.where(m, 1.0, 0.0).astype(BF16)
    ones8 = jnp.ones((8, MOE_TILE), BF16)
    subs = [slice(k * MOE_TILE, (k + 1) * MOE_TILE) for k in range(tm // MOE_TILE)]
    oh1 = [one(pick1[:, c]) for c in subs]
    oh2 = [one(pick2[:, c]) for c in subs]
    oh = [a + b for a, b in zip(oh1, oh2)]

    x = x_ref[...]
    xb = x.astype(BF16)
    y_nsa = _dot_tn(ot_ref[...], wn_ref[...])
    y_conv = _dot(cu_ref[...], wc_ref[...])
    merged = (_sigmoid(_dot(xb, wgm_ref[:, 0:d])) * y_nsa
              + _sigmoid(_dot(xb, wgm_ref[:, d:2 * d])) * y_conv)
    mix = _dot(merged.astype(BF16), wo_ref[...])

    before = [_dot(o, earlier_token) for o in oh]
    cnt = [_dot_nt(ones8, o) for o in oh]
    chunks = [jnp.floor((c + (CHUNK - 1.0)) * (1.0 / CHUNK)).astype(BF16) for c in cnt]
    run_start = [(_dot(c, lower_expert) * CHUNK).astype(BF16) for c in chunks]
    start1 = [_dot(r, o)[0:1, :] for r, o in zip(run_start, oh1)]
    start2 = [_dot(r, o)[0:1, :] for r, o in zip(run_start, oh2)]
    recs = []
    for k, c in enumerate(subs):
        slot1 = start1[k] + jnp.sum(jnp.where(pick1[:, c], before[k], 0.0), axis=0, keepdims=True)
        slot2 = start2[k] + jnp.sum(jnp.where(pick2[:, c], before[k], 0.0), axis=0, keepdims=True)
        recs.append(jnp.concatenate([slot1, slot2, c1[:, c], c2[:, c], jnp.zeros((4, MOE_TILE), F32)], axis=0))
        cnt_ref[k] = cnt[k]
    routet_ref[...] = jnp.concatenate(recs, axis=1)

    h1 = _layer_norm(alpha * x + mix, g1_ref[...], b1_ref[...])
    h1_ref[...] = h1
    lt_s[...] = (_dot_nt(wrt_ref[...], h1.astype(BF16)) + brt_ref[...])[0:ROUTE_ROWS, :]


def _merge(ot, cu, x2, wn, wc, wgm, wo, g1, b1, wrt, brt, alpha):
    T, D = x2.shape
    tm = brt.shape[1]
    sub = tm // MOE_TILE
    nt = T // tm
    full = lambda a: pl.BlockSpec(a.shape, lambda i: (0,) * a.ndim)
    cur = lambda i: jnp.minimum(i, nt - 1)
    prev = lambda i: jnp.maximum(i - 1, 0)
    return pl.pallas_call(
        functools.partial(_merge_kernel, alpha),
        grid=(nt + 1,),
        in_specs=[
            pl.BlockSpec((Q_DIM, tm), lambda i: (0, cur(i))),
            pl.BlockSpec((tm, CONV_DIM), lambda i: (cur(i), 0)),
            pl.BlockSpec((tm, D), lambda i: (cur(i), 0)),
            full(wn), full(wc), full(wgm), full(wo), full(g1), full(b1), full(wrt), full(brt),
        ],
        out_specs=(pl.BlockSpec((tm, D), lambda i: (cur(i), 0)),
                   pl.BlockSpec((8, tm), lambda i: (0, prev(i))),
                   pl.BlockSpec((sub, 8, LANES), lambda i: (prev(i), 0, 0))),
        out_shape=(jax.ShapeDtypeStruct((T, D), F32),
                   jax.ShapeDtypeStruct((8, T), F32),
                   jax.ShapeDtypeStruct((T // MOE_TILE, 8, LANES), F32)),
        scratch_shapes=[pltpu.VMEM((ROUTE_ROWS, tm), F32)],
        compiler_params=_cparams(1),
        name="merge",
    )(ot, cu, x2, wn, wc, wgm, wo, g1, b1, wrt, brt)


def _chunk_rows(q):
    return pl.ds(pl.multiple_of(q * CHUNK, CHUNK), CHUNK)


def _wait_chunks(n, copy_of_rows):
    size = MAX_CHUNKS
    while size >= 1:
        @pl.when((n & size) != 0)
        def _(size=size):
            copy_of_rows(size * CHUNK).wait()
        size //= 2


def _dispatch_kernel(tab_ref, tab_m1_ref, tab_m2_ref, tails_ref, nused_ref, h1_ref, routet_ref, xs_ref,
                     buf, zero_s, sem):
    i = pl.program_id(0)
    last = pl.num_programs(0) - 1
    tm = h1_ref.shape[0]
    slot = i % 2

    def drain(tab, slot_):
        _wait_chunks(tab[0, 0, TAB_COUNT],
                     lambda rows: pltpu.make_async_copy(buf.at[slot_, pl.ds(0, rows)], xs_ref.at[pl.ds(0, rows)],
                                                        sem.at[slot_]))

    s1 = routet_ref[R_SLOT1:R_SLOT1 + 1, :].astype(jnp.int32)
    s2 = routet_ref[R_SLOT2:R_SLOT2 + 1, :].astype(jnp.int32)
    r_io = lax.broadcasted_iota(jnp.int32, (LOCAL_ROWS, tm), 0)
    perm = jnp.where(r_io == s1, 1.0, jnp.where(r_io == s2, 1.0, 0.0)).astype(BF16)
    srt = _dot(perm, h1_ref[...].astype(BF16)).astype(BF16)

    @pl.when(i >= 2)
    def _():
        drain(tab_m2_ref, slot)

    buf[slot] = srt

    def start(q, c):
        dst = pl.multiple_of(tab_ref[0, 0, q], CHUNK)
        pltpu.make_async_copy(buf.at[slot, _chunk_rows(q)], xs_ref.at[pl.ds(dst, CHUNK)], sem.at[slot]).start()
        return c

    lax.fori_loop(0, tab_ref[0, 0, TAB_COUNT], start, 0)

    @pl.when(i == last)
    def _():
        drain(tab_ref, slot)

    @pl.when(jnp.logical_and(i == last, i >= 1))
    def _():
        drain(tab_m1_ref, 1 - slot)

    @pl.when(i == last)
    def _():
        zero_s[...] = jnp.zeros(zero_s.shape, zero_s.dtype)

        def tail_copy(q):
            dst = pl.multiple_of(jnp.maximum(tails_ref[q], 0), CHUNK)
            return pltpu.make_async_copy(zero_s.at[pl.ds(0, CHUNK)], xs_ref.at[pl.ds(dst, CHUNK)], sem.at[2])

        def tile_copy(j):
            dst = pl.multiple_of(j * EXPERT_TILE, EXPERT_TILE)
            return pltpu.make_async_copy(zero_s, xs_ref.at[pl.ds(dst, EXPERT_TILE)], sem.at[2])

        def start_tail(q, c):
            @pl.when(tails_ref[q] >= 0)
            def _():
                tail_copy(q).start()
            return c

        def wait_tail(q, c):
            @pl.when(tails_ref[q] >= 0)
            def _():
                tail_copy(q).wait()
            return c

        def start_tile(j, c):
            tile_copy(j).start()
            return c

        def wait_tile(j, c):
            tile_copy(j).wait()
            return c

        n_tiles = xs_ref.shape[0] // EXPERT_TILE
        lax.fori_loop(0, tails_ref.shape[0], start_tail, 0)
        lax.fori_loop(nused_ref[0], n_tiles, start_tile, 0)
        lax.fori_loop(0, tails_ref.shape[0], wait_tail, 0)
        lax.fori_loop(nused_ref[0], n_tiles, wait_tile, 0)


def _dispatch(tab, tails, n_used, h1, route_t, n_rows):
    T, D = h1.shape
    tm = MOE_TILE
    tab_blk = lambda back: pl.BlockSpec((1, 1, LANES), lambda i: (jnp.maximum(i - back, 0), 0, 0),
                                        memory_space=pltpu.SMEM)
    return pl.pallas_call(
        _dispatch_kernel,
        grid=(T // tm,),
        in_specs=[
            tab_blk(0), tab_blk(1), tab_blk(2),
            pl.BlockSpec(memory_space=pltpu.SMEM),
            pl.BlockSpec(memory_space=pltpu.SMEM),
            pl.BlockSpec((tm, D), lambda i: (i, 0)),
            pl.BlockSpec((8, tm), lambda i: (0, i)),
        ],
        out_specs=pl.BlockSpec(memory_space=pl.ANY),
        out_shape=jax.ShapeDtypeStruct((n_rows, D), BF16),
        scratch_shapes=[pltpu.VMEM((2, LOCAL_ROWS, D), BF16), pltpu.VMEM((EXPERT_TILE, D), BF16),
                        pltpu.SemaphoreType.DMA((3,))],
        compiler_params=_cparams(1),
        name="moe_dispatch",
    )(tab, tab, tab, tails, n_used, h1, route_t)


def _experts_kernel(te_ref, nu_ref, xs_ref, wg_ref, wu_ref, wd_ref, ys_ref, h_s):
    del te_ref
    j = pl.program_id(0)
    n_used = nu_ref[0]

    def down():
        ys_ref[...] = _dot(h_s[...], wd_ref[0].astype(BF16)).astype(BF16)

    def gate_up():
        xb = xs_ref[...]
        hg = _dot(xb, wg_ref[0].astype(BF16))
        h_s[...] = (hg * _sigmoid(hg) * _dot(xb, wu_ref[0].astype(BF16))).astype(BF16)

    @pl.when(j == 0)
    def _():
        gate_up()
        ys_ref[...] = jnp.zeros(ys_ref.shape, ys_ref.dtype)

    @pl.when(jnp.logical_and(j >= 1, j < n_used))
    def _():
        h_prev = h_s[...]
        xb = xs_ref[...]
        hg = _dot(xb, wg_ref[0].astype(BF16))
        hu = _dot(xb, wu_ref[0].astype(BF16))
        ys_ref[...] = _dot(h_prev, wd_ref[0].astype(BF16)).astype(BF16)
        h_s[...] = (hg * _sigmoid(hg) * hu).astype(BF16)

    @pl.when(j == n_used)
    def _():
        down()

    @pl.when(j > n_used)
    def _():
        ys_ref[...] = jnp.zeros(ys_ref.shape, ys_ref.dtype)


def _experts(tile_expert, n_used, xs, wg, wu, wd):
    n_rows, D = xs.shape
    hid = wg.shape[2]
    tm = EXPERT_TILE
    n_tiles = n_rows // tm
    cur = lambda j, nu: jnp.minimum(j, nu[0] - 1)
    prev = lambda j: jnp.maximum(j - 1, 0)
    grid_spec = pltpu.PrefetchScalarGridSpec(
        num_scalar_prefetch=2,
        grid=(n_tiles + 1,),
        in_specs=[
            pl.BlockSpec((tm, D), lambda j, te, nu: (cur(j, nu), 0)),
            pl.BlockSpec((1, D, hid), lambda j, te, nu: (te[cur(j, nu)], 0, 0)),
            pl.BlockSpec((1, D, hid), lambda j, te, nu: (te[cur(j, nu)], 0, 0)),
            pl.BlockSpec((1, hid, D), lambda j, te, nu: (te[prev(j)], 0, 0)),
        ],
        out_specs=pl.BlockSpec((tm, D), lambda j, te, nu: (prev(j), 0)),
        scratch_shapes=[pltpu.VMEM((tm, hid), BF16)],
    )
    return pl.pallas_call(
        _experts_kernel,
        grid_spec=grid_spec,
        out_shape=jax.ShapeDtypeStruct((n_rows, D), BF16),
        compiler_params=_cparams(1),
        name="moe_experts",
    )(tile_expert, n_used, xs, wg, wu, wd)


def _combine_kernel(alpha, tab_ref, tab_next_ref, routet_ref, h1_ref, p_ref, pproj_ref, pgw_ref,
                    pgb_ref, g2_ref, b2_ref, ys_ref, out_ref, buf, sem):
    i = pl.program_id(0)
    tm = h1_ref.shape[0]
    slot = i % 2

    def gather(tab, slot_):
        def start(q, c):
            src = pl.multiple_of(tab[0, 0, q], CHUNK)
            pltpu.make_async_copy(ys_ref.at[pl.ds(src, CHUNK)], buf.at[slot_, _chunk_rows(q)],
                                  sem.at[slot_]).start()
            return c
        lax.fori_loop(0, tab[0, 0, TAB_COUNT], start, 0)

    @pl.when(i == 0)
    def _():
        buf[...] = jnp.zeros(buf.shape, buf.dtype)
        gather(tab_ref, 0)

    @pl.when(i + 1 < pl.num_programs(0))
    def _():
        gather(tab_next_ref, 1 - slot)

    h1 = h1_ref[...]
    ple = (_sigmoid(_dot(h1.astype(BF16), pgw_ref[...]) + pgb_ref[...])
           * _dot(p_ref[...].astype(BF16), pproj_ref[...]))

    _wait_chunks(tab_ref[0, 0, TAB_COUNT],
                 lambda rows: pltpu.make_async_copy(ys_ref.at[pl.ds(0, rows)], buf.at[slot, pl.ds(0, rows)],
                                                    sem.at[slot]))

    s1 = routet_ref[R_SLOT1:R_SLOT1 + 1, :].astype(jnp.int32)
    s2 = routet_ref[R_SLOT2:R_SLOT2 + 1, :].astype(jnp.int32)
    c1 = routet_ref[R_C1:R_C1 + 1, :]
    c2 = routet_ref[R_C2:R_C2 + 1, :]
    r_io = lax.broadcasted_iota(jnp.int32, (LOCAL_ROWS, tm), 0)
    weights = jnp.where(r_io == s1, c1, jnp.where(r_io == s2, c2, 0.0)).astype(BF16)
    ffn = _dot_tn(weights, buf[slot])
    out_ref[...] = _layer_norm(alpha * h1 + ffn + ple, g2_ref[...], b2_ref[...])


def _combine(tab, route_t, h1, p2, pproj, pgw, pgb, g2, b2, ys, alpha):
    T, D = h1.shape
    tm = MOE_TILE
    nt = T // tm
    full = lambda a: pl.BlockSpec(a.shape, lambda i: (0,) * a.ndim)
    return pl.pallas_call(
        functools.partial(_combine_kernel, alpha),
        grid=(nt,),
        in_specs=[
            pl.BlockSpec((1, 1, LANES), lambda i: (i, 0, 0), memory_space=pltpu.SMEM),
            pl.BlockSpec((1, 1, LANES), lambda i: (jnp.minimum(i + 1, nt - 1), 0, 0), memory_space=pltpu.SMEM),
            pl.BlockSpec((8, tm), lambda i: (0, i)),
            pl.BlockSpec((tm, D), lambda i: (i, 0)),
            pl.BlockSpec((tm, p2.shape[1]), lambda i: (i, 0)),
            full(pproj), full(pgw), full(pgb), full(g2), full(b2),
            pl.BlockSpec(memory_space=pl.ANY),
        ],
        out_specs=pl.BlockSpec((tm, D), lambda i: (i, 0)),
        out_shape=jax.ShapeDtypeStruct((T, D), F32),
        scratch_shapes=[pltpu.VMEM((2, LOCAL_ROWS, D), BF16), pltpu.SemaphoreType.DMA((2,))],
        compiler_params=_cparams(1),
        name="moe_combine",
    )(tab, tab, route_t, h1, p2, pproj, pgw, pgb, g2, b2, ys)


def _moe_plan(cnt_tiles, n_expert_tiles):
    cnt = cnt_tiles[:, 0, :N_EXPERTS].astype(jnp.int32)
    cnt8 = (cnt + CHUNK - 1) // CHUNK * CHUNK
    lend = jnp.cumsum(cnt8, axis=1)
    lstart = lend - cnt8
    gend = jnp.cumsum(cnt8, axis=0)
    region = (gend[-1] + EXPERT_TILE - 1) // EXPERT_TILE * EXPERT_TILE
    oend = jnp.cumsum(region)
    gstart = (oend - region)[None, :] + gend - cnt8
    q8 = jnp.arange(MAX_CHUNKS, dtype=jnp.int32) * CHUNK
    eq = jnp.minimum(jnp.sum((lend[:, None, :] <= q8[None, :, None]).astype(jnp.int32), axis=-1), N_EXPERTS - 1)
    shift = jnp.sum(jnp.where(eq[:, :, None] == jnp.arange(N_EXPERTS)[None, None, :],
                              (gstart - lstart)[:, None, :], 0), axis=-1)
    dstq = shift + q8[None, :]
    nt = cnt.shape[0]
    tab = jnp.concatenate([dstq, jnp.zeros((nt, TAB_COUNT - MAX_CHUNKS), jnp.int32), lend[:, -1:] // CHUNK], axis=1)
    tile_row = jnp.arange(n_expert_tiles, dtype=jnp.int32) * EXPERT_TILE
    tile_expert = jnp.minimum(jnp.sum((oend[None, :] <= tile_row[:, None]).astype(jnp.int32), axis=1),
                              N_EXPERTS - 1)
    n_used = (oend[-1] // EXPERT_TILE).reshape(1)
    c = jnp.arange(EXPERT_TILE // CHUNK, dtype=jnp.int32)[None, :] * CHUNK
    tail_start = (oend - region + gend[-1])[:, None] + c
    tails = jnp.where(tail_start < oend[:, None], tail_start, -1).reshape(-1)
    return tab.reshape(nt, 1, LANES), tails, tile_expert, n_used


def _split_w_in(w_in, D):
    sizes = [Q_DIM] + [KV_DIM] * 6 + [N_HEADS * N_BRANCH] + [CONV_DIM] * 3 + [D] * 2
    offs = np.concatenate([[0], np.cumsum(sizes)])
    names = ["q", "k_cmp", "v_cmp", "k_slc", "v_slc", "k_win", "v_win", "g_nsa",
             "conv_b", "conv_c", "conv_h", "g_m_nsa", "g_m_conv"]
    return {n: w_in[:, int(offs[k]):int(offs[k + 1])] for k, n in enumerate(names)}


def _layer(x2, p2, B, seq, depth, w_in, cmp_pe, cmp_w1, cmp_b1, cmp_w2, cmp_b2, conv_w, w_nsa_out,
           w_conv_out, w_o, ln1_g, ln1_b, rg_w, rg_b, re_w, re_b, e_wg, e_wu, e_wd, ple_proj,
           ple_gate_w, ple_gate_b, ln2_g, ln2_b):
    T, D = x2.shape
    alpha = (2.0 * depth) ** 0.25
    w = _split_w_in(w_in, D)
    wtok = jnp.concatenate([w["k_cmp"], w["v_cmp"], w["k_slc"], w["k_win"],
                            w["conv_b"], w["conv_c"], w["conv_h"]], axis=1).astype(BF16)
    gcols = w["g_nsa"].reshape(D, N_GROUPS_KV, HEADS_PER_GROUP * N_BRANCH)
    gcols = jnp.pad(gcols, ((0, 0), (0, 0), (0, GATE_ROWS - HEADS_PER_GROUP * N_BRANCH)))
    wt = jnp.concatenate([w["q"], gcols.reshape(D, N_GROUPS_KV * GATE_ROWS), w["v_slc"], w["v_win"]],
                         axis=1).astype(BF16).T
    kcmp, vcmp, kslc, kwin, cu, qt, gt, vslct, vwint = _in_proj(x2, wtok, wt, conv_w, seq)

    half = CMP_BLOCK // 2
    n_chunks = seq // CMP_STRIDE
    n_cmp = (seq - CMP_BLOCK) // CMP_STRIDE + 1
    hidden = cmp_w1.shape[-1]
    eye = jnp.eye(N_GROUPS_KV, dtype=F32)
    w1r = cmp_w1.reshape(2, CMP_BLOCK, HEAD_DIM, hidden)
    expand = lambda m: jnp.einsum("ildh,gk->ilgdkh", m, eye).reshape(
        2, half * KV_DIM, N_GROUPS_KV * hidden).astype(BF16)
    wa, wb = expand(w1r[:, :half]), expand(w1r[:, half:])
    pe = jnp.broadcast_to(cmp_pe.reshape(2, 1, CMP_BLOCK * HEAD_DIM), (2, 8, CMP_BLOCK * HEAD_DIM)).astype(BF16)
    b1t = jnp.tile(cmp_b1.reshape(2, 1, hidden), (1, 1, N_GROUPS_KV))
    w2b = jnp.einsum("ihd,gk->ighkd", cmp_w2, eye).reshape(2, N_GROUPS_KV * hidden, KV_DIM).astype(BF16)
    b2t = jnp.tile(cmp_b2.reshape(2, 1, HEAD_DIM), (1, 1, N_GROUPS_KV))
    kc, vct = _compress(kcmp.reshape(B, n_chunks, CMP_STRIDE * KV_DIM),
                        vcmp.reshape(B, n_chunks, CMP_STRIDE * KV_DIM),
                        wa, wb, pe, cmp_w1.astype(BF16), b1t, w2b, b2t)

    n_sel = seq // SEL_BLOCK
    c_start = np.arange(n_chunks) * CMP_STRIDE
    s_start = np.arange(n_sel) * SEL_BLOCK
    overlap = ((c_start[None, :] <= s_start[:, None] + SEL_BLOCK - 1)
               & (c_start[None, :] + CMP_BLOCK - 1 >= s_start[:, None])).astype(np.float32)
    ocmpt, selb = _cmp_attn(qt, kc, vct, gt, jnp.asarray(overlap, BF16), B, seq, n_cmp)
    ot = _nsa_attn(qt, kslc, kwin, vslct, vwint, selb, gt, ocmpt, B, seq)

    wgm = jnp.concatenate([w["g_m_nsa"], w["g_m_conv"]], axis=1).astype(BF16)
    merge_tile = 1024
    wrt = jnp.pad(jnp.concatenate([re_w, rg_w], axis=1).T, ((0, LANES - N_EXPERTS - N_EXPERT_GROUPS), (0, 0)))
    brt = jnp.pad(jnp.concatenate([re_b, rg_b]), (0, LANES - N_EXPERTS - N_EXPERT_GROUPS))
    brt = jnp.broadcast_to(brt[:, None], (LANES, merge_tile))
    h1, route_t, cnt_tiles = _merge(ot, cu, x2, w_nsa_out.astype(BF16), w_conv_out.astype(BF16), wgm,
                                    w_o.astype(BF16), ln1_g.reshape(1, D), ln1_b.reshape(1, D),
                                    wrt.astype(BF16), brt, alpha)
    n_tok_tiles = T // MOE_TILE
    max_rows = 2 * T + n_tok_tiles * N_EXPERTS * (CHUNK - 1) + N_EXPERTS * (EXPERT_TILE - 1)
    n_expert_tiles = -(-max_rows // EXPERT_TILE)
    tab, tails, tile_expert, n_used = _moe_plan(cnt_tiles, n_expert_tiles)
    xs = _dispatch(tab, tails, n_used, h1, route_t, n_expert_tiles * EXPERT_TILE)
    ys = _experts(tile_expert, n_used, xs, e_wg, e_wu, e_wd)
    return _combine(tab, route_t, h1, p2, ple_proj.astype(BF16), ple_gate_w.astype(BF16),
                    ple_gate_b.reshape(1, D), ln2_g.reshape(1, D), ln2_b.reshape(1, D), ys, alpha)


def kernel(x, p, w_in, cmp_pe, cmp_w1, cmp_b1, cmp_w2, cmp_b2, conv_w, w_nsa_out, w_conv_out, w_o, ln1_g, ln1_b, router_group_w, router_group_b, router_expert_w, router_expert_b, expert_w_gate, expert_w_up, expert_w_down, ple_proj, ple_gate_w, ple_gate_b, ln2_g, ln2_b):
    B, seq, D = x.shape
    depth = w_in.shape[0]
    x2 = x.reshape(B * seq, D)
    for i in range(depth):
        x2 = _layer(x2, p[i].reshape(B * seq, -1), B, seq, depth, w_in[i], cmp_pe[i], cmp_w1[i], cmp_b1[i],
                    cmp_w2[i], cmp_b2[i], conv_w[i], w_nsa_out[i], w_conv_out[i], w_o[i], ln1_g[i], ln1_b[i],
                    router_group_w[i], router_group_b[i], router_expert_w[i], router_expert_b[i],
                    expert_w_gate[i], expert_w_up[i], expert_w_down[i], ple_proj[i], ple_gate_w[i],
                    ple_gate_b[i], ln2_g[i], ln2_b[i])
    return x2.reshape(B, seq, D)
```

```python
import functools

import jax
import jax.numpy as jnp
import numpy as np
from jax import lax
from jax.experimental import pallas as pl
from jax.experimental.pallas import tpu as pltpu

F32 = jnp.float32
BF16 = jnp.bfloat16

N_HEADS = 8
N_GROUPS_KV = 2
HEADS_PER_GROUP = N_HEADS // N_GROUPS_KV
HEAD_DIM = 64
Q_DIM = N_HEADS * HEAD_DIM
KV_DIM = N_GROUPS_KV * HEAD_DIM
N_BRANCH = 3
CMP_BLOCK = 32
CMP_STRIDE = 16
SEL_BLOCK = 64
SEL_TOPK = 16
WINDOW = 512
CONV_DIM = 512
N_EXPERT_GROUPS = 4
EXPERTS_PER_GROUP = 8
N_EXPERTS = N_EXPERT_GROUPS * EXPERTS_PER_GROUP
ATTN_SCALE = HEAD_DIM ** -0.5
LOG2_E = 1.4426950408889634
FORCE_BONUS = 1e4
NEG_INF = -1e30
LN_EPS = 1e-5

LANES = 128
ATT_TILE = 256
GATE_ROWS = 16
SUM_ROWS = 16
VMEM_LIMIT = 48 * 1024 * 1024

MOE_TILE = 256
EXPERT_TILE = 1024
CHUNK = 16
LOCAL_ROWS = -(-(2 * MOE_TILE + N_EXPERTS * (CHUNK - 1)) // ATT_TILE) * ATT_TILE
MAX_CHUNKS = LOCAL_ROWS // CHUNK
TAB_COUNT = LANES - 1


def _cparams(n_axes):
    return pltpu.CompilerParams(dimension_semantics=("arbitrary",) * n_axes,
                                vmem_limit_bytes=VMEM_LIMIT)


def _dot(a, b):
    return jnp.dot(a, b, preferred_element_type=F32)


def _dot_nt(a, b):
    return lax.dot_general(a, b, (((1,), (1,)), ((), ())), preferred_element_type=F32)


def _dot_tn(a, b):
    return lax.dot_general(a, b, (((0,), (0,)), ((), ())), preferred_element_type=F32)


def _sigmoid(v):
    return 1.0 / (1.0 + jnp.exp(-v))


def _layer_norm(v, g, b):
    mu = jnp.mean(v, axis=-1, keepdims=True)
    d = v - mu
    var = jnp.mean(d * d, axis=-1, keepdims=True)
    return d * lax.rsqrt(var + LN_EPS) * g + b


def _inproj_kernel(seq, x_ref, xprev_ref, wtok_ref, wt_ref, convw_ref,
                   kcmp_ref, vcmp_ref, kslc_ref, kwin_ref, cu_ref, qt_ref, gt_ref, vslct_ref, vwint_ref, cmp_s):
    i = pl.program_id(0)
    tm = x_ref.shape[0]
    xb = x_ref[...].astype(BF16)

    c0 = 4 * KV_DIM
    kvf = _dot(xb, wtok_ref[:, 0:c0])
    kv = kvf.astype(BF16)
    cmp_s[0] = kvf[:, 0:KV_DIM]
    cmp_s[1] = kvf[:, KV_DIM:2 * KV_DIM]
    for l in range(CMP_STRIDE):
        rows = pl.ds(l, tm // CMP_STRIDE, stride=CMP_STRIDE)
        kcmp_ref[:, l * KV_DIM:(l + 1) * KV_DIM] = cmp_s[0, rows, :].astype(BF16)
        vcmp_ref[:, l * KV_DIM:(l + 1) * KV_DIM] = cmp_s[1, rows, :].astype(BF16)
    kwin_ref[...] = kv[:, 3 * KV_DIM:4 * KV_DIM]
    n_sel = seq // SEL_BLOCK
    blk = ((i * tm + lax.broadcasted_iota(jnp.int32, (tm, KV_DIM), 0)) // SEL_BLOCK) % n_sel
    onehot = jnp.where(lax.broadcasted_iota(jnp.int32, (tm, KV_DIM), 1) == blk, 1.0, 0.0).astype(BF16)
    kslc_ref[...] = jnp.concatenate([kv[:, 2 * KV_DIM:3 * KV_DIM], onehot], axis=1)
    conv = _dot(xb, wtok_ref[:, c0:c0 + 3 * CONV_DIM])
    cb = conv[:, 0:CONV_DIM]
    u = conv[:, CONV_DIM:2 * CONV_DIM] * conv[:, 2 * CONV_DIM:3 * CONV_DIM]
    xpb = xprev_ref[...].astype(BF16)
    up = (_dot(xpb, wtok_ref[:, c0 + CONV_DIM:c0 + 2 * CONV_DIM])
          * _dot(xpb, wtok_ref[:, c0 + 2 * CONV_DIM:c0 + 3 * CONV_DIM]))
    up = jnp.where(i % (seq // tm) == 0, 0.0, up)
    row = lax.broadcasted_iota(jnp.int32, (tm, CONV_DIM), 0)
    u1 = jnp.where(row == 0, up[7:8, :], pltpu.roll(u, 1, 0))
    u2 = jnp.where(row == 0, up[6:7, :], jnp.where(row == 1, up[7:8, :], pltpu.roll(u, 2, 0)))
    w = convw_ref[...]
    uc = w[0:1, :] * u2 + w[1:2, :] * u1 + w[2:3, :] * u
    cu_ref[...] = (cb * uc).astype(BF16)

    zt = _dot_nt(wt_ref[...], xb)
    qt_ref[...] = (zt[0:Q_DIM, :] * (ATTN_SCALE * LOG2_E)).astype(BF16)
    r0 = Q_DIM
    gt_ref[...] = _sigmoid(zt[r0:r0 + 2 * GATE_ROWS, :])
    r0 += 2 * GATE_ROWS
    vs = zt[r0:r0 + KV_DIM, :].astype(BF16)
    vw = zt[r0 + KV_DIM:r0 + 2 * KV_DIM, :].astype(BF16)
    for c in range(tm // ATT_TILE):
        vslct_ref[c] = vs[:, c * ATT_TILE:(c + 1) * ATT_TILE]
        vwint_ref[c] = vw[:, c * ATT_TILE:(c + 1) * ATT_TILE]


def _in_proj(x2, wtok, wt, conv_w, seq):
    T, D = x2.shape
    tm = 1024
    nt = T // tm
    n_tok = wtok.shape[1]
    n_t = wt.shape[0]
    row_blk = lambda n: pl.BlockSpec((tm, n), lambda i: (i, 0))
    out_shape = (
        jax.ShapeDtypeStruct((T // CMP_STRIDE, CMP_STRIDE * KV_DIM), BF16),
        jax.ShapeDtypeStruct((T // CMP_STRIDE, CMP_STRIDE * KV_DIM), BF16),
        jax.ShapeDtypeStruct((T, 2 * KV_DIM), BF16),
        jax.ShapeDtypeStruct((T, KV_DIM), BF16),
        jax.ShapeDtypeStruct((T, CONV_DIM), BF16),
        jax.ShapeDtypeStruct((Q_DIM, T), BF16),
        jax.ShapeDtypeStruct((2 * GATE_ROWS, T), F32),
        jax.ShapeDtypeStruct((T // ATT_TILE, KV_DIM, ATT_TILE), BF16),
        jax.ShapeDtypeStruct((T // ATT_TILE, KV_DIM, ATT_TILE), BF16),
    )
    vt_blk = pl.BlockSpec((tm // ATT_TILE, KV_DIM, ATT_TILE), lambda i: (i, 0, 0))
    chunk_blk = pl.BlockSpec((tm // CMP_STRIDE, CMP_STRIDE * KV_DIM), lambda i: (i, 0))
    return pl.pallas_call(
        functools.partial(_inproj_kernel, seq),
        grid=(nt,),
        in_specs=[
            row_blk(D),
            pl.BlockSpec((8, D), lambda i: (jnp.maximum(i * (tm // 8) - 1, 0), 0)),
            pl.BlockSpec((D, n_tok), lambda i: (0, 0)),
            pl.BlockSpec((n_t, D), lambda i: (0, 0)),
            pl.BlockSpec((3, CONV_DIM), lambda i: (0, 0)),
        ],
        out_specs=(
            chunk_blk, chunk_blk, row_blk(2 * KV_DIM), row_blk(KV_DIM), row_blk(CONV_DIM),
            pl.BlockSpec((Q_DIM, tm), lambda i: (0, i)),
            pl.BlockSpec((2 * GATE_ROWS, tm), lambda i: (0, i)),
            vt_blk, vt_blk,
        ),
        out_shape=out_shape,
        scratch_shapes=[pltpu.VMEM((2, tm, KV_DIM), F32)],
        compiler_params=_cparams(1),
        name="in_proj",
    )(x2, x2, wtok, wt, conv_w)


def _compress_kernel(kin_ref, vin_ref, wa_ref, wb_ref, pe_ref, w1_ref, b1_ref, w2_ref, b2_ref,
                     kc_ref, vct_ref):
    def one(idx, in_ref):
        c = in_ref[0]
        a = _dot(c, wa_ref[idx])
        b = _dot(c, wb_ref[idx])
        n = b.shape[0]
        peb = _dot(pe_ref[idx], w1_ref[idx])[0:1, :]
        bias = jnp.concatenate([peb, peb], axis=1) + b1_ref[idx]
        h = a + pltpu.roll(b, n - 1, 0) + bias
        return _dot(jax.nn.gelu(h).astype(BF16), w2_ref[idx]) + b2_ref[idx]

    kc_ref[0] = one(0, kin_ref).astype(BF16)
    vct_ref[0] = one(1, vin_ref).T.astype(BF16)


def _compress(kcmp3, vcmp3, wa, wb, pe, w1, b1, w2, b2):
    B, nch, width = kcmp3.shape
    hid2 = wa.shape[2]
    full = lambda a: pl.BlockSpec(a.shape, lambda b: (0,) * a.ndim)
    in_blk = pl.BlockSpec((1, nch, width), lambda b: (b, 0, 0))
    out_blk = pl.BlockSpec((1, nch, KV_DIM), lambda b: (b, 0, 0))
    return pl.pallas_call(
        _compress_kernel,
        grid=(B,),
        in_specs=[in_blk, in_blk, full(wa), full(wb), full(pe), full(w1), full(b1), full(w2), full(b2)],
        out_specs=(out_blk, pl.BlockSpec((1, KV_DIM, nch), lambda b: (b, 0, 0))),
        out_shape=(jax.ShapeDtypeStruct((B, nch, KV_DIM), BF16),
                   jax.ShapeDtypeStruct((B, KV_DIM, nch), BF16)),
        compiler_params=_cparams(1),
        name="compress",
    )(kcmp3, vcmp3, wa, wb, pe, w1, b1, w2, b2)


def _group_q(qt_ref, hh, g_is0):
    qh = qt_ref[hh * HEAD_DIM:(hh + 1) * HEAD_DIM, :]
    z = jnp.zeros_like(qh)
    return jnp.concatenate([qh, z], axis=0) if g_is0 else jnp.concatenate([z, qh], axis=0)


def _cmpattn_kernel(n_cmp, qt_ref, kc_ref, vct_ref, gt_ref, ovt_ref, ocmpt_ref, selb_ref):
    i = pl.program_id(1)
    tq = qt_ref.shape[1]
    nc = kc_ref.shape[1]
    n_sel = ovt_ref.shape[0]
    kc = kc_ref[0]
    t_n = i * tq + lax.broadcasted_iota(jnp.int32, (nc, tq), 1)
    n_io = lax.broadcasted_iota(jnp.int32, (nc, tq), 0)
    blk_end = jnp.where(n_io < n_cmp, n_io * CMP_STRIDE + CMP_BLOCK - 1, jnp.int32(2 ** 30))
    vis = blk_end <= t_n
    j_io = lax.broadcasted_iota(jnp.int32, (n_sel, tq), 0)
    t_j = i * tq + lax.broadcasted_iota(jnp.int32, (n_sel, tq), 1)
    cur = t_j // SEL_BLOCK
    bonus = jnp.where(j_io == 0, FORCE_BONUS,
                      jnp.where(j_io == cur, FORCE_BONUS, jnp.where(j_io == cur - 1, FORCE_BONUS, 0.0)))
    valid = j_io * SEL_BLOCK <= t_j
    for g in range(N_GROUPS_KV):
        imp = jnp.zeros((n_sel, tq), F32)
        for hh in range(HEADS_PER_GROUP):
            h = g * HEADS_PER_GROUP + hh
            s = _dot(kc, _group_q(qt_ref, h, g == 0))
            s = jnp.where(vis, s, NEG_INF)
            m = jnp.max(s, axis=0, keepdims=True)
            p = jnp.where(vis, jnp.exp2(s - m), 0.0)
            l = jnp.sum(p, axis=0, keepdims=True)
            pr = (p * jnp.where(l > 0.0, 1.0 / l, 0.0)).astype(BF16)
            o = _dot(vct_ref[0, g * HEAD_DIM:(g + 1) * HEAD_DIM, :], pr)
            gate = gt_ref[g * GATE_ROWS + hh * N_BRANCH:g * GATE_ROWS + hh * N_BRANCH + 1, :]
            ocmpt_ref[h * HEAD_DIM:(h + 1) * HEAD_DIM, :] = (o * gate).astype(BF16)
            imp = imp + _dot(ovt_ref[...], pr)
        score = jnp.where(valid, imp + bonus, -FORCE_BONUS)
        cnt = jnp.zeros((n_sel, tq), F32)
        for jp in range(n_sel):
            r = score[jp:jp + 1, :]
            cnt = cnt + jnp.where(r > score, 1.0, jnp.where(r == score, jnp.where(j_io > jp, 1.0, 0.0), 0.0))
        selb_ref[0, g] = jnp.where(cnt < float(min(SEL_TOPK, n_sel)), 0.0, NEG_INF).astype(BF16)


def _cmp_attn(qt, kc, vct, gt, ovt, B, seq, n_cmp):
    T = qt.shape[1]
    tq = ATT_TILE
    nq = seq // tq
    n_sel = ovt.shape[0]
    nc = kc.shape[1]
    return pl.pallas_call(
        functools.partial(_cmpattn_kernel, n_cmp),
        grid=(B, nq),
        in_specs=[
            pl.BlockSpec((Q_DIM, tq), lambda b, i: (0, b * nq + i)),
            pl.BlockSpec((1, nc, KV_DIM), lambda b, i: (b, 0, 0)),
            pl.BlockSpec((1, KV_DIM, nc), lambda b, i: (b, 0, 0)),
            pl.BlockSpec((2 * GATE_ROWS, tq), lambda b, i: (0, b * nq + i)),
            pl.BlockSpec(ovt.shape, lambda b, i: (0, 0)),
        ],
        out_specs=(
            pl.BlockSpec((Q_DIM, tq), lambda b, i: (0, b * nq + i)),
            pl.BlockSpec((1, N_GROUPS_KV, n_sel, tq), lambda b, i: (b * nq + i, 0, 0, 0)),
        ),
        out_shape=(jax.ShapeDtypeStruct((Q_DIM, T), BF16),
                   jax.ShapeDtypeStruct((B * nq, N_GROUPS_KV, n_sel, tq), BF16)),
        compiler_params=_cparams(2),
        name="cmp_attn",
    )(qt, kc, vct, gt, ovt)


def _nsa_kernel(qt_ref, kslc_ref, kwin_ref, vslct_ref, vwint_ref, selb_ref, gt_ref, ocmpt_ref, causal_ref,
                window_ref, ot_ref, q_s, s_s, m_s, acc_s):
    i = pl.program_id(1)
    tq = qt_ref.shape[1]
    tile = ATT_TILE
    pair = 2 * tile
    n_sel = selb_ref.shape[2]
    groups = range(N_GROUPS_KV)
    gq = HEADS_PER_GROUP * HEAD_DIM

    for g in groups:
        for hh in range(HEADS_PER_GROUP):
            lanes = slice(hh * tq, (hh + 1) * tq)
            qh = qt_ref[g * gq + hh * HEAD_DIM:g * gq + (hh + 1) * HEAD_DIM, :]
            zero = jnp.zeros_like(qh)
            q_s[g, 0:KV_DIM, lanes] = jnp.concatenate([qh, zero] if g == 0 else [zero, qh], axis=0)
            q_s[g, KV_DIM:KV_DIM + n_sel, lanes] = selb_ref[0, g]
            q_s[g, KV_DIM + n_sel:2 * KV_DIM, lanes] = jnp.zeros((KV_DIM - n_sel, tq), BF16)

    def v_rows(ref, g, tiles):
        vt = jnp.concatenate([ref[j, g * HEAD_DIM:(g + 1) * HEAD_DIM, :] for j in tiles], axis=1)
        return jnp.concatenate([vt, jnp.ones((SUM_ROWS, vt.shape[1]), BF16)], axis=0)

    def sel_scores(g, jj):
        keys = kslc_ref[pl.ds(pl.multiple_of(jj * pair, pair), pair), :]
        return _dot(keys, q_s[g])

    def sel_update(g, s, jj):
        m_old = m_s[g]
        m_new = jnp.maximum(m_old, jnp.max(s, axis=0, keepdims=True))
        p = jnp.exp2(s - m_new).astype(BF16)
        acc_s[g] = jnp.exp2(m_old - m_new) * acc_s[g] + _dot(v_rows(vslct_ref, g, (2 * jj, 2 * jj + 1)), p)
        m_s[g] = m_new

    m_s[...] = jnp.full(m_s.shape, NEG_INF, F32)
    acc_s[...] = jnp.zeros(acc_s.shape, F32)
    n_full = i // 2
    for g in groups:
        s_s[g] = sel_scores(g, 0)

    def sel_body(jj, carry):
        for g in groups:
            s = s_s[g]
            s_s[g] = sel_scores(g, jj + 1)
            sel_update(g, s, jj)
        return carry

    lax.fori_loop(0, n_full, sel_body, 0)

    j0 = jnp.maximum(i - WINDOW // tile, 0)
    n_win = WINDOW + tile
    for g in groups:
        sel_update(g, s_s[g] + causal_ref[0], n_full)
        o_slc = acc_s[g, 0:HEAD_DIM, :] * (1.0 / acc_s[g, HEAD_DIM:HEAD_DIM + 1, :])
        s = _dot(kwin_ref[pl.ds(pl.multiple_of(j0 * tile, tile), n_win), :], q_s[g, 0:KV_DIM, :])
        s = s + window_ref[0]
        p = jnp.exp2(s - jnp.max(s, axis=0, keepdims=True)).astype(BF16)
        ow = _dot(v_rows(vwint_ref, g, (j0, j0 + 1, j0 + 2)), p)
        o_win = ow[0:HEAD_DIM, :] * (1.0 / ow[HEAD_DIM:HEAD_DIM + 1, :])
        for hh in range(HEADS_PER_GROUP):
            lanes = slice(hh * tq, (hh + 1) * tq)
            rows = slice(g * gq + hh * HEAD_DIM, g * gq + (hh + 1) * HEAD_DIM)
            gate = g * GATE_ROWS + hh * N_BRANCH
            o = (ocmpt_ref[rows, :].astype(F32) + gt_ref[gate + 1:gate + 2, :] * o_slc[:, lanes]
                 + gt_ref[gate + 2:gate + 3, :] * o_win[:, lanes])
            ot_ref[rows, :] = o.astype(BF16)


def _nsa_attn(qt, kslc, kwin, vslct, vwint, selb, gt, ocmpt, B, seq):
    T = qt.shape[1]
    tq = ATT_TILE
    nq = seq // tq
    ktiles = seq // ATT_TILE
    n_sel = selb.shape[2]
    wide = HEADS_PER_GROUP * tq
    n_win = WINDOW + ATT_TILE
    kmq = (lax.broadcasted_iota(jnp.int32, (1, n_win, wide), 1)
           - lax.broadcasted_iota(jnp.int32, (1, n_win, wide), 2) % tq)
    par = lax.broadcasted_iota(jnp.int32, (2, 1, 1), 0)
    causal = jnp.where(kmq[:, :2 * ATT_TILE] <= par * ATT_TILE, 0.0, NEG_INF).astype(F32)
    behind = lax.broadcasted_iota(jnp.int32, (n_win // ATT_TILE, 1, 1), 0) * ATT_TILE - kmq
    window = jnp.where((behind >= 0) & (behind < WINDOW), 0.0, NEG_INF).astype(F32)
    qblk = pl.BlockSpec((Q_DIM, tq), lambda b, i: (0, b * nq + i))
    vblk = pl.BlockSpec((ktiles, KV_DIM, ATT_TILE), lambda b, i: (b, 0, 0))
    return pl.pallas_call(
        _nsa_kernel,
        grid=(B, nq),
        in_specs=[
            qblk,
            pl.BlockSpec((seq, 2 * KV_DIM), lambda b, i: (b, 0)),
            pl.BlockSpec((seq, KV_DIM), lambda b, i: (b, 0)),
            vblk, vblk,
            pl.BlockSpec((1, N_GROUPS_KV, n_sel, tq), lambda b, i: (b * nq + i, 0, 0, 0)),
            pl.BlockSpec((N_GROUPS_KV * GATE_ROWS, tq), lambda b, i: (0, b * nq + i)),
            qblk,
            pl.BlockSpec((1, 2 * ATT_TILE, wide), lambda b, i: (i % 2, 0, 0)),
            pl.BlockSpec((1, n_win, wide), lambda b, i: (jnp.minimum(i, WINDOW // ATT_TILE), 0, 0)),
        ],
        out_specs=qblk,
        out_shape=jax.ShapeDtypeStruct((Q_DIM, T), BF16),
        scratch_shapes=[
            pltpu.VMEM((N_GROUPS_KV, 2 * KV_DIM, wide), BF16),
            pltpu.VMEM((N_GROUPS_KV, 2 * ATT_TILE, wide), F32),
            pltpu.VMEM((N_GROUPS_KV, 1, wide), F32),
            pltpu.VMEM((N_GROUPS_KV, HEAD_DIM + SUM_ROWS, wide), F32),
        ],
        compiler_params=_cparams(2),
        name="nsa_attn",
    )(qt, kslc, kwin, vslct, vwint, selb, gt, ocmpt, causal, window)


def _route(lt):
    row = lax.broadcasted_iota(jnp.int32, lt.shape, 0)
    big = 2 ** 20
    col = lambda f, v: f(v, axis=0, keepdims=True)
    grp = jnp.where((row >= N_EXPERTS) & (row < N_EXPERTS + N_EXPERT_GROUPS), lt, NEG_INF)
    gmax = col(jnp.max, grp)
    g_idx = col(jnp.min, jnp.where(grp == gmax, row, big)) - N_EXPERTS
    p_group = 1.0 / col(jnp.sum, jnp.exp(grp - gmax))
    own = jnp.where((row < N_EXPERTS) & (row // EXPERTS_PER_GROUP == g_idx), lt, NEG_INF)
    l1 = col(jnp.max, own)
    e1 = col(jnp.min, jnp.where(own == l1, row, big))
    rest = jnp.where(row == e1, NEG_INF, own)
    l2 = col(jnp.max, rest)
    e2 = col(jnp.min, jnp.where(rest == l2, row, big))
    r = jnp.exp(l2 - l1)
    c1 = p_group / (1.0 + r)
    return e1, e2, c1, c1 * r


R_SLOT1, R_SLOT2, R_C1, R_C2 = range(4)
ROUTE_ROWS = 40


def _merge_kernel(alpha, ot_ref, cu_ref, x_ref, wn_ref, wc_ref, wgm_ref, wo_ref, g1_ref, b1_ref,
                  wrt_ref, brt_ref, h1_ref, routet_ref, cnt_ref, lt_s):
    i = pl.program_id(0)
    tm, d = x_ref.shape
    iota = lambda shape, dim: lax.broadcasted_iota(jnp.int32, shape, dim)
    sq = (MOE_TILE, MOE_TILE)
    earlier_token = jnp.where(iota(sq, 0) < iota(sq, 1), 1.0, 0.0).astype(BF16)
    lower_expert = jnp.where(iota((LANES, LANES), 0) < iota((LANES, LANES), 1), 1.0, 0.0).astype(BF16)

    @pl.when(i == 0)
    def _():
        lt_s[...] = jnp.zeros(lt_s.shape, F32)

    e1, e2, c1, c2 = _route(lt_s[...])
    row = iota((LANES, tm), 0)
    pick1 = row == e1
    pick2 = row == e2
    one = lambda m: jnp.where(m, 1.0, 0.0).astype(BF16)
    ones8 = jnp.ones((8, MOE_TILE), BF16)
    subs = [slice(k * MOE_TILE, (k + 1) * MOE_TILE) for k in range(tm // MOE_TILE)]
    oh1 = [one(pick1[:, c]) for c in subs]
    oh2 = [one(pick2[:, c]) for c in subs]
    oh = [a + b for a, b in zip(oh1, oh2)]

    x = x_ref[...]
    xb = x.astype(BF16)
    y_nsa = _dot_tn(ot_ref[...], wn_ref[...])
    y_conv = _dot(cu_ref[...], wc_ref[...])
    merged = (_sigmoid(_dot(xb, wgm_ref[:, 0:d])) * y_nsa
              + _sigmoid(_dot(xb, wgm_ref[:, d:2 * d])) * y_conv)
    mix = _dot(merged.astype(BF16), wo_ref[...])

    before = [_dot(o, earlier_token) for o in oh]
    cnt = [_dot_nt(ones8, o) for o in oh]
    chunks = [jnp.floor((c + (CHUNK - 1.0)) * (1.0 / CHUNK)).astype(BF16) for c in cnt]
    run_start = [(_dot(c, lower_expert) * CHUNK).astype(BF16) for c in chunks]
    start1 = [_dot(r, o)[0:1, :] for r, o in zip(run_start, oh1)]
    start2 = [_dot(r, o)[0:1, :] for r, o in zip(run_start, oh2)]
    recs = []
    for k, c in enumerate(subs):
        slot1 = start1[k] + jnp.sum(jnp.where(pick1[:, c], before[k], 0.0), axis=0, keepdims=True)
        slot2 = start2[k] + jnp.sum(jnp.where(pick2[:, c], before[k], 0.0), axis=0, keepdims=True)
        recs.append(jnp.concatenate([slot1, slot2, c1[:, c], c2[:, c], jnp.zeros((4, MOE_TILE), F32)], axis=0))
        cnt_ref[k] = cnt[k]
    routet_ref[...] = jnp.concatenate(recs, axis=1)

    h1 = _layer_norm(alpha * x + mix, g1_ref[...], b1_ref[...])
    h1_ref[...] = h1
    lt_s[...] = (_dot_nt(wrt_ref[...], h1.astype(BF16)) + brt_ref[...])[0:ROUTE_ROWS, :]


def _merge(ot, cu, x2, wn, wc, wgm, wo, g1, b1, wrt, brt, alpha):
    T, D = x2.shape
    tm = brt.shape[1]
    sub = tm // MOE_TILE
    nt = T // tm
    full = lambda a: pl.BlockSpec(a.shape, lambda i: (0,) * a.ndim)
    cur = lambda i: jnp.minimum(i, nt - 1)
    prev = lambda i: jnp.maximum(i - 1, 0)
    return pl.pallas_call(
        functools.partial(_merge_kernel, alpha),
        grid=(nt + 1,),
        in_specs=[
            pl.BlockSpec((Q_DIM, tm), lambda i: (0, cur(i))),
            pl.BlockSpec((tm, CONV_DIM), lambda i: (cur(i), 0)),
            pl.BlockSpec((tm, D), lambda i: (cur(i), 0)),
            full(wn), full(wc), full(wgm), full(wo), full(g1), full(b1), full(wrt), full(brt),
        ],
        out_specs=(pl.BlockSpec((tm, D), lambda i: (cur(i), 0)),
                   pl.BlockSpec((8, tm), lambda i: (0, prev(i))),
                   pl.BlockSpec((sub, 8, LANES), lambda i: (prev(i), 0, 0))),
        out_shape=(jax.ShapeDtypeStruct((T, D), F32),
                   jax.ShapeDtypeStruct((8, T), F32),
                   jax.ShapeDtypeStruct((T // MOE_TILE, 8, LANES), F32)),
        scratch_shapes=[pltpu.VMEM((ROUTE_ROWS, tm), F32)],
        compiler_params=_cparams(1),
        name="merge",
    )(ot, cu, x2, wn, wc, wgm, wo, g1, b1, wrt, brt)


def _chunk_rows(q):
    return pl.ds(pl.multiple_of(q * CHUNK, CHUNK), CHUNK)


def _wait_chunks(n, copy_of_rows):
    size = MAX_CHUNKS
    while size >= 1:
        @pl.when((n & size) != 0)
        def _(size=size):
            copy_of_rows(size * CHUNK).wait()
        size //= 2


def _dispatch_kernel(tab_ref, tab_m1_ref, tab_m2_ref, tails_ref, nused_ref, h1_ref, routet_ref, xs_ref,
                     buf, zero_s, sem):
    i = pl.program_id(0)
    last = pl.num_programs(0) - 1
    tm = h1_ref.shape[0]
    slot = i % 2

    def drain(tab, slot_):
        _wait_chunks(tab[0, 0, TAB_COUNT],
                     lambda rows: pltpu.make_async_copy(buf.at[slot_, pl.ds(0, rows)], xs_ref.at[pl.ds(0, rows)],
                                                        sem.at[slot_]))

    s1 = routet_ref[R_SLOT1:R_SLOT1 + 1, :].astype(jnp.int32)
    s2 = routet_ref[R_SLOT2:R_SLOT2 + 1, :].astype(jnp.int32)
    r_io = lax.broadcasted_iota(jnp.int32, (LOCAL_ROWS, tm), 0)
    perm = jnp.where(r_io == s1, 1.0, jnp.where(r_io == s2, 1.0, 0.0)).astype(BF16)
    srt = _dot(perm, h1_ref[...].astype(BF16)).astype(BF16)

    @pl.when(i >= 2)
    def _():
        drain(tab_m2_ref, slot)

    buf[slot] = srt

    def start(q, c):
        dst = pl.multiple_of(tab_ref[0, 0, q], CHUNK)
        pltpu.make_async_copy(buf.at[slot, _chunk_rows(q)], xs_ref.at[pl.ds(dst, CHUNK)], sem.at[slot]).start()
        return c

    lax.fori_loop(0, tab_ref[0, 0, TAB_COUNT], start, 0)

    @pl.when(i == last)
    def _():
        drain(tab_ref, slot)

    @pl.when(jnp.logical_and(i == last, i >= 1))
    def _():
        drain(tab_m1_ref, 1 - slot)

    @pl.when(i == last)
    def _():
        zero_s[...] = jnp.zeros(zero_s.shape, zero_s.dtype)

        def tail_copy(q):
            dst = pl.multiple_of(jnp.maximum(tails_ref[q], 0), CHUNK)
            return pltpu.make_async_copy(zero_s.at[pl.ds(0, CHUNK)], xs_ref.at[pl.ds(dst, CHUNK)], sem.at[2])

        def tile_copy(j):
            dst = pl.multiple_of(j * EXPERT_TILE, EXPERT_TILE)
            return pltpu.make_async_copy(zero_s, xs_ref.at[pl.ds(dst, EXPERT_TILE)], sem.at[2])

        def start_tail(q, c):
            @pl.when(tails_ref[q] >= 0)
            def _():
                tail_copy(q).start()
            return c

        def wait_tail(q, c):
            @pl.when(tails_ref[q] >= 0)
            def _():
                tail_copy(q).wait()
            return c

        def start_tile(j, c):
            tile_copy(j).start()
            return c

        def wait_tile(j, c):
            tile_copy(j).wait()
            return c

        n_tiles = xs_ref.shape[0] // EXPERT_TILE
        lax.fori_loop(0, tails_ref.shape[0], start_tail, 0)
        lax.fori_loop(nused_ref[0], n_tiles, start_tile, 0)
        lax.fori_loop(0, tails_ref.shape[0], wait_tail, 0)
        lax.fori_loop(nused_ref[0], n_tiles, wait_tile, 0)


def _dispatch(tab, tails, n_used, h1, route_t, n_rows):
    T, D = h1.shape
    tm = MOE_TILE
    tab_blk = lambda back: pl.BlockSpec((1, 1, LANES), lambda i: (jnp.maximum(i - back, 0), 0, 0),
                                        memory_space=pltpu.SMEM)
    return pl.pallas_call(
        _dispatch_kernel,
        grid=(T // tm,),
        in_specs=[
            tab_blk(0), tab_blk(1), tab_blk(2),
            pl.BlockSpec(memory_space=pltpu.SMEM),
            pl.BlockSpec(memory_space=pltpu.SMEM),
            pl.BlockSpec((tm, D), lambda i: (i, 0)),
            pl.BlockSpec((8, tm), lambda i: (0, i)),
        ],
        out_specs=pl.BlockSpec(memory_space=pl.ANY),
        out_shape=jax.ShapeDtypeStruct((n_rows, D), BF16),
        scratch_shapes=[pltpu.VMEM((2, LOCAL_ROWS, D), BF16), pltpu.VMEM((EXPERT_TILE, D), BF16),
                        pltpu.SemaphoreType.DMA((3,))],
        compiler_params=_cparams(1),
        name="moe_dispatch",
    )(tab, tab, tab, tails, n_used, h1, route_t)


def _experts_kernel(te_ref, nu_ref, xs_ref, wg_ref, wu_ref, wd_ref, ys_ref, h_s):
    del te_ref
    j = pl.program_id(0)
    n_used = nu_ref[0]

    def down():
        ys_ref[...] = _dot(h_s[...], wd_ref[0].astype(BF16)).astype(BF16)

    def gate_up():
        xb = xs_ref[...]
        hg = _dot(xb, wg_ref[0].astype(BF16))
        h_s[...] = (hg * _sigmoid(hg) * _dot(xb, wu_ref[0].astype(BF16))).astype(BF16)

    @pl.when(j == 0)
    def _():
        gate_up()
        ys_ref[...] = jnp.zeros(ys_ref.shape, ys_ref.dtype)

    @pl.when(jnp.logical_and(j >= 1, j < n_used))
    def _():
        h_prev = h_s[...]
        xb = xs_ref[...]
        hg = _dot(xb, wg_ref[0].astype(BF16))
        hu = _dot(xb, wu_ref[0].astype(BF16))
        ys_ref[...] = _dot(h_prev, wd_ref[0].astype(BF16)).astype(BF16)
        h_s[...] = (hg * _sigmoid(hg) * hu).astype(BF16)

    @pl.when(j == n_used)
    def _():
        down()


def _experts(tile_expert, n_used, xs, wg, wu, wd):
    n_rows, D = xs.shape
    hid = wg.shape[2]
    tm = EXPERT_TILE
    n_tiles = n_rows // tm
    cur = lambda j, nu: jnp.minimum(j, nu[0] - 1)
    prev = lambda j, nu: jnp.clip(j - 1, 0, nu[0] - 1)
    grid_spec = pltpu.PrefetchScalarGridSpec(
        num_scalar_prefetch=2,
        grid=(n_tiles + 1,),
        in_specs=[
            pl.BlockSpec((tm, D), lambda j, te, nu: (cur(j, nu), 0)),
            pl.BlockSpec((1, D, hid), lambda j, te, nu: (te[cur(j, nu)], 0, 0)),
            pl.BlockSpec((1, D, hid), lambda j, te, nu: (te[cur(j, nu)], 0, 0)),
            pl.BlockSpec((1, hid, D), lambda j, te, nu: (te[prev(j, nu)], 0, 0)),
        ],
        out_specs=pl.BlockSpec((tm, D), lambda j, te, nu: (prev(j, nu), 0)),
        scratch_shapes=[pltpu.VMEM((tm, hid), BF16)],
    )
    return pl.pallas_call(
        _experts_kernel,
        grid_spec=grid_spec,
        out_shape=jax.ShapeDtypeStruct((n_rows, D), BF16),
        input_output_aliases={2: 0},
        compiler_params=_cparams(1),
        name="moe_experts",
    )(tile_expert, n_used, xs, wg, wu, wd)


def _combine_kernel(alpha, tab_ref, tab_next_ref, routet_ref, h1_ref, p_ref, pproj_ref, pgw_ref,
                    pgb_ref, g2_ref, b2_ref, ys_ref, out_ref, buf, sem):
    i = pl.program_id(0)
    tm = h1_ref.shape[0]
    slot = i % 2

    def gather(tab, slot_):
        def start(q, c):
            src = pl.multiple_of(tab[0, 0, q], CHUNK)
            pltpu.make_async_copy(ys_ref.at[pl.ds(src, CHUNK)], buf.at[slot_, _chunk_rows(q)],
                                  sem.at[slot_]).start()
            return c
        lax.fori_loop(0, tab[0, 0, TAB_COUNT], start, 0)

    @pl.when(i == 0)
    def _():
        buf[...] = jnp.zeros(buf.shape, buf.dtype)
        gather(tab_ref, 0)

    @pl.when(i + 1 < pl.num_programs(0))
    def _():
        gather(tab_next_ref, 1 - slot)

    h1 = h1_ref[...]
    ple = (_sigmoid(_dot(h1.astype(BF16), pgw_ref[...]) + pgb_ref[...])
           * _dot(p_ref[...].astype(BF16), pproj_ref[...]))

    _wait_chunks(tab_ref[0, 0, TAB_COUNT],
                 lambda rows: pltpu.make_async_copy(ys_ref.at[pl.ds(0, rows)], buf.at[slot, pl.ds(0, rows)],
                                                    sem.at[slot]))

    s1 = routet_ref[R_SLOT1:R_SLOT1 + 1, :].astype(jnp.int32)
    s2 = routet_ref[R_SLOT2:R_SLOT2 + 1, :].astype(jnp.int32)
    c1 = routet_ref[R_C1:R_C1 + 1, :]
    c2 = routet_ref[R_C2:R_C2 + 1, :]
    r_io = lax.broadcasted_iota(jnp.int32, (LOCAL_ROWS, tm), 0)
    weights = jnp.where(r_io == s1, c1, jnp.where(r_io == s2, c2, 0.0)).astype(BF16)
    ffn = _dot_tn(weights, buf[slot])
    out_ref[...] = _layer_norm(alpha * h1 + ffn + ple, g2_ref[...], b2_ref[...])


def _combine(tab, route_t, h1, p2, pproj, pgw, pgb, g2, b2, ys, alpha):
    T, D = h1.shape
    tm = MOE_TILE
    nt = T // tm
    full = lambda a: pl.BlockSpec(a.shape, lambda i: (0,) * a.ndim)
    return pl.pallas_call(
        functools.partial(_combine_kernel, alpha),
        grid=(nt,),
        in_specs=[
            pl.BlockSpec((1, 1, LANES), lambda i: (i, 0, 0), memory_space=pltpu.SMEM),
            pl.BlockSpec((1, 1, LANES), lambda i: (jnp.minimum(i + 1, nt - 1), 0, 0), memory_space=pltpu.SMEM),
            pl.BlockSpec((8, tm), lambda i: (0, i)),
            pl.BlockSpec((tm, D), lambda i: (i, 0)),
            pl.BlockSpec((tm, p2.shape[1]), lambda i: (i, 0)),
            full(pproj), full(pgw), full(pgb), full(g2), full(b2),
            pl.BlockSpec(memory_space=pl.ANY),
        ],
        out_specs=pl.BlockSpec((tm, D), lambda i: (i, 0)),
        out_shape=jax.ShapeDtypeStruct((T, D), F32),
        scratch_shapes=[pltpu.VMEM((2, LOCAL_ROWS, D), BF16), pltpu.SemaphoreType.DMA((2,))],
        compiler_params=_cparams(1),
        name="moe_combine",
    )(tab, tab, route_t, h1, p2, pproj, pgw, pgb, g2, b2, ys)


def _moe_plan(cnt_tiles, n_expert_tiles):
    cnt = cnt_tiles[:, 0, :N_EXPERTS].astype(jnp.int32)
    cnt8 = (cnt + CHUNK - 1) // CHUNK * CHUNK
    lend = jnp.cumsum(cnt8, axis=1)
    lstart = lend - cnt8
    gend = jnp.cumsum(cnt8, axis=0)
    region = (gend[-1] + EXPERT_TILE - 1) // EXPERT_TILE * EXPERT_TILE
    oend = jnp.cumsum(region)
    gstart = (oend - region)[None, :] + gend - cnt8
    q8 = jnp.arange(MAX_CHUNKS, dtype=jnp.int32) * CHUNK
    eq = jnp.minimum(jnp.sum((lend[:, None, :] <= q8[None, :, None]).astype(jnp.int32), axis=-1), N_EXPERTS - 1)
    shift = jnp.sum(jnp.where(eq[:, :, None] == jnp.arange(N_EXPERTS)[None, None, :],
                              (gstart - lstart)[:, None, :], 0), axis=-1)
    dstq = shift + q8[None, :]
    nt = cnt.shape[0]
    tab = jnp.concatenate([dstq, jnp.zeros((nt, TAB_COUNT - MAX_CHUNKS), jnp.int32), lend[:, -1:] // CHUNK], axis=1)
    tile_row = jnp.arange(n_expert_tiles, dtype=jnp.int32) * EXPERT_TILE
    tile_expert = jnp.minimum(jnp.sum((oend[None, :] <= tile_row[:, None]).astype(jnp.int32), axis=1),
                              N_EXPERTS - 1)
    n_used = (oend[-1] // EXPERT_TILE).reshape(1)
    c = jnp.arange(EXPERT_TILE // CHUNK, dtype=jnp.int32)[None, :] * CHUNK
    tail_start = (oend - region + gend[-1])[:, None] + c
    tails = jnp.where(tail_start < oend[:, None], tail_start, -1).reshape(-1)
    return tab.reshape(nt, 1, LANES), tails, tile_expert, n_used


def _split_w_in(w_in, D):
    sizes = [Q_DIM] + [KV_DIM] * 6 + [N_HEADS * N_BRANCH] + [CONV_DIM] * 3 + [D] * 2
    offs = np.concatenate([[0], np.cumsum(sizes)])
    names = ["q", "k_cmp", "v_cmp", "k_slc", "v_slc", "k_win", "v_win", "g_nsa",
             "conv_b", "conv_c", "conv_h", "g_m_nsa", "g_m_conv"]
    return {n: w_in[:, int(offs[k]):int(offs[k + 1])] for k, n in enumerate(names)}


def _layer(x2, p2, B, seq, depth, w_in, cmp_pe, cmp_w1, cmp_b1, cmp_w2, cmp_b2, conv_w, w_nsa_out,
           w_conv_out, w_o, ln1_g, ln1_b, rg_w, rg_b, re_w, re_b, e_wg, e_wu, e_wd, ple_proj,
           ple_gate_w, ple_gate_b, ln2_g, ln2_b):
    T, D = x2.shape
    alpha = (2.0 * depth) ** 0.25
    w = _split_w_in(w_in, D)
    wtok = jnp.concatenate([w["k_cmp"], w["v_cmp"], w["k_slc"], w["k_win"],
                            w["conv_b"], w["conv_c"], w["conv_h"]], axis=1).astype(BF16)
    gcols = w["g_nsa"].reshape(D, N_GROUPS_KV, HEADS_PER_GROUP * N_BRANCH)
    gcols = jnp.pad(gcols, ((0, 0), (0, 0), (0, GATE_ROWS - HEADS_PER_GROUP * N_BRANCH)))
    wt = jnp.concatenate([w["q"], gcols.reshape(D, N_GROUPS_KV * GATE_ROWS), w["v_slc"], w["v_win"]],
                         axis=1).astype(BF16).T
    kcmp, vcmp, kslc, kwin, cu, qt, gt, vslct, vwint = _in_proj(x2, wtok, wt, conv_w, seq)

    half = CMP_BLOCK // 2
    n_chunks = seq // CMP_STRIDE
    n_cmp = (seq - CMP_BLOCK) // CMP_STRIDE + 1
    hidden = cmp_w1.shape[-1]
    eye = jnp.eye(N_GROUPS_KV, dtype=F32)
    w1r = cmp_w1.reshape(2, CMP_BLOCK, HEAD_DIM, hidden)
    expand = lambda m: jnp.einsum("ildh,gk->ilgdkh", m, eye).reshape(
        2, half * KV_DIM, N_GROUPS_KV * hidden).astype(BF16)
    wa, wb = expand(w1r[:, :half]), expand(w1r[:, half:])
    pe = jnp.broadcast_to(cmp_pe.reshape(2, 1, CMP_BLOCK * HEAD_DIM), (2, 8, CMP_BLOCK * HEAD_DIM)).astype(BF16)
    b1t = jnp.tile(cmp_b1.reshape(2, 1, hidden), (1, 1, N_GROUPS_KV))
    w2b = jnp.einsum("ihd,gk->ighkd", cmp_w2, eye).reshape(2, N_GROUPS_KV * hidden, KV_DIM).astype(BF16)
    b2t = jnp.tile(cmp_b2.reshape(2, 1, HEAD_DIM), (1, 1, N_GROUPS_KV))
    kc, vct = _compress(kcmp.reshape(B, n_chunks, CMP_STRIDE * KV_DIM),
                        vcmp.reshape(B, n_chunks, CMP_STRIDE * KV_DIM),
                        wa, wb, pe, cmp_w1.astype(BF16), b1t, w2b, b2t)

    n_sel = seq // SEL_BLOCK
    c_start = np.arange(n_chunks) * CMP_STRIDE
    s_start = np.arange(n_sel) * SEL_BLOCK
    overlap = ((c_start[None, :] <= s_start[:, None] + SEL_BLOCK - 1)
               & (c_start[None, :] + CMP_BLOCK - 1 >= s_start[:, None])).astype(np.float32)
    ocmpt, selb = _cmp_attn(qt, kc, vct, gt, jnp.asarray(overlap, BF16), B, seq, n_cmp)
    ot = _nsa_attn(qt, kslc, kwin, vslct, vwint, selb, gt, ocmpt, B, seq)

    wgm = jnp.concatenate([w["g_m_nsa"], w["g_m_conv"]], axis=1).astype(BF16)
    merge_tile = 1024
    wrt = jnp.pad(jnp.concatenate([re_w, rg_w], axis=1).T, ((0, LANES - N_EXPERTS - N_EXPERT_GROUPS), (0, 0)))
    brt = jnp.pad(jnp.concatenate([re_b, rg_b]), (0, LANES - N_EXPERTS - N_EXPERT_GROUPS))
    brt = jnp.broadcast_to(brt[:, None], (LANES, merge_tile))
    h1, route_t, cnt_tiles = _merge(ot, cu, x2, w_nsa_out.astype(BF16), w_conv_out.astype(BF16), wgm,
                                    w_o.astype(BF16), ln1_g.reshape(1, D), ln1_b.reshape(1, D),
                                    wrt.astype(BF16), brt, alpha)
    n_tok_tiles = T // MOE_TILE
    max_rows = 2 * T + n_tok_tiles * N_EXPERTS * (CHUNK - 1) + N_EXPERTS * (EXPERT_TILE - 1)
    n_expert_tiles = -(-max_rows // EXPERT_TILE)
    tab, tails, tile_expert, n_used = _moe_plan(cnt_tiles, n_expert_tiles)
    xs = _dispatch(tab, tails, n_used, h1, route_t, n_expert_tiles * EXPERT_TILE)
    ys = _experts(tile_expert, n_used, xs, e_wg, e_wu, e_wd)
    return _combine(tab, route_t, h1, p2, ple_proj.astype(BF16), ple_gate_w.astype(BF16),
                    ple_gate_b.reshape(1, D), ln2_g.reshape(1, D), ln2_b.reshape(1, D), ys, alpha)


def kernel(x, p, w_in, cmp_pe, cmp_w1, cmp_b1, cmp_w2, cmp_b2, conv_w, w_nsa_out, w_conv_out, w_o, ln1_g, ln1_b, router_group_w, router_group_b, router_expert_w, router_expert_b, expert_w_gate, expert_w_up, expert_w_down, ple_proj, ple_gate_w, ple_gate_b, ln2_g, ln2_b):
    B, seq, D = x.shape
    depth = w_in.shape[0]
    x2 = x.reshape(B * seq, D)
    for i in range(depth):
        x2 = _layer(x2, p[i].reshape(B * seq, -1), B, seq, depth, w_in[i], cmp_pe[i], cmp_w1[i], cmp_b1[i],
                    cmp_w2[i], cmp_b2[i], conv_w[i], w_nsa_out[i], w_conv_out[i], w_o[i], ln1_g[i], ln1_b[i],
                    router_group_w[i], router_group_b[i], router_expert_w[i], router_expert_b[i],
                    expert_w_gate[i], expert_w_up[i], expert_w_down[i], ple_proj[i], ple_gate_w[i],
                    ple_gate_b[i], ln2_g[i], ln2_b[i])
    return x2.reshape(B, seq, D)
```

```python
import functools

import jax
import jax.numpy as jnp
import numpy as np
from jax import lax
from jax.experimental import pallas as pl
from jax.experimental.pallas import tpu as pltpu

F32 = jnp.float32
BF16 = jnp.bfloat16

N_HEADS = 8
N_GROUPS_KV = 2
HEADS_PER_GROUP = N_HEADS // N_GROUPS_KV
HEAD_DIM = 64
Q_DIM = N_HEADS * HEAD_DIM
KV_DIM = N_GROUPS_KV * HEAD_DIM
N_BRANCH = 3
CMP_BLOCK = 32
CMP_STRIDE = 16
SEL_BLOCK = 64
SEL_TOPK = 16
WINDOW = 512
CONV_DIM = 512
N_EXPERT_GROUPS = 4
EXPERTS_PER_GROUP = 8
N_EXPERTS = N_EXPERT_GROUPS * EXPERTS_PER_GROUP
ATTN_SCALE = HEAD_DIM ** -0.5
LOG2_E = 1.4426950408889634
FORCE_BONUS = 1e4
NEG_INF = -1e30
LN_EPS = 1e-5

LANES = 128
ATT_TILE = 256
GATE_ROWS = 16
SUM_ROWS = 16
VMEM_LIMIT = 48 * 1024 * 1024

MOE_TILE = 256
EXPERT_TILE = 1024
CHUNK = 16
LOCAL_ROWS = -(-(2 * MOE_TILE + N_EXPERTS * (CHUNK - 1)) // ATT_TILE) * ATT_TILE
MAX_CHUNKS = LOCAL_ROWS // CHUNK
TAB_COUNT = LANES - 1


def _cparams(n_axes):
    return pltpu.CompilerParams(dimension_semantics=("arbitrary",) * n_axes,
                                vmem_limit_bytes=VMEM_LIMIT)


def _dot(a, b):
    return jnp.dot(a, b, preferred_element_type=F32)


def _dot_nt(a, b):
    return lax.dot_general(a, b, (((1,), (1,)), ((), ())), preferred_element_type=F32)


def _dot_tn(a, b):
    return lax.dot_general(a, b, (((0,), (0,)), ((), ())), preferred_element_type=F32)


def _sigmoid(v):
    return 1.0 / (1.0 + jnp.exp(-v))


def _layer_norm(v, g, b):
    mu = jnp.mean(v, axis=-1, keepdims=True)
    d = v - mu
    var = jnp.mean(d * d, axis=-1, keepdims=True)
    return d * lax.rsqrt(var + LN_EPS) * g + b


def _inproj_kernel(seq, x_ref, xprev_ref, wtok_ref, wt_ref, convw_ref,
                   kcmp_ref, vcmp_ref, kslc_ref, kwin_ref, cu_ref, qt_ref, gt_ref, vslct_ref, vwint_ref, cmp_s):
    i = pl.program_id(0)
    tm = x_ref.shape[0]
    xb = x_ref[...].astype(BF16)

    c0 = 4 * KV_DIM
    kvf = _dot(xb, wtok_ref[:, 0:c0])
    kv = kvf.astype(BF16)
    cmp_s[0] = kvf[:, 0:KV_DIM]
    cmp_s[1] = kvf[:, KV_DIM:2 * KV_DIM]
    for l in range(CMP_STRIDE):
        rows = pl.ds(l, tm // CMP_STRIDE, stride=CMP_STRIDE)
        kcmp_ref[:, l * KV_DIM:(l + 1) * KV_DIM] = cmp_s[0, rows, :].astype(BF16)
        vcmp_ref[:, l * KV_DIM:(l + 1) * KV_DIM] = cmp_s[1, rows, :].astype(BF16)
    kwin_ref[...] = kv[:, 3 * KV_DIM:4 * KV_DIM]
    n_sel = seq // SEL_BLOCK
    blk = ((i * tm + lax.broadcasted_iota(jnp.int32, (tm, KV_DIM), 0)) // SEL_BLOCK) % n_sel
    onehot = jnp.where(lax.broadcasted_iota(jnp.int32, (tm, KV_DIM), 1) == blk, 1.0, 0.0).astype(BF16)
    kslc_ref[...] = jnp.concatenate([kv[:, 2 * KV_DIM:3 * KV_DIM], onehot], axis=1)
    conv = _dot(xb, wtok_ref[:, c0:c0 + 3 * CONV_DIM])
    cb = conv[:, 0:CONV_DIM]
    u = conv[:, CONV_DIM:2 * CONV_DIM] * conv[:, 2 * CONV_DIM:3 * CONV_DIM]
    xpb = xprev_ref[...].astype(BF16)
    up = (_dot(xpb, wtok_ref[:, c0 + CONV_DIM:c0 + 2 * CONV_DIM])
          * _dot(xpb, wtok_ref[:, c0 + 2 * CONV_DIM:c0 + 3 * CONV_DIM]))
    up = jnp.where(i % (seq // tm) == 0, 0.0, up)
    row = lax.broadcasted_iota(jnp.int32, (tm, CONV_DIM), 0)
    u1 = jnp.where(row == 0, up[7:8, :], pltpu.roll(u, 1, 0))
    u2 = jnp.where(row == 0, up[6:7, :], jnp.where(row == 1, up[7:8, :], pltpu.roll(u, 2, 0)))
    w = convw_ref[...]
    uc = w[0:1, :] * u2 + w[1:2, :] * u1 + w[2:3, :] * u
    cu_ref[...] = (cb * uc).astype(BF16)

    zt = _dot_nt(wt_ref[...], xb)
    qt_ref[...] = (zt[0:Q_DIM, :] * (ATTN_SCALE * LOG2_E)).astype(BF16)
    r0 = Q_DIM
    gt_ref[...] = _sigmoid(zt[r0:r0 + 2 * GATE_ROWS, :])
    r0 += 2 * GATE_ROWS
    vs = zt[r0:r0 + KV_DIM, :].astype(BF16)
    vw = zt[r0 + KV_DIM:r0 + 2 * KV_DIM, :].astype(BF16)
    for c in range(tm // ATT_TILE):
        vslct_ref[c] = vs[:, c * ATT_TILE:(c + 1) * ATT_TILE]
        vwint_ref[c] = vw[:, c * ATT_TILE:(c + 1) * ATT_TILE]


def _in_proj(x2, wtok, wt, conv_w, seq):
    T, D = x2.shape
    tm = 1024
    nt = T // tm
    n_tok = wtok.shape[1]
    n_t = wt.shape[0]
    row_blk = lambda n: pl.BlockSpec((tm, n), lambda i: (i, 0))
    out_shape = (
        jax.ShapeDtypeStruct((T // CMP_STRIDE, CMP_STRIDE * KV_DIM), BF16),
        jax.ShapeDtypeStruct((T // CMP_STRIDE, CMP_STRIDE * KV_DIM), BF16),
        jax.ShapeDtypeStruct((T, 2 * KV_DIM), BF16),
        jax.ShapeDtypeStruct((T, KV_DIM), BF16),
        jax.ShapeDtypeStruct((T, CONV_DIM), BF16),
        jax.ShapeDtypeStruct((Q_DIM, T), BF16),
        jax.ShapeDtypeStruct((2 * GATE_ROWS, T), F32),
        jax.ShapeDtypeStruct((T // ATT_TILE, KV_DIM, ATT_TILE), BF16),
        jax.ShapeDtypeStruct((T // ATT_TILE, KV_DIM, ATT_TILE), BF16),
    )
    vt_blk = pl.BlockSpec((tm // ATT_TILE, KV_DIM, ATT_TILE), lambda i: (i, 0, 0))
    chunk_blk = pl.BlockSpec((tm // CMP_STRIDE, CMP_STRIDE * KV_DIM), lambda i: (i, 0))
    return pl.pallas_call(
        functools.partial(_inproj_kernel, seq),
        grid=(nt,),
        in_specs=[
            row_blk(D),
            pl.BlockSpec((8, D), lambda i: (jnp.maximum(i * (tm // 8) - 1, 0), 0)),
            pl.BlockSpec((D, n_tok), lambda i: (0, 0)),
            pl.BlockSpec((n_t, D), lambda i: (0, 0)),
            pl.BlockSpec((3, CONV_DIM), lambda i: (0, 0)),
        ],
        out_specs=(
            chunk_blk, chunk_blk, row_blk(2 * KV_DIM), row_blk(KV_DIM), row_blk(CONV_DIM),
            pl.BlockSpec((Q_DIM, tm), lambda i: (0, i)),
            pl.BlockSpec((2 * GATE_ROWS, tm), lambda i: (0, i)),
            vt_blk, vt_blk,
        ),
        out_shape=out_shape,
        scratch_shapes=[pltpu.VMEM((2, tm, KV_DIM), F32)],
        compiler_params=_cparams(1),
        name="in_proj",
    )(x2, x2, wtok, wt, conv_w)


def _compress_kernel(kin_ref, vin_ref, wa_ref, wb_ref, pe_ref, w1_ref, b1_ref, w2_ref, b2_ref,
                     kc_ref, vct_ref):
    def one(idx, in_ref):
        c = in_ref[0]
        a = _dot(c, wa_ref[idx])
        b = _dot(c, wb_ref[idx])
        n = b.shape[0]
        peb = _dot(pe_ref[idx], w1_ref[idx])[0:1, :]
        bias = jnp.concatenate([peb, peb], axis=1) + b1_ref[idx]
        h = a + pltpu.roll(b, n - 1, 0) + bias
        return _dot(jax.nn.gelu(h).astype(BF16), w2_ref[idx]) + b2_ref[idx]

    kc_ref[0] = one(0, kin_ref).astype(BF16)
    vct_ref[0] = one(1, vin_ref).T.astype(BF16)


def _compress(kcmp3, vcmp3, wa, wb, pe, w1, b1, w2, b2):
    B, nch, width = kcmp3.shape
    hid2 = wa.shape[2]
    full = lambda a: pl.BlockSpec(a.shape, lambda b: (0,) * a.ndim)
    in_blk = pl.BlockSpec((1, nch, width), lambda b: (b, 0, 0))
    out_blk = pl.BlockSpec((1, nch, KV_DIM), lambda b: (b, 0, 0))
    return pl.pallas_call(
        _compress_kernel,
        grid=(B,),
        in_specs=[in_blk, in_blk, full(wa), full(wb), full(pe), full(w1), full(b1), full(w2), full(b2)],
        out_specs=(out_blk, pl.BlockSpec((1, KV_DIM, nch), lambda b: (b, 0, 0))),
        out_shape=(jax.ShapeDtypeStruct((B, nch, KV_DIM), BF16),
                   jax.ShapeDtypeStruct((B, KV_DIM, nch), BF16)),
        compiler_params=_cparams(1),
        name="compress",
    )(kcmp3, vcmp3, wa, wb, pe, w1, b1, w2, b2)


def _group_q(qt_ref, hh, g_is0):
    qh = qt_ref[hh * HEAD_DIM:(hh + 1) * HEAD_DIM, :]
    z = jnp.zeros_like(qh)
    return jnp.concatenate([qh, z], axis=0) if g_is0 else jnp.concatenate([z, qh], axis=0)


def _cmpattn_kernel(n_cmp, qt_ref, kc_ref, vct_ref, gt_ref, ovt_ref, ocmpt_ref, selb_ref):
    i = pl.program_id(1)
    tq = qt_ref.shape[1]
    nc = kc_ref.shape[1]
    n_sel = ovt_ref.shape[0]
    kc = kc_ref[0]
    t_n = i * tq + lax.broadcasted_iota(jnp.int32, (nc, tq), 1)
    n_io = lax.broadcasted_iota(jnp.int32, (nc, tq), 0)
    blk_end = jnp.where(n_io < n_cmp, n_io * CMP_STRIDE + CMP_BLOCK - 1, jnp.int32(2 ** 30))
    vis = blk_end <= t_n
    j_io = lax.broadcasted_iota(jnp.int32, (n_sel, tq), 0)
    t_j = i * tq + lax.broadcasted_iota(jnp.int32, (n_sel, tq), 1)
    cur = t_j // SEL_BLOCK
    bonus = jnp.where(j_io == 0, FORCE_BONUS,
                      jnp.where(j_io == cur, FORCE_BONUS, jnp.where(j_io == cur - 1, FORCE_BONUS, 0.0)))
    valid = j_io * SEL_BLOCK <= t_j
    for g in range(N_GROUPS_KV):
        imp = jnp.zeros((n_sel, tq), F32)
        for hh in range(HEADS_PER_GROUP):
            h = g * HEADS_PER_GROUP + hh
            s = _dot(kc, _group_q(qt_ref, h, g == 0))
            s = jnp.where(vis, s, NEG_INF)
            m = jnp.max(s, axis=0, keepdims=True)
            p = jnp.where(vis, jnp.exp2(s - m), 0.0)
            l = jnp.sum(p, axis=0, keepdims=True)
            pr = (p * jnp.where(l > 0.0, 1.0 / l, 0.0)).astype(BF16)
            o = _dot(vct_ref[0, g * HEAD_DIM:(g + 1) * HEAD_DIM, :], pr)
            gate = gt_ref[g * GATE_ROWS + hh * N_BRANCH:g * GATE_ROWS + hh * N_BRANCH + 1, :]
            ocmpt_ref[h * HEAD_DIM:(h + 1) * HEAD_DIM, :] = (o * gate).astype(BF16)
            imp = imp + _dot(ovt_ref[...], pr)
        score = jnp.where(valid, imp + bonus, -FORCE_BONUS)
        cnt = jnp.zeros((n_sel, tq), F32)
        for jp in range(n_sel):
            r = score[jp:jp + 1, :]
            cnt = cnt + jnp.where(r > score, 1.0, jnp.where(r == score, jnp.where(j_io > jp, 1.0, 0.0), 0.0))
        selb_ref[0, g] = jnp.where(cnt < float(min(SEL_TOPK, n_sel)), 0.0, NEG_INF).astype(BF16)


def _cmp_attn(qt, kc, vct, gt, ovt, B, seq, n_cmp):
    T = qt.shape[1]
    tq = ATT_TILE
    nq = seq // tq
    n_sel = ovt.shape[0]
    nc = kc.shape[1]
    return pl.pallas_call(
        functools.partial(_cmpattn_kernel, n_cmp),
        grid=(B, nq),
        in_specs=[
            pl.BlockSpec((Q_DIM, tq), lambda b, i: (0, b * nq + i)),
            pl.BlockSpec((1, nc, KV_DIM), lambda b, i: (b, 0, 0)),
            pl.BlockSpec((1, KV_DIM, nc), lambda b, i: (b, 0, 0)),
            pl.BlockSpec((2 * GATE_ROWS, tq), lambda b, i: (0, b * nq + i)),
            pl.BlockSpec(ovt.shape, lambda b, i: (0, 0)),
        ],
        out_specs=(
            pl.BlockSpec((Q_DIM, tq), lambda b, i: (0, b * nq + i)),
            pl.BlockSpec((1, N_GROUPS_KV, n_sel, tq), lambda b, i: (b * nq + i, 0, 0, 0)),
        ),
        out_shape=(jax.ShapeDtypeStruct((Q_DIM, T), BF16),
                   jax.ShapeDtypeStruct((B * nq, N_GROUPS_KV, n_sel, tq), BF16)),
        compiler_params=_cparams(2),
        name="cmp_attn",
    )(qt, kc, vct, gt, ovt)


def _nsa_kernel(qt_ref, kslc_ref, kwin_ref, vslct_ref, vwint_ref, selb_ref, gt_ref, ocmpt_ref, causal_ref,
                window_ref, ot_ref, q_s, s_s, m_s, acc_s):
    i = pl.program_id(1)
    tq = qt_ref.shape[1]
    tile = ATT_TILE
    pair = 2 * tile
    n_sel = selb_ref.shape[2]
    groups = range(N_GROUPS_KV)
    gq = HEADS_PER_GROUP * HEAD_DIM

    for g in groups:
        for hh in range(HEADS_PER_GROUP):
            lanes = slice(hh * tq, (hh + 1) * tq)
            qh = qt_ref[g * gq + hh * HEAD_DIM:g * gq + (hh + 1) * HEAD_DIM, :]
            zero = jnp.zeros_like(qh)
            q_s[g, 0:KV_DIM, lanes] = jnp.concatenate([qh, zero] if g == 0 else [zero, qh], axis=0)
            q_s[g, KV_DIM:KV_DIM + n_sel, lanes] = selb_ref[0, g]
            q_s[g, KV_DIM + n_sel:2 * KV_DIM, lanes] = jnp.zeros((KV_DIM - n_sel, tq), BF16)

    def v_rows(ref, g, tiles):
        vt = jnp.concatenate([ref[j, g * HEAD_DIM:(g + 1) * HEAD_DIM, :] for j in tiles], axis=1)
        return jnp.concatenate([vt, jnp.ones((SUM_ROWS, vt.shape[1]), BF16)], axis=0)

    def sel_scores(g, jj):
        keys = kslc_ref[pl.ds(pl.multiple_of(jj * pair, pair), pair), :]
        return _dot(keys, q_s[g])

    def sel_update(g, s, jj):
        m_old = m_s[g]
        m_new = jnp.maximum(m_old, jnp.max(s, axis=0, keepdims=True))
        p = jnp.exp2(s - m_new).astype(BF16)
        acc_s[g] = jnp.exp2(m_old - m_new) * acc_s[g] + _dot(v_rows(vslct_ref, g, (2 * jj, 2 * jj + 1)), p)
        m_s[g] = m_new

    m_s[...] = jnp.full(m_s.shape, NEG_INF, F32)
    acc_s[...] = jnp.zeros(acc_s.shape, F32)
    n_full = i // 2
    for g in groups:
        s_s[g] = sel_scores(g, 0)

    def sel_body(jj, carry):
        for g in groups:
            s = s_s[g]
            s_s[g] = sel_scores(g, jj + 1)
            sel_update(g, s, jj)
        return carry

    lax.fori_loop(0, n_full, sel_body, 0)

    j0 = jnp.maximum(i - WINDOW // tile, 0)
    n_win = WINDOW + tile
    for g in groups:
        sel_update(g, s_s[g] + causal_ref[0], n_full)
        o_slc = acc_s[g, 0:HEAD_DIM, :] * (1.0 / acc_s[g, HEAD_DIM:HEAD_DIM + 1, :])
        s = _dot(kwin_ref[pl.ds(pl.multiple_of(j0 * tile, tile), n_win), :], q_s[g, 0:KV_DIM, :])
        s = s + window_ref[0]
        p = jnp.exp2(s - jnp.max(s, axis=0, keepdims=True)).astype(BF16)
        ow = _dot(v_rows(vwint_ref, g, (j0, j0 + 1, j0 + 2)), p)
        o_win = ow[0:HEAD_DIM, :] * (1.0 / ow[HEAD_DIM:HEAD_DIM + 1, :])
        for hh in range(HEADS_PER_GROUP):
            lanes = slice(hh * tq, (hh + 1) * tq)
            rows = slice(g * gq + hh * HEAD_DIM, g * gq + (hh + 1) * HEAD_DIM)
            gate = g * GATE_ROWS + hh * N_BRANCH
            o = (ocmpt_ref[rows, :].astype(F32) + gt_ref[gate + 1:gate + 2, :] * o_slc[:, lanes]
                 + gt_ref[gate + 2:gate + 3, :] * o_win[:, lanes])
            ot_ref[rows, :] = o.astype(BF16)


def _nsa_attn(qt, kslc, kwin, vslct, vwint, selb, gt, ocmpt, B, seq):
    T = qt.shape[1]
    tq = ATT_TILE
    nq = seq // tq
    ktiles = seq // ATT_TILE
    n_sel = selb.shape[2]
    wide = HEADS_PER_GROUP * tq
    n_win = WINDOW + ATT_TILE
    kmq = (lax.broadcasted_iota(jnp.int32, (1, n_win, wide), 1)
           - lax.broadcasted_iota(jnp.int32, (1, n_win, wide), 2) % tq)
    par = lax.broadcasted_iota(jnp.int32, (2, 1, 1), 0)
    causal = jnp.where(kmq[:, :2 * ATT_TILE] <= par * ATT_TILE, 0.0, NEG_INF).astype(F32)
    behind = lax.broadcasted_iota(jnp.int32, (n_win // ATT_TILE, 1, 1), 0) * ATT_TILE - kmq
    window = jnp.where((behind >= 0) & (behind < WINDOW), 0.0, NEG_INF).astype(F32)
    qblk = pl.BlockSpec((Q_DIM, tq), lambda b, i: (0, b * nq + i))
    vblk = pl.BlockSpec((ktiles, KV_DIM, ATT_TILE), lambda b, i: (b, 0, 0))
    return pl.pallas_call(
        _nsa_kernel,
        grid=(B, nq),
        in_specs=[
            qblk,
            pl.BlockSpec((seq, 2 * KV_DIM), lambda b, i: (b, 0)),
            pl.BlockSpec((seq, KV_DIM), lambda b, i: (b, 0)),
            vblk, vblk,
            pl.BlockSpec((1, N_GROUPS_KV, n_sel, tq), lambda b, i: (b * nq + i, 0, 0, 0)),
            pl.BlockSpec((N_GROUPS_KV * GATE_ROWS, tq), lambda b, i: (0, b * nq + i)),
            qblk,
            pl.BlockSpec((1, 2 * ATT_TILE, wide), lambda b, i: (i % 2, 0, 0)),
            pl.BlockSpec((1, n_win, wide), lambda b, i: (jnp.minimum(i, WINDOW // ATT_TILE), 0, 0)),
        ],
        out_specs=qblk,
        out_shape=jax.ShapeDtypeStruct((Q_DIM, T), BF16),
        scratch_shapes=[
            pltpu.VMEM((N_GROUPS_KV, 2 * KV_DIM, wide), BF16),
            pltpu.VMEM((N_GROUPS_KV, 2 * ATT_TILE, wide), F32),
            pltpu.VMEM((N_GROUPS_KV, 1, wide), F32),
            pltpu.VMEM((N_GROUPS_KV, HEAD_DIM + SUM_ROWS, wide), F32),
        ],
        compiler_params=_cparams(2),
        name="nsa_attn",
    )(qt, kslc, kwin, vslct, vwint, selb, gt, ocmpt, causal, window)


def _route(lt):
    row = lax.broadcasted_iota(jnp.int32, lt.shape, 0)
    big = 2 ** 20
    col = lambda f, v: f(v, axis=0, keepdims=True)
    grp = jnp.where((row >= N_EXPERTS) & (row < N_EXPERTS + N_EXPERT_GROUPS), lt, NEG_INF)
    gmax = col(jnp.max, grp)
    g_idx = col(jnp.min, jnp.where(grp == gmax, row, big)) - N_EXPERTS
    p_group = 1.0 / col(jnp.sum, jnp.exp(grp - gmax))
    own = jnp.where((row < N_EXPERTS) & (row // EXPERTS_PER_GROUP == g_idx), lt, NEG_INF)
    l1 = col(jnp.max, own)
    e1 = col(jnp.min, jnp.where(own == l1, row, big))
    rest = jnp.where(row == e1, NEG_INF, own)
    l2 = col(jnp.max, rest)
    e2 = col(jnp.min, jnp.where(rest == l2, row, big))
    r = jnp.exp(l2 - l1)
    c1 = p_group / (1.0 + r)
    return e1, e2, c1, c1 * r


R_SLOT1, R_SLOT2, R_C1, R_C2 = range(4)
ROUTE_ROWS = 40


def _merge_kernel(alpha, ot_ref, cu_ref, x_ref, wn_ref, wc_ref, wgm_ref, wo_ref, g1_ref, b1_ref,
                  wrt_ref, brt_ref, h1_ref, routet_ref, cnt_ref, lt_s):
    i = pl.program_id(0)
    tm, d = x_ref.shape
    iota = lambda shape, dim: lax.broadcasted_iota(jnp.int32, shape, dim)
    sq = (MOE_TILE, MOE_TILE)
    earlier_token = jnp.where(iota(sq, 0) < iota(sq, 1), 1.0, 0.0).astype(BF16)
    lower_expert = jnp.where(iota((LANES, LANES), 0) < iota((LANES, LANES), 1), 1.0, 0.0).astype(BF16)

    @pl.when(i == 0)
    def _():
        lt_s[...] = jnp.zeros(lt_s.shape, F32)

    e1, e2, c1, c2 = _route(lt_s[...])
    row = iota((LANES, tm), 0)
    pick1 = row == e1
    pick2 = row == e2
    one = lambda m: jnp.where(m, 1.0, 0.0).astype(BF16)
    ones8 = jnp.ones((8, MOE_TILE), BF16)
    subs = [slice(k * MOE_TILE, (k + 1) * MOE_TILE) for k in range(tm // MOE_TILE)]
    oh1 = [one(pick1[:, c]) for c in subs]
    oh2 = [one(pick2[:, c]) for c in subs]
    oh = [a + b for a, b in zip(oh1, oh2)]

    x = x_ref[...]
    xb = x.astype(BF16)
    y_nsa = _dot_tn(ot_ref[...], wn_ref[...])
    y_conv = _dot(cu_ref[...], wc_ref[...])
    merged = (_sigmoid(_dot(xb, wgm_ref[:, 0:d])) * y_nsa
              + _sigmoid(_dot(xb, wgm_ref[:, d:2 * d])) * y_conv)
    mix = _dot(merged.astype(BF16), wo_ref[...])

    before = [_dot(o, earlier_token) for o in oh]
    cnt = [_dot_nt(ones8, o) for o in oh]
    chunks = [jnp.floor((c + (CHUNK - 1.0)) * (1.0 / CHUNK)).astype(BF16) for c in cnt]
    run_start = [(_dot(c, lower_expert) * CHUNK).astype(BF16) for c in chunks]
    start1 = [_dot(r, o)[0:1, :] for r, o in zip(run_start, oh1)]
    start2 = [_dot(r, o)[0:1, :] for r, o in zip(run_start, oh2)]
    recs = []
    for k, c in enumerate(subs):
        slot1 = start1[k] + jnp.sum(jnp.where(pick1[:, c], before[k], 0.0), axis=0, keepdims=True)
        slot2 = start2[k] + jnp.sum(jnp.where(pick2[:, c], before[k], 0.0), axis=0, keepdims=True)
        recs.append(jnp.concatenate([slot1, slot2, c1[:, c], c2[:, c], jnp.zeros((4, MOE_TILE), F32)], axis=0))
        cnt_ref[k] = cnt[k]
    routet_ref[...] = jnp.concatenate(recs, axis=1)

    h1 = _layer_norm(alpha * x + mix, g1_ref[...], b1_ref[...])
    h1_ref[...] = h1
    lt_s[...] = (_dot_nt(wrt_ref[...], h1.astype(BF16)) + brt_ref[...])[0:ROUTE_ROWS, :]


def _merge(ot, cu, x2, wn, wc, wgm, wo, g1, b1, wrt, brt, alpha):
    T, D = x2.shape
    tm = brt.shape[1]
    sub = tm // MOE_TILE
    nt = T // tm
    full = lambda a: pl.BlockSpec(a.shape, lambda i: (0,) * a.ndim)
    cur = lambda i: jnp.minimum(i, nt - 1)
    prev = lambda i: jnp.maximum(i - 1, 0)
    return pl.pallas_call(
        functools.partial(_merge_kernel, alpha),
        grid=(nt + 1,),
        in_specs=[
            pl.BlockSpec((Q_DIM, tm), lambda i: (0, cur(i))),
            pl.BlockSpec((tm, CONV_DIM), lambda i: (cur(i), 0)),
            pl.BlockSpec((tm, D), lambda i: (cur(i), 0)),
            full(wn), full(wc), full(wgm), full(wo), full(g1), full(b1), full(wrt), full(brt),
        ],
        out_specs=(pl.BlockSpec((tm, D), lambda i: (cur(i), 0)),
                   pl.BlockSpec((8, tm), lambda i: (0, prev(i))),
                   pl.BlockSpec((sub, 8, LANES), lambda i: (prev(i), 0, 0))),
        out_shape=(jax.ShapeDtypeStruct((T, D), F32),
                   jax.ShapeDtypeStruct((8, T), F32),
                   jax.ShapeDtypeStruct((T // MOE_TILE, 8, LANES), F32)),
        scratch_shapes=[pltpu.VMEM((ROUTE_ROWS, tm), F32)],
        compiler_params=_cparams(1),
        name="merge",
    )(ot, cu, x2, wn, wc, wgm, wo, g1, b1, wrt, brt)


def _chunk_rows(q):
    return pl.ds(pl.multiple_of(q * CHUNK, CHUNK), CHUNK)


def _start_chunks(n, chunk_copy):
    def pair(k, c):
        chunk_copy(2 * k).start(priority=0)

        @pl.when(2 * k + 1 < n)
        def _():
            chunk_copy(2 * k + 1).start(priority=1)
        return c

    lax.fori_loop(0, (n + 1) // 2, pair, 0)


def _wait_chunks(n, copy_of_rows):
    size = MAX_CHUNKS
    while size >= 1:
        @pl.when((n & size) != 0)
        def _(size=size):
            copy_of_rows(size * CHUNK).wait()
        size //= 2


def _dispatch_kernel(tab_ref, tab_m1_ref, tab_m2_ref, tails_ref, nused_ref, h1_ref, routet_ref, xs_ref,
                     buf, zero_s, sem):
    i = pl.program_id(0)
    last = pl.num_programs(0) - 1
    tm = h1_ref.shape[0]
    slot = i % 2

    def drain(tab, slot_):
        _wait_chunks(tab[0, 0, TAB_COUNT],
                     lambda rows: pltpu.make_async_copy(buf.at[slot_, pl.ds(0, rows)], xs_ref.at[pl.ds(0, rows)],
                                                        sem.at[slot_]))

    s1 = routet_ref[R_SLOT1:R_SLOT1 + 1, :].astype(jnp.int32)
    s2 = routet_ref[R_SLOT2:R_SLOT2 + 1, :].astype(jnp.int32)
    r_io = lax.broadcasted_iota(jnp.int32, (LOCAL_ROWS, tm), 0)
    perm = jnp.where(r_io == s1, 1.0, jnp.where(r_io == s2, 1.0, 0.0)).astype(BF16)
    srt = _dot(perm, h1_ref[...].astype(BF16)).astype(BF16)

    @pl.when(i >= 2)
    def _():
        drain(tab_m2_ref, slot)

    buf[slot] = srt

    def chunk_copy(q):
        dst = pl.multiple_of(tab_ref[0, 0, q], CHUNK)
        return pltpu.make_async_copy(buf.at[slot, _chunk_rows(q)], xs_ref.at[pl.ds(dst, CHUNK)], sem.at[slot])

    _start_chunks(tab_ref[0, 0, TAB_COUNT], chunk_copy)

    @pl.when(i == last)
    def _():
        drain(tab_ref, slot)

    @pl.when(jnp.logical_and(i == last, i >= 1))
    def _():
        drain(tab_m1_ref, 1 - slot)

    @pl.when(i == last)
    def _():
        zero_s[...] = jnp.zeros(zero_s.shape, zero_s.dtype)

        def tail_copy(q):
            dst = pl.multiple_of(jnp.maximum(tails_ref[q], 0), CHUNK)
            return pltpu.make_async_copy(zero_s.at[pl.ds(0, CHUNK)], xs_ref.at[pl.ds(dst, CHUNK)], sem.at[2])

        def tile_copy(j):
            dst = pl.multiple_of(j * EXPERT_TILE, EXPERT_TILE)
            return pltpu.make_async_copy(zero_s, xs_ref.at[pl.ds(dst, EXPERT_TILE)], sem.at[2])

        def start_tail(q, c):
            @pl.when(tails_ref[q] >= 0)
            def _():
                tail_copy(q).start()
            return c

        def wait_tail(q, c):
            @pl.when(tails_ref[q] >= 0)
            def _():
                tail_copy(q).wait()
            return c

        def start_tile(j, c):
            tile_copy(j).start()
            return c

        def wait_tile(j, c):
            tile_copy(j).wait()
            return c

        n_tiles = xs_ref.shape[0] // EXPERT_TILE
        lax.fori_loop(0, tails_ref.shape[0], start_tail, 0)
        lax.fori_loop(nused_ref[0], n_tiles, start_tile, 0)
        lax.fori_loop(0, tails_ref.shape[0], wait_tail, 0)
        lax.fori_loop(nused_ref[0], n_tiles, wait_tile, 0)


def _dispatch(tab, tails, n_used, h1, route_t, n_rows):
    T, D = h1.shape
    tm = MOE_TILE
    tab_blk = lambda back: pl.BlockSpec((1, 1, LANES), lambda i: (jnp.maximum(i - back, 0), 0, 0),
                                        memory_space=pltpu.SMEM)
    return pl.pallas_call(
        _dispatch_kernel,
        grid=(T // tm,),
        in_specs=[
            tab_blk(0), tab_blk(1), tab_blk(2),
            pl.BlockSpec(memory_space=pltpu.SMEM),
            pl.BlockSpec(memory_space=pltpu.SMEM),
            pl.BlockSpec((tm, D), lambda i: (i, 0)),
            pl.BlockSpec((8, tm), lambda i: (0, i)),
        ],
        out_specs=pl.BlockSpec(memory_space=pl.ANY),
        out_shape=jax.ShapeDtypeStruct((n_rows, D), BF16),
        scratch_shapes=[pltpu.VMEM((2, LOCAL_ROWS, D), BF16), pltpu.VMEM((EXPERT_TILE, D), BF16),
                        pltpu.SemaphoreType.DMA((3,))],
        compiler_params=_cparams(1),
        name="moe_dispatch",
    )(tab, tab, tab, tails, n_used, h1, route_t)


def _experts_kernel(te_ref, nu_ref, xs_ref, wg_ref, wu_ref, wd_ref, ys_ref, h_s):
    del te_ref
    j = pl.program_id(0)
    n_used = nu_ref[0]

    def down():
        ys_ref[...] = _dot(h_s[...], wd_ref[0].astype(BF16)).astype(BF16)

    def gate_up():
        xb = xs_ref[...]
        hg = _dot(xb, wg_ref[0].astype(BF16))
        h_s[...] = (hg * _sigmoid(hg) * _dot(xb, wu_ref[0].astype(BF16))).astype(BF16)

    @pl.when(j == 0)
    def _():
        gate_up()
        ys_ref[...] = jnp.zeros(ys_ref.shape, ys_ref.dtype)

    @pl.when(jnp.logical_and(j >= 1, j < n_used))
    def _():
        h_prev = h_s[...]
        xb = xs_ref[...]
        hg = _dot(xb, wg_ref[0].astype(BF16))
        hu = _dot(xb, wu_ref[0].astype(BF16))
        ys_ref[...] = _dot(h_prev, wd_ref[0].astype(BF16)).astype(BF16)
        h_s[...] = (hg * _sigmoid(hg) * hu).astype(BF16)

    @pl.when(j == n_used)
    def _():
        down()


def _experts(tile_expert, n_used, xs, wg, wu, wd):
    n_rows, D = xs.shape
    hid = wg.shape[2]
    tm = EXPERT_TILE
    n_tiles = n_rows // tm
    cur = lambda j, nu: jnp.minimum(j, nu[0] - 1)
    prev = lambda j, nu: jnp.clip(j - 1, 0, nu[0] - 1)
    grid_spec = pltpu.PrefetchScalarGridSpec(
        num_scalar_prefetch=2,
        grid=(n_tiles + 1,),
        in_specs=[
            pl.BlockSpec((tm, D), lambda j, te, nu: (cur(j, nu), 0)),
            pl.BlockSpec((1, D, hid), lambda j, te, nu: (te[cur(j, nu)], 0, 0)),
            pl.BlockSpec((1, D, hid), lambda j, te, nu: (te[cur(j, nu)], 0, 0)),
            pl.BlockSpec((1, hid, D), lambda j, te, nu: (te[prev(j, nu)], 0, 0)),
        ],
        out_specs=pl.BlockSpec((tm, D), lambda j, te, nu: (prev(j, nu), 0)),
        scratch_shapes=[pltpu.VMEM((tm, hid), BF16)],
    )
    return pl.pallas_call(
        _experts_kernel,
        grid_spec=grid_spec,
        out_shape=jax.ShapeDtypeStruct((n_rows, D), BF16),
        input_output_aliases={2: 0},
        compiler_params=_cparams(1),
        name="moe_experts",
    )(tile_expert, n_used, xs, wg, wu, wd)


def _combine_kernel(alpha, tab_ref, tab_next_ref, routet_ref, h1_ref, p_ref, pproj_ref, pgw_ref,
                    pgb_ref, g2_ref, b2_ref, ys_ref, out_ref, buf, sem):
    i = pl.program_id(0)
    tm = h1_ref.shape[0]
    slot = i % 2

    def gather(tab, slot_):
        def chunk_copy(q):
            src = pl.multiple_of(tab[0, 0, q], CHUNK)
            return pltpu.make_async_copy(ys_ref.at[pl.ds(src, CHUNK)], buf.at[slot_, _chunk_rows(q)],
                                         sem.at[slot_])
        _start_chunks(tab[0, 0, TAB_COUNT], chunk_copy)

    @pl.when(i == 0)
    def _():
        buf[...] = jnp.zeros(buf.shape, buf.dtype)
        gather(tab_ref, 0)

    @pl.when(i + 1 < pl.num_programs(0))
    def _():
        gather(tab_next_ref, 1 - slot)

    h1 = h1_ref[...]
    ple = (_sigmoid(_dot(h1.astype(BF16), pgw_ref[...]) + pgb_ref[...])
           * _dot(p_ref[...].astype(BF16), pproj_ref[...]))

    _wait_chunks(tab_ref[0, 0, TAB_COUNT],
                 lambda rows: pltpu.make_async_copy(ys_ref.at[pl.ds(0, rows)], buf.at[slot, pl.ds(0, rows)],
                                                    sem.at[slot]))

    s1 = routet_ref[R_SLOT1:R_SLOT1 + 1, :].astype(jnp.int32)
    s2 = routet_ref[R_SLOT2:R_SLOT2 + 1, :].astype(jnp.int32)
    c1 = routet_ref[R_C1:R_C1 + 1, :]
    c2 = routet_ref[R_C2:R_C2 + 1, :]
    r_io = lax.broadcasted_iota(jnp.int32, (LOCAL_ROWS, tm), 0)
    weights = jnp.where(r_io == s1, c1, jnp.where(r_io == s2, c2, 0.0)).astype(BF16)
    ffn = _dot_tn(weights, buf[slot])
    out_ref[...] = _layer_norm(alpha * h1 + ffn + ple, g2_ref[...], b2_ref[...])


def _combine(tab, route_t, h1, p2, pproj, pgw, pgb, g2, b2, ys, alpha):
    T, D = h1.shape
    tm = MOE_TILE
    nt = T // tm
    full = lambda a: pl.BlockSpec(a.shape, lambda i: (0,) * a.ndim)
    return pl.pallas_call(
        functools.partial(_combine_kernel, alpha),
        grid=(nt,),
        in_specs=[
            pl.BlockSpec((1, 1, LANES), lambda i: (i, 0, 0), memory_space=pltpu.SMEM),
            pl.BlockSpec((1, 1, LANES), lambda i: (jnp.minimum(i + 1, nt - 1), 0, 0), memory_space=pltpu.SMEM),
            pl.BlockSpec((8, tm), lambda i: (0, i)),
            pl.BlockSpec((tm, D), lambda i: (i, 0)),
            pl.BlockSpec((tm, p2.shape[1]), lambda i: (i, 0)),
            full(pproj), full(pgw), full(pgb), full(g2), full(b2),
            pl.BlockSpec(memory_space=pl.ANY),
        ],
        out_specs=pl.BlockSpec((tm, D), lambda i: (i, 0)),
        out_shape=jax.ShapeDtypeStruct((T, D), F32),
        scratch_shapes=[pltpu.VMEM((2, LOCAL_ROWS, D), BF16), pltpu.SemaphoreType.DMA((2,))],
        compiler_params=_cparams(1),
        name="moe_combine",
    )(tab, tab, route_t, h1, p2, pproj, pgw, pgb, g2, b2, ys)


def _moe_plan(cnt_tiles, n_expert_tiles):
    cnt = cnt_tiles[:, 0, :N_EXPERTS].astype(jnp.int32)
    cnt8 = (cnt + CHUNK - 1) // CHUNK * CHUNK
    lend = jnp.cumsum(cnt8, axis=1)
    lstart = lend - cnt8
    gend = jnp.cumsum(cnt8, axis=0)
    region = (gend[-1] + EXPERT_TILE - 1) // EXPERT_TILE * EXPERT_TILE
    oend = jnp.cumsum(region)
    gstart = (oend - region)[None, :] + gend - cnt8
    q8 = jnp.arange(MAX_CHUNKS, dtype=jnp.int32) * CHUNK
    eq = jnp.minimum(jnp.sum((lend[:, None, :] <= q8[None, :, None]).astype(jnp.int32), axis=-1), N_EXPERTS - 1)
    shift = jnp.sum(jnp.where(eq[:, :, None] == jnp.arange(N_EXPERTS)[None, None, :],
                              (gstart - lstart)[:, None, :], 0), axis=-1)
    dstq = shift + q8[None, :]
    nt = cnt.shape[0]
    tab = jnp.concatenate([dstq, jnp.zeros((nt, TAB_COUNT - MAX_CHUNKS), jnp.int32), lend[:, -1:] // CHUNK], axis=1)
    tile_row = jnp.arange(n_expert_tiles, dtype=jnp.int32) * EXPERT_TILE
    tile_expert = jnp.minimum(jnp.sum((oend[None, :] <= tile_row[:, None]).astype(jnp.int32), axis=1),
                              N_EXPERTS - 1)
    n_used = (oend[-1] // EXPERT_TILE).reshape(1)
    c = jnp.arange(EXPERT_TILE // CHUNK, dtype=jnp.int32)[None, :] * CHUNK
    tail_start = (oend - region + gend[-1])[:, None] + c
    tails = jnp.where(tail_start < oend[:, None], tail_start, -1).reshape(-1)
    return tab.reshape(nt, 1, LANES), tails, tile_expert, n_used


def _split_w_in(w_in, D):
    sizes = [Q_DIM] + [KV_DIM] * 6 + [N_HEADS * N_BRANCH] + [CONV_DIM] * 3 + [D] * 2
    offs = np.concatenate([[0], np.cumsum(sizes)])
    names = ["q", "k_cmp", "v_cmp", "k_slc", "v_slc", "k_win", "v_win", "g_nsa",
             "conv_b", "conv_c", "conv_h", "g_m_nsa", "g_m_conv"]
    return {n: w_in[:, int(offs[k]):int(offs[k + 1])] for k, n in enumerate(names)}


def _layer(x2, p2, B, seq, depth, w_in, cmp_pe, cmp_w1, cmp_b1, cmp_w2, cmp_b2, conv_w, w_nsa_out,
           w_conv_out, w_o, ln1_g, ln1_b, rg_w, rg_b, re_w, re_b, e_wg, e_wu, e_wd, ple_proj,
           ple_gate_w, ple_gate_b, ln2_g, ln2_b):
    T, D = x2.shape
    alpha = (2.0 * depth) ** 0.25
    w = _split_w_in(w_in, D)
    wtok = jnp.concatenate([w["k_cmp"], w["v_cmp"], w["k_slc"], w["k_win"],
                            w["conv_b"], w["conv_c"], w["conv_h"]], axis=1).astype(BF16)
    gcols = w["g_nsa"].reshape(D, N_GROUPS_KV, HEADS_PER_GROUP * N_BRANCH)
    gcols = jnp.pad(gcols, ((0, 0), (0, 0), (0, GATE_ROWS - HEADS_PER_GROUP * N_BRANCH)))
    wt = jnp.concatenate([w["q"], gcols.reshape(D, N_GROUPS_KV * GATE_ROWS), w["v_slc"], w["v_win"]],
                         axis=1).astype(BF16).T
    kcmp, vcmp, kslc, kwin, cu, qt, gt, vslct, vwint = _in_proj(x2, wtok, wt, conv_w, seq)

    half = CMP_BLOCK // 2
    n_chunks = seq // CMP_STRIDE
    n_cmp = (seq - CMP_BLOCK) // CMP_STRIDE + 1
    hidden = cmp_w1.shape[-1]
    eye = jnp.eye(N_GROUPS_KV, dtype=F32)
    w1r = cmp_w1.reshape(2, CMP_BLOCK, HEAD_DIM, hidden)
    expand = lambda m: jnp.einsum("ildh,gk->ilgdkh", m, eye).reshape(
        2, half * KV_DIM, N_GROUPS_KV * hidden).astype(BF16)
    wa, wb = expand(w1r[:, :half]), expand(w1r[:, half:])
    pe = jnp.broadcast_to(cmp_pe.reshape(2, 1, CMP_BLOCK * HEAD_DIM), (2, 8, CMP_BLOCK * HEAD_DIM)).astype(BF16)
    b1t = jnp.tile(cmp_b1.reshape(2, 1, hidden), (1, 1, N_GROUPS_KV))
    w2b = jnp.einsum("ihd,gk->ighkd", cmp_w2, eye).reshape(2, N_GROUPS_KV * hidden, KV_DIM).astype(BF16)
    b2t = jnp.tile(cmp_b2.reshape(2, 1, HEAD_DIM), (1, 1, N_GROUPS_KV))
    kc, vct = _compress(kcmp.reshape(B, n_chunks, CMP_STRIDE * KV_DIM),
                        vcmp.reshape(B, n_chunks, CMP_STRIDE * KV_DIM),
                        wa, wb, pe, cmp_w1.astype(BF16), b1t, w2b, b2t)

    n_sel = seq // SEL_BLOCK
    c_start = np.arange(n_chunks) * CMP_STRIDE
    s_start = np.arange(n_sel) * SEL_BLOCK
    overlap = ((c_start[None, :] <= s_start[:, None] + SEL_BLOCK - 1)
               & (c_start[None, :] + CMP_BLOCK - 1 >= s_start[:, None])).astype(np.float32)
    ocmpt, selb = _cmp_attn(qt, kc, vct, gt, jnp.asarray(overlap, BF16), B, seq, n_cmp)
    ot = _nsa_attn(qt, kslc, kwin, vslct, vwint, selb, gt, ocmpt, B, seq)

    wgm = jnp.concatenate([w["g_m_nsa"], w["g_m_conv"]], axis=1).astype(BF16)
    merge_tile = 1024
    wrt = jnp.pad(jnp.concatenate([re_w, rg_w], axis=1).T, ((0, LANES - N_EXPERTS - N_EXPERT_GROUPS), (0, 0)))
    brt = jnp.pad(jnp.concatenate([re_b, rg_b]), (0, LANES - N_EXPERTS - N_EXPERT_GROUPS))
    brt = jnp.broadcast_to(brt[:, None], (LANES, merge_tile))
    h1, route_t, cnt_tiles = _merge(ot, cu, x2, w_nsa_out.astype(BF16), w_conv_out.astype(BF16), wgm,
                                    w_o.astype(BF16), ln1_g.reshape(1, D), ln1_b.reshape(1, D),
                                    wrt.astype(BF16), brt, alpha)
    n_tok_tiles = T // MOE_TILE
    max_rows = 2 * T + n_tok_tiles * N_EXPERTS * (CHUNK - 1) + N_EXPERTS * (EXPERT_TILE - 1)
    n_expert_tiles = -(-max_rows // EXPERT_TILE)
    tab, tails, tile_expert, n_used = _moe_plan(cnt_tiles, n_expert_tiles)
    xs = _dispatch(tab, tails, n_used, h1, route_t, n_expert_tiles * EXPERT_TILE)
    ys = _experts(tile_expert, n_used, xs, e_wg, e_wu, e_wd)
    return _combine(tab, route_t, h1, p2, ple_proj.astype(BF16), ple_gate_w.astype(BF16),
                    ple_gate_b.reshape(1, D), ln2_g.reshape(1, D), ln2_b.reshape(1, D), ys, alpha)


def kernel(x, p, w_in, cmp_pe, cmp_w1, cmp_b1, cmp_w2, cmp_b2, conv_w, w_nsa_out, w_conv_out, w_o, ln1_g, ln1_b, router_group_w, router_group_b, router_expert_w, router_expert_b, expert_w_gate, expert_w_up, expert_w_down, ple_proj, ple_gate_w, ple_gate_b, ln2_g, ln2_b):
    B, seq, D = x.shape
    depth = w_in.shape[0]
    x2 = x.reshape(B * seq, D)
    for i in range(depth):
        x2 = _layer(x2, p[i].reshape(B * seq, -1), B, seq, depth, w_in[i], cmp_pe[i], cmp_w1[i], cmp_b1[i],
                    cmp_w2[i], cmp_b2[i], conv_w[i], w_nsa_out[i], w_conv_out[i], w_o[i], ln1_g[i], ln1_b[i],
                    router_group_w[i], router_group_b[i], router_expert_w[i], router_expert_b[i],
                    expert_w_gate[i], expert_w_up[i], expert_w_down[i], ple_proj[i], ple_gate_w[i],
                    ple_gate_b[i], ln2_g[i], ln2_b[i])
    return x2.reshape(B, seq, D)
```
